```python
import math
import jax
import jax.numpy as jnp
from jax import lax
import numpy as np

D_MODEL = 1024
BATCH = 8
SEQ = 4096
DEPTH = 4

CHUNK = 64
Q_BLOCK = 128
N_MEM = 256
MAX_STREAM_OFFSET = 4096
LN_EPS = 1e-5
RMS_EPS = 1e-6

DA_HEAD_DIM = 64
DA_V_DIM = 2 * DA_HEAD_DIM
DA_WIDTH = D_MODEL // 2
DA_HEADS = DA_WIDTH // DA_V_DIM
DA_QK = 2 * DA_HEADS * DA_HEAD_DIM
ROPE_THETA = 10000.0

RW_HEAD_DIM = 64
RW_WIDTH = D_MODEL // 4
RW_HEADS = RW_WIDTH // RW_HEAD_DIM
RW_DECAY_RANK = 32
RW_AAA_RANK = 32
RW_GATE_RANK = 64
RW_IN = 3 * RW_WIDTH + RW_DECAY_RANK + RW_AAA_RANK + RW_GATE_RANK
RW_SPLITS = tuple(int(v) for v in np.cumsum([RW_WIDTH, RW_WIDTH, RW_WIDTH, RW_DECAY_RANK, RW_AAA_RANK]))
RW_DECAY_SCALE = math.exp(-0.5)
RW_LNX_EPS = 64e-5

S5_WIDTH = D_MODEL - DA_WIDTH - RW_WIDTH
S5_GROUP_CH = 16
S5_GROUPS = S5_WIDTH // S5_GROUP_CH
S5_STATE = 64

MIX_WIDTH = DA_WIDTH + RW_WIDTH + S5_WIDTH
IN_SIZES = (DA_QK, DA_QK, DA_WIDTH, RW_IN, S5_WIDTH)
IN_WIDTH = sum(IN_SIZES)
IN_SPLITS = tuple(int(v) for v in np.cumsum(IN_SIZES)[:-1])

MEM_HEADS = 4
MEM_HEAD_DIM = D_MODEL // MEM_HEADS

D_FF = ((8 * D_MODEL // 3 + 127) // 128) * 128
FFN_CONV = 3

ALPHA = (2.0 * DEPTH) ** 0.25
BETA = (8.0 * DEPTH) ** -0.25

kernel_name = 'hybrid_streaming_encoder_trunk'


def layer_norm(x, g, b):
    x32 = x.astype(jnp.float32)
    mu = jnp.mean(x32, axis=-1, keepdims=True)
    var = jnp.mean(jnp.square(x32 - mu), axis=-1, keepdims=True)
    return ((x32 - mu) * lax.rsqrt(var + LN_EPS) * g + b).astype(x.dtype)


def rms_norm(x, g):
    x32 = x.astype(jnp.float32)
    y = x32 * lax.rsqrt(jnp.mean(jnp.square(x32), axis=-1, keepdims=True) + RMS_EPS)
    return (y * g).astype(x.dtype)


def rope_tables(positions):
    inv_freq = ROPE_THETA ** (-jnp.arange(0, DA_HEAD_DIM, 2, dtype=jnp.float32) / DA_HEAD_DIM)
    ang = positions.astype(jnp.float32)[..., None] * inv_freq
    ang = jnp.concatenate([ang, ang], axis=-1)
    return jnp.cos(ang), jnp.sin(ang)


def apply_rope(t, cos, sin):
    c = cos[:, :, None, None, :].astype(t.dtype)
    s = sin[:, :, None, None, :].astype(t.dtype)
    half = t.shape[-1] // 2
    rot = jnp.concatenate([-t[..., half:], t[..., :half]], axis=-1)
    return t * c + rot * s


def diff_attention(pq, pk, pv, cos, sin, lam_q1, lam_k1, lam_q2, lam_k2, subln_g, lambda_init):
    bsz, seq, _ = pq.shape
    nb = seq // Q_BLOCK
    q = apply_rope(pq.reshape(bsz, seq, DA_HEADS, 2, DA_HEAD_DIM), cos, sin) * (DA_HEAD_DIM ** -0.5)
    k = apply_rope(pk.reshape(bsz, seq, DA_HEADS, 2, DA_HEAD_DIM), cos, sin)
    q = q.transpose(3, 0, 2, 1, 4)
    k = k.transpose(3, 0, 2, 1, 4)
    v = pv.reshape(bsz, seq, DA_HEADS, DA_V_DIM).transpose(0, 2, 1, 3)
    lam = (jnp.exp(jnp.sum((lam_q1 * lam_k1).astype(jnp.float32)))
           - jnp.exp(jnp.sum((lam_q2 * lam_k2).astype(jnp.float32))) + lambda_init)
    q_blocks = q.reshape(2, bsz, DA_HEADS, nb, Q_BLOCK, DA_HEAD_DIM).transpose(3, 0, 1, 2, 4, 5)
    key_chunk = jnp.arange(seq) // CHUNK

    def block(args):
        i, qb = args
        q_chunk = (i * Q_BLOCK + jnp.arange(Q_BLOCK)) // CHUNK
        mask = key_chunk[None, :] <= q_chunk[:, None]
        s = jnp.einsum('nbhqd,nbhkd->nbhqk', qb, k).astype(jnp.float32)
        p = jax.nn.softmax(jnp.where(mask, s, -jnp.inf), axis=-1)
        attn = p[0] - lam * p[1]
        return jnp.einsum('bhqk,bhkv->bhqv', attn.astype(v.dtype), v)

    o = lax.map(block, (jnp.arange(nb), q_blocks))
    o = o.transpose(1, 2, 0, 3, 4).reshape(bsz, DA_HEADS, seq, DA_V_DIM)
    o = rms_norm(o, subln_g) * (1.0 - lambda_init)
    return o.transpose(0, 2, 1, 3).reshape(bsz, seq, DA_WIDTH)


def rwkv7_time_mix(p, mu, w0, w2, a0, a2, g2, k_k, k_a, r_k, lnx_g, lnx_b):
    bsz, seq, _ = p.shape
    prev = jnp.pad(p, ((0, 0), (1, 0), (0, 0)))[:, :-1]
    p = p + (prev - p) * mu
    r, k, v, wd, ad, gd = jnp.split(p, RW_SPLITS, axis=-1)

    def heads(t):
        return t.astype(jnp.float32).reshape(bsz, seq, RW_HEADS, RW_HEAD_DIM)

    decay = jnp.exp(-RW_DECAY_SCALE * jax.nn.sigmoid((w0 + jnp.tanh(wd) @ w2).astype(jnp.float32)))
    a = jax.nn.sigmoid((a0 + ad @ a2).astype(jnp.float32))
    g = jax.nn.sigmoid(gd) @ g2
    kk = heads(k * k_k)
    kk = kk * lax.rsqrt(jnp.maximum(jnp.sum(kk * kk, axis=-1, keepdims=True), 1e-24))
    k = k.astype(jnp.float32) * (1.0 + (a - 1.0) * k_a)
    r_h, k_h, v_h, w_h, a_h = heads(r), heads(k), heads(v), heads(decay), heads(a)

    def step(state, inp):
        r_t, w_t, k_t, v_t, kk_t, a_t = inp
        sa = jnp.einsum('bhvk,bhk->bhv', state, -kk_t)
        state = (state * w_t[:, :, None, :] + sa[..., None] * (kk_t * a_t)[:, :, None, :]
                 + v_t[..., None] * k_t[:, :, None, :])
        return state, jnp.einsum('bhvk,bhk->bhv', state, r_t)

    xs = tuple(jnp.moveaxis(t, 1, 0) for t in (r_h, w_h, k_h, v_h, kk, a_h))
    s0 = jnp.zeros((bsz, RW_HEADS, RW_HEAD_DIM, RW_HEAD_DIM), jnp.float32)
    _, y = lax.scan(step, s0, xs)
    y = jnp.moveaxis(y, 0, 1)
    mean = jnp.mean(y, axis=-1, keepdims=True)
    var = jnp.mean(jnp.square(y - mean), axis=-1, keepdims=True)
    y = ((y - mean) * lax.rsqrt(var + RW_LNX_EPS)).reshape(bsz, seq, RW_WIDTH) * lnx_g + lnx_b
    bonus = jnp.sum(r_h * k_h * r_k, axis=-1, keepdims=True) * v_h
    y = y + bonus.reshape(bsz, seq, RW_WIDTH)
    return (y * g).astype(p.dtype)


def s5_ssm(u, a_re, a_im, b_re, b_im, c_re, c_im, d_skip, log_step, glu_w, glu_b, out_g):
    bsz, seq, _ = u.shape
    u32 = u.astype(jnp.float32).reshape(bsz, seq, S5_GROUPS, S5_GROUP_CH)
    A = lax.complex(a_re.astype(jnp.float32), a_im.astype(jnp.float32))
    delta = jnp.exp(log_step.astype(jnp.float32))[:, None]
    a_bar = jnp.exp(delta * A)
    Bc = lax.complex(b_re.astype(jnp.float32), b_im.astype(jnp.float32))
    b_bar = ((a_bar - 1.0) / A)[..., None] * Bc
    bu = lax.complex(jnp.einsum('gph,blgh->blgp', jnp.real(b_bar), u32),
                     jnp.einsum('gph,blgh->blgp', jnp.imag(b_bar), u32))
    a_seq = jnp.broadcast_to(a_bar, bu.shape)

    def combine(e1, e2):
        a1, b1 = e1
        a2, b2 = e2
        return a1 * a2, a2 * b1 + b2

    _, states = lax.associative_scan(combine, (a_seq, bu), axis=1)
    Cc = lax.complex(c_re.astype(jnp.float32), c_im.astype(jnp.float32))
    y = jnp.real(jnp.einsum('ghp,blgp->blgh', Cc, states))
    y = (y + d_skip.astype(jnp.float32).reshape(S5_GROUPS, S5_GROUP_CH) * u32).reshape(bsz, seq, S5_WIDTH)
    z = jax.nn.gelu(y)
    z = z * jax.nn.sigmoid(z @ glu_w.astype(jnp.float32) + glu_b)
    return rms_norm(z, out_g).astype(u.dtype)


def memory_cross_attention(x, mem, wq, wkv, wo):
    bsz, seq, _ = x.shape
    q = (x @ wq).reshape(bsz, seq, MEM_HEADS, MEM_HEAD_DIM)
    k, v = jnp.split(mem @ wkv, 2, axis=-1)
    k = k.reshape(bsz, N_MEM, MEM_HEADS, MEM_HEAD_DIM)
    v = v.reshape(bsz, N_MEM, MEM_HEADS, MEM_HEAD_DIM)
    s = jnp.einsum('blhd,bmhd->bhlm', q, k).astype(jnp.float32) * (MEM_HEAD_DIM ** -0.5)
    p = jax.nn.softmax(s, axis=-1).astype(v.dtype)
    o = jnp.einsum('bhlm,bmhd->blhd', p, v).reshape(bsz, seq, D_MODEL)
    return o @ wo


def conv_ffn(x, w_up, conv_w, conv_b, w_down):
    seq = x.shape[1]
    a, g = jnp.split(x @ w_up, 2, axis=-1)
    ap = jnp.pad(a, ((0, 0), (FFN_CONV - 1, 0), (0, 0)))
    a = conv_b + sum(conv_w[j] * ap[:, j:j + seq] for j in range(FFN_CONV))
    return (jax.nn.silu(a) * g) @ w_down


def setup_inputs(seed: int = 0) -> dict:
    key = jax.random.key(seed)
    keys = iter(jax.random.split(key, 64))

    def normal(shape, scale):
        return scale * jax.random.normal(next(keys), shape, jnp.float32)

    def uniform(shape, lo, hi):
        return jax.random.uniform(next(keys), shape, jnp.float32, lo, hi)

    L = DEPTH
    n_idx = jnp.arange(S5_STATE, dtype=jnp.float32)
    positions = (jax.random.randint(next(keys), (BATCH, 1), 0, MAX_STREAM_OFFSET, dtype=jnp.int32)
                 + jnp.arange(SEQ, dtype=jnp.int32)[None, :])
    return {
        'x': normal((BATCH, SEQ, D_MODEL), 1.0),
        'mem': normal((BATCH, N_MEM, D_MODEL), 1.0),
        'positions': positions,
        'w_in': normal((L, D_MODEL, IN_WIDTH), D_MODEL ** -0.5),
        'da_lam_q1': normal((L, DA_HEAD_DIM), 0.1),
        'da_lam_k1': normal((L, DA_HEAD_DIM), 0.1),
        'da_lam_q2': normal((L, DA_HEAD_DIM), 0.1),
        'da_lam_k2': normal((L, DA_HEAD_DIM), 0.1),
        'da_subln_g': 1.0 + normal((L, DA_V_DIM), 0.02),
        'rw_mu': uniform((L, RW_IN), 0.0, 1.0),
        'rw_w0': normal((L, RW_WIDTH), 1.0),
        'rw_w2': normal((L, RW_DECAY_RANK, RW_WIDTH), 0.1 * RW_DECAY_RANK ** -0.5),
        'rw_a0': normal((L, RW_WIDTH), 0.1),
        'rw_a2': normal((L, RW_AAA_RANK, RW_WIDTH), 0.1 * RW_AAA_RANK ** -0.5),
        'rw_g2': normal((L, RW_GATE_RANK, RW_WIDTH), RW_GATE_RANK ** -0.5),
        'rw_k_k': 0.85 + normal((L, RW_WIDTH), 0.02),
        'rw_k_a': 1.0 + normal((L, RW_WIDTH), 0.02),
        'rw_r_k': normal((L, RW_HEADS, RW_HEAD_DIM), 0.1),
        'rw_lnx_g': 1.0 + normal((L, RW_WIDTH), 0.02),
        'rw_lnx_b': normal((L, RW_WIDTH), 0.02),
        's5_a_re': -0.5 + normal((L, S5_GROUPS, S5_STATE), 0.01),
        's5_a_im': math.pi * n_idx + normal((L, S5_GROUPS, S5_STATE), 0.01),
        's5_b_re': normal((L, S5_GROUPS, S5_STATE, S5_GROUP_CH), (2.0 * S5_GROUP_CH) ** -0.5),
        's5_b_im': normal((L, S5_GROUPS, S5_STATE, S5_GROUP_CH), (2.0 * S5_GROUP_CH) ** -0.5),
        's5_c_re': normal((L, S5_GROUPS, S5_GROUP_CH, S5_STATE), S5_STATE ** -0.5),
        's5_c_im': normal((L, S5_GROUPS, S5_GROUP_CH, S5_STATE), S5_STATE ** -0.5),
        's5_d': normal((L, S5_WIDTH), 1.0),
        's5_log_step': uniform((L, S5_GROUPS), math.log(1e-3), math.log(1e-1)),
        's5_glu_w': normal((L, S5_WIDTH, S5_WIDTH), S5_WIDTH ** -0.5),
        's5_glu_b': normal((L, S5_WIDTH), 0.02),
        's5_out_g': 1.0 + normal((L, S5_WIDTH), 0.02),
        'w_out': normal((L, MIX_WIDTH, D_MODEL), BETA * MIX_WIDTH ** -0.5),
        'ln1_g': 1.0 + normal((L, D_MODEL), 0.02),
        'ln1_b': normal((L, D_MODEL), 0.02),
        'ca_wq': normal((L, D_MODEL, D_MODEL), D_MODEL ** -0.5),
        'ca_wkv': normal((L, D_MODEL, 2 * D_MODEL), D_MODEL ** -0.5),
        'ca_wo': normal((L, D_MODEL, D_MODEL), BETA * D_MODEL ** -0.5),
        'ln2_g': 1.0 + normal((L, D_MODEL), 0.02),
        'ln2_b': normal((L, D_MODEL), 0.02),
        'ffn_w_up': normal((L, D_MODEL, 2 * D_FF), D_MODEL ** -0.5),
        'ffn_conv_w': normal((L, FFN_CONV, D_FF), 0.5),
        'ffn_conv_b': normal((L, D_FF), 0.02),
        'ffn_w_down': normal((L, D_FF, D_MODEL), BETA * D_FF ** -0.5),
        'ln3_g': 1.0 + normal((L, D_MODEL), 0.02),
        'ln3_b': normal((L, D_MODEL), 0.02),
    }


def reference(x, mem, positions, w_in, da_lam_q1, da_lam_k1, da_lam_q2, da_lam_k2, da_subln_g,
              rw_mu, rw_w0, rw_w2, rw_a0, rw_a2, rw_g2, rw_k_k, rw_k_a, rw_r_k, rw_lnx_g, rw_lnx_b,
              s5_a_re, s5_a_im, s5_b_re, s5_b_im, s5_c_re, s5_c_im, s5_d, s5_log_step,
              s5_glu_w, s5_glu_b, s5_out_g, w_out, ln1_g, ln1_b, ca_wq, ca_wkv, ca_wo, ln2_g, ln2_b,
              ffn_w_up, ffn_conv_w, ffn_conv_b, ffn_w_down, ln3_g, ln3_b):
    cos, sin = rope_tables(positions)
    for l in range(DEPTH):
        lambda_init = 0.8 - 0.6 * math.exp(-0.3 * l)
        p_q, p_k, p_v, p_rw, p_s5 = jnp.split(x @ w_in[l], IN_SPLITS, axis=-1)
        h_da = diff_attention(p_q, p_k, p_v, cos, sin, da_lam_q1[l], da_lam_k1[l], da_lam_q2[l],
                              da_lam_k2[l], da_subln_g[l], lambda_init)
        h_rw = rwkv7_time_mix(p_rw, rw_mu[l], rw_w0[l], rw_w2[l], rw_a0[l], rw_a2[l], rw_g2[l],
                              rw_k_k[l], rw_k_a[l], rw_r_k[l], rw_lnx_g[l], rw_lnx_b[l])
        h_s5 = s5_ssm(p_s5, s5_a_re[l], s5_a_im[l], s5_b_re[l], s5_b_im[l], s5_c_re[l], s5_c_im[l],
                      s5_d[l], s5_log_step[l], s5_glu_w[l], s5_glu_b[l], s5_out_g[l])
        h = jnp.concatenate([h_da.astype(x.dtype), h_rw.astype(x.dtype), h_s5.astype(x.dtype)], axis=-1)
        x = layer_norm(ALPHA * x + h @ w_out[l], ln1_g[l], ln1_b[l])
        h = memory_cross_attention(x, mem, ca_wq[l], ca_wkv[l], ca_wo[l])
        x = layer_norm(ALPHA * x + h, ln2_g[l], ln2_b[l])
        h = conv_ffn(x, ffn_w_up[l], ffn_conv_w[l], ffn_conv_b[l], ffn_w_down[l])
        x = layer_norm(ALPHA * x + h, ln3_g[l], ln3_b[l])
    return x
```

```python
import functools
import math

import numpy as np
import jax
import jax.numpy as jnp
from jax import lax
from jax.experimental import pallas as pl
from jax.experimental.pallas import tpu as pltpu

F32 = jnp.float32
BF16 = jnp.bfloat16
HI = lax.Precision.HIGHEST

D_MODEL = 1024
DEPTH = 4
CHUNK = 64
LN_EPS = 1e-5
RMS_EPS = 1e-6

DA_HEAD_DIM = 64
DA_V_DIM = 128
DA_WIDTH = 512
DA_HEADS = 4
ROPE_THETA = 10000.0

RW_HEAD_DIM = 64
RW_WIDTH = 256
RW_HEADS = 4
RW_DECAY_RANK = 32
RW_AAA_RANK = 32
RW_GATE_RANK = 64
RW_IN = 3 * RW_WIDTH + RW_DECAY_RANK + RW_AAA_RANK + RW_GATE_RANK
RW_DECAY_SCALE = math.exp(-0.5)
RW_LNX_EPS = 64e-5

S5_WIDTH = 256
S5_GROUP_CH = 16
S5_GROUPS = 16
S5_STATE = 64
S5_CPLX = S5_GROUPS * S5_STATE

IN_WIDTH = 3 * DA_WIDTH + RW_IN + S5_WIDTH
MEM_HEADS = 4
MEM_HEAD_DIM = 256
D_FF = 2816
ALPHA = (2.0 * DEPTH) ** 0.25

COL_Q, COL_K, COL_V, COL_S5, COL_RW = 0, 512, 1024, 1536, 1792

VMEM_LIMIT = 48 * 1024 * 1024
LANES = 128
SUBLANES = 8


def _cp(*sem):
    return pltpu.CompilerParams(dimension_semantics=sem, vmem_limit_bytes=VMEM_LIMIT)


def _layer_norm_rows(v, g, b):
    mu = jnp.mean(v, axis=-1, keepdims=True)
    c = v - mu
    var = jnp.mean(c * c, axis=-1, keepdims=True)
    return c * lax.rsqrt(var + LN_EPS) * g + b


def _mm_kernel(a_ref, w_ref, o_ref):
    o_ref[...] = jnp.dot(a_ref[...].astype(BF16), w_ref[...],
                         preferred_element_type=F32).astype(o_ref.dtype)


def _matmul(a, w, out_dtype, tm, tn):
    m, k = a.shape
    n = w.shape[1]
    return pl.pallas_call(
        _mm_kernel,
        out_shape=jax.ShapeDtypeStruct((m, n), out_dtype),
        grid=(m // tm, n // tn),
        in_specs=[pl.BlockSpec((tm, k), lambda i, j: (i, 0)),
                  pl.BlockSpec((k, tn), lambda i, j: (0, j))],
        out_specs=pl.BlockSpec((tm, tn), lambda i, j: (i, j)),
        compiler_params=_cp("parallel", "arbitrary"),
        name="matmul",
    )(a, w)


def _mm_ln_kernel(n_in, *refs):
    a_refs = refs[:n_in]
    w_refs = refs[n_in:2 * n_in]
    x_ref, g_ref, b_ref, o_ref = refs[2 * n_in:]
    acc = ALPHA * x_ref[...]
    for a_ref, w_ref in zip(a_refs, w_refs):
        acc = acc + jnp.dot(a_ref[...].astype(BF16), w_ref[...], preferred_element_type=F32)
    o_ref[...] = _layer_norm_rows(acc, g_ref[...], b_ref[...])


def _matmul_ln(a_list, w_list, x, g, b, tm):
    m, d = x.shape
    n_in = len(a_list)
    in_specs = [pl.BlockSpec((tm, a.shape[1]), lambda i: (i, 0)) for a in a_list]
    in_specs += [pl.BlockSpec(w.shape, lambda i: (0, 0)) for w in w_list]
    in_specs += [pl.BlockSpec((tm, d), lambda i: (i, 0)),
                 pl.BlockSpec((1, d), lambda i: (0, 0)),
                 pl.BlockSpec((1, d), lambda i: (0, 0))]
    return pl.pallas_call(
        functools.partial(_mm_ln_kernel, n_in),
        out_shape=jax.ShapeDtypeStruct((m, d), F32),
        grid=(m // tm,),
        in_specs=in_specs,
        out_specs=pl.BlockSpec((tm, d), lambda i: (i, 0)),
        compiler_params=_cp("parallel"),
        name="matmul_ln",
    )(*a_list, *w_list, x, g.reshape(1, d), b.reshape(1, d))


def _rope_table_kernel(pos_ref, freq_ref, sign_ref, cos_ref, sin_ref):
    ang = pos_ref[...] * freq_ref[...]
    cos_ref[...] = jnp.cos(ang)
    sin_ref[...] = jnp.sin(ang) * sign_ref[...]


def _rope_tables(positions, tm):
    t = positions.size
    inv_freq = ROPE_THETA ** (-jnp.arange(0, DA_HEAD_DIM, 2, dtype=F32) / DA_HEAD_DIM)
    freq_row = jnp.tile(inv_freq, 4).reshape(1, LANES)
    sign_row = jnp.tile(jnp.concatenate([-jnp.ones((32,), F32), jnp.ones((32,), F32)]), 2).reshape(1, LANES)
    pos_col = positions.astype(F32).reshape(t, 1)
    row = pl.BlockSpec((1, LANES), lambda i: (0, 0))
    return pl.pallas_call(
        _rope_table_kernel,
        out_shape=(jax.ShapeDtypeStruct((t, LANES), F32),) * 2,
        grid=(t // tm,),
        in_specs=[pl.BlockSpec((tm, 1), lambda i: (i, 0)), row, row],
        out_specs=(pl.BlockSpec((tm, LANES), lambda i: (i, 0)),) * 2,
        compiler_params=_cp("parallel"),
        name="rope_tables",
    )(pos_col, freq_row, sign_row)


def _rope_kernel(q_ref, k_ref, v_ref, cos_ref, sin_ref, qo_ref, ko_ref, vo_ref):
    cos = cos_ref[...]
    sin = sin_ref[...]
    lane = lax.broadcasted_iota(jnp.int32, cos.shape, 1)
    low = (lane % DA_HEAD_DIM) < (DA_HEAD_DIM // 2)

    def rope(t):
        swapped = jnp.where(low, pltpu.roll(t, LANES - 32, axis=1), pltpu.roll(t, 32, axis=1))
        return t * cos + swapped * sin

    scale = DA_HEAD_DIM ** -0.5
    for h in range(DA_HEADS):
        sl = slice(h * LANES, (h + 1) * LANES)
        qo_ref[:, sl] = (rope(q_ref[:, sl]) * scale).astype(BF16)
        ko_ref[:, sl] = rope(k_ref[:, sl]).astype(BF16)
    vo_ref[...] = v_ref[...].astype(BF16)


def _rope(proj, cos_t, sin_t, tm):
    t = proj.shape[0]
    col = lambda c: pl.BlockSpec((tm, DA_WIDTH), lambda i: (i, c))
    tab = pl.BlockSpec((tm, LANES), lambda i: (i, 0))
    out = pl.BlockSpec((tm, DA_WIDTH), lambda i: (i, 0))
    return pl.pallas_call(
        _rope_kernel,
        out_shape=(jax.ShapeDtypeStruct((t, DA_WIDTH), BF16),) * 3,
        grid=(t // tm,),
        in_specs=[col(COL_Q // DA_WIDTH), col(COL_K // DA_WIDTH), col(COL_V // DA_WIDTH), tab, tab],
        out_specs=(out,) * 3,
        compiler_params=_cp("parallel"),
        name="rope",
    )(proj, proj, proj, cos_t, sin_t)


ATT_TQ = 256
ATT_TK = 256


def _attn_kernel(lam_ref, g_ref, q_ref, k_ref, v_ref, o_ref, *, lambda_init):
    i = pl.program_id(2)
    q = q_ref[...]
    lane = lax.broadcasted_iota(jnp.int32, q.shape, 1)
    zero = jnp.zeros_like(q)
    qs = jnp.concatenate([jnp.where(lane < DA_HEAD_DIM, q, zero),
                          jnp.where(lane >= DA_HEAD_DIM, q, zero)], axis=0)

    def step(j, carry, masked):
        m, l, acc = carry
        start = pl.multiple_of(j * ATT_TK, ATT_TK)
        kb = k_ref[pl.ds(start, ATT_TK), :]
        vb = v_ref[pl.ds(start, ATT_TK), :]
        s = lax.dot_general(qs, kb, (((1,), (1,)), ((), ())), preferred_element_type=F32)
        if masked:
            qrow = lax.broadcasted_iota(jnp.int32, s.shape, 0) % ATT_TQ
            kcol = lax.broadcasted_iota(jnp.int32, s.shape, 1)
            s = jnp.where(kcol // CHUNK <= qrow // CHUNK, s, -jnp.inf)
        m_new = jnp.maximum(m, jnp.max(s, axis=-1, keepdims=True))
        alpha = jnp.exp(m - m_new)
        p = jnp.exp(s - m_new)
        l = alpha * l + jnp.sum(p, axis=-1, keepdims=True)
        acc = alpha * acc + jnp.dot(p.astype(BF16), vb, preferred_element_type=F32)
        return m_new, l, acc

    init = (jnp.full((2 * ATT_TQ, 1), -jnp.inf, F32),
            jnp.zeros((2 * ATT_TQ, 1), F32),
            jnp.zeros((2 * ATT_TQ, DA_V_DIM), F32))
    carry = lax.fori_loop(0, i, lambda j, c: step(j, c, False), init)
    m, l, acc = step(i, carry, True)
    o = acc / l
    o = o[:ATT_TQ] - lam_ref[...] * o[ATT_TQ:]
    o = o * lax.rsqrt(jnp.mean(o * o, axis=-1, keepdims=True) + RMS_EPS) * g_ref[...]
    o_ref[...] = o * (1.0 - lambda_init)


def _attention(qr, kr, vr, lam_row, subln_g, lambda_init, bsz, seq):
    nq = seq // ATT_TQ
    row = pl.BlockSpec((1, LANES), lambda b, h, i: (0, 0))
    kv = pl.BlockSpec((seq, LANES), lambda b, h, i: (b, h))
    qo = pl.BlockSpec((ATT_TQ, LANES), lambda b, h, i: (b * nq + i, h))
    return pl.pallas_call(
        functools.partial(_attn_kernel, lambda_init=lambda_init),
        out_shape=jax.ShapeDtypeStruct((bsz * seq, DA_WIDTH), F32),
        grid=(bsz, DA_HEADS, nq),
        in_specs=[row, row, qo, kv, kv],
        out_specs=qo,
        compiler_params=_cp("parallel", "parallel", "arbitrary"),
        name="diff_attention",
    )(lam_row, subln_g.reshape(1, LANES), qr, kr, vr)


def _rw_prep_kernel(p_ref, halo_ref, mu_ref, w0_ref, a0_ref, kk_ref, ka_ref, rk_ref,
                    w2_ref, a2_ref, g2_ref, seg_ref,
                    r_ref, w_ref, k_ref, an_ref, b_ref, v_ref, g_ref, bonus_ref, *, tiles_per_seq):
    i = pl.program_id(0)
    p = p_ref[...]
    row = lax.broadcasted_iota(jnp.int32, p.shape, 0)
    first = jnp.where(i % tiles_per_seq == 0, 0.0, 1.0)
    last_prev = jnp.broadcast_to(halo_ref[SUBLANES - 1:SUBLANES, :], p.shape) * first
    prev = jnp.where(row == 0, last_prev, pltpu.roll(p, 1, axis=0))
    p = p + (prev - p) * mu_ref[...]
    r = p[:, 0:RW_WIDTH]
    k = p[:, RW_WIDTH:2 * RW_WIDTH]
    v = p[:, 2 * RW_WIDTH:3 * RW_WIDTH]
    low = p[:, 3 * RW_WIDTH:RW_IN]
    dot = lambda a, b: jnp.dot(a, b, preferred_element_type=F32, precision=HI)
    decay = jnp.exp(-RW_DECAY_SCALE * jax.nn.sigmoid(w0_ref[...] + dot(jnp.tanh(low), w2_ref[...])))
    a = jax.nn.sigmoid(a0_ref[...] + dot(low, a2_ref[...]))
    g = dot(jax.nn.sigmoid(low), g2_ref[...])
    seg = seg_ref[...]
    kk = k * kk_ref[...]
    kk = kk * lax.rsqrt(jnp.maximum(dot(kk * kk, seg), 1e-24))
    k = k * (1.0 + (a - 1.0) * ka_ref[...])
    r_ref[...] = r
    w_ref[...] = decay
    k_ref[...] = k
    an_ref[...] = -kk
    b_ref[...] = kk * a
    v_ref[...] = v
    g_ref[...] = g
    bonus_ref[...] = dot(r * k * rk_ref[...], seg) * v


def _head_segments():
    idx = np.arange(RW_WIDTH) // RW_HEAD_DIM
    return jnp.asarray((idx[:, None] == idx[None, :]).astype(np.float32))


def _rw_prep(proj, mu, w0, w2, a0, a2, g2, k_k, k_a, r_k, seq, tm):
    t = proj.shape[0]
    hb = tm // SUBLANES
    w2p = jnp.zeros((LANES, RW_WIDTH), F32).at[0:32].set(w2)
    a2p = jnp.zeros((LANES, RW_WIDTH), F32).at[32:64].set(a2)
    g2p = jnp.zeros((LANES, RW_WIDTH), F32).at[64:128].set(g2)
    vec = lambda n: pl.BlockSpec((1, n), lambda i: (0, 0))
    mat = lambda r, c: pl.BlockSpec((r, c), lambda i: (0, 0))
    out = pl.BlockSpec((tm, RW_WIDTH), lambda i: (i, 0))
    return pl.pallas_call(
        functools.partial(_rw_prep_kernel, tiles_per_seq=seq // tm),
        out_shape=(jax.ShapeDtypeStruct((t, RW_WIDTH), F32),) * 8,
        grid=(t // tm,),
        in_specs=[pl.BlockSpec((tm, RW_IN), lambda i: (i, COL_RW // RW_IN)),
                  pl.BlockSpec((SUBLANES, RW_IN), lambda i: (jnp.maximum(i * hb - 1, 0), COL_RW // RW_IN)),
                  vec(RW_IN), vec(RW_WIDTH), vec(RW_WIDTH), vec(RW_WIDTH), vec(RW_WIDTH), vec(RW_WIDTH),
                  mat(LANES, RW_WIDTH), mat(LANES, RW_WIDTH), mat(LANES, RW_WIDTH), mat(RW_WIDTH, RW_WIDTH)],
        out_specs=(out,) * 8,
        compiler_params=_cp("parallel"),
        name="rwkv_prep",
    )(proj, proj, mu.reshape(1, RW_IN), w0.reshape(1, -1), a0.reshape(1, -1), k_k.reshape(1, -1),
      k_a.reshape(1, -1), r_k.reshape(1, -1), w2p, a2p, g2p, _head_segments())


RW_VLO = 4
RW_VHI = RW_HEAD_DIM // RW_VLO
RW_TT = 64
RW_NACC = 4


def _rw_scan_kernel(r_ref, w_ref, k_ref, an_ref, b_ref, v_ref, y_ref, s_ref):
    @pl.when(pl.program_id(0) == 0)
    def _():
        s_ref[...] = jnp.zeros_like(s_ref)

    nvb = RW_VHI // SUBLANES

    def bc(ref, t, kk):
        return jnp.broadcast_to(ref[t, pl.ds(kk, 1), :], (SUBLANES, LANES))

    def body(t, _):
        zacc = [[jnp.zeros((SUBLANES, LANES), F32) for _ in range(RW_NACC)] for _ in range(nvb)]
        for kk in range(RW_HEAD_DIM):
            an = bc(an_ref, t, kk)
            for vb in range(nvb):
                zacc[vb][kk % RW_NACC] += s_ref[kk, vb * SUBLANES:(vb + 1) * SUBLANES, :] * an
        z = [sum(zacc[vb][1:], zacc[vb][0]) for vb in range(nvb)]
        vt = [v_ref[t, vb * SUBLANES:(vb + 1) * SUBLANES, :] for vb in range(nvb)]
        yacc = [[jnp.zeros((SUBLANES, LANES), F32) for _ in range(RW_NACC)] for _ in range(nvb)]
        for kk in range(RW_HEAD_DIM):
            wr = bc(w_ref, t, kk)
            br = bc(b_ref, t, kk)
            kr = bc(k_ref, t, kk)
            rr = bc(r_ref, t, kk)
            for vb in range(nvb):
                sl = slice(vb * SUBLANES, (vb + 1) * SUBLANES)
                s_new = s_ref[kk, sl, :] * wr + z[vb] * br + vt[vb] * kr
                s_ref[kk, sl, :] = s_new
                yacc[vb][kk % RW_NACC] += s_new * rr
        for vb in range(nvb):
            y_ref[t, vb * SUBLANES:(vb + 1) * SUBLANES, :] = sum(yacc[vb][1:], yacc[vb][0])
        return 0

    lax.fori_loop(0, RW_TT, body, 0)


def _rw_scan(r4, w4, k4, an4, b4, v4, seq):
    kspec = pl.BlockSpec((RW_TT, RW_HEAD_DIM, LANES), lambda i: (i, 0, 0))
    vspec = pl.BlockSpec((RW_TT, RW_VHI, LANES), lambda i: (i, 0, 0))
    return pl.pallas_call(
        _rw_scan_kernel,
        out_shape=jax.ShapeDtypeStruct((seq, RW_VHI, LANES), F32),
        grid=(seq // RW_TT,),
        in_specs=[kspec] * 5 + [vspec],
        out_specs=vspec,
        scratch_shapes=[pltpu.VMEM((RW_HEAD_DIM, RW_VHI, LANES), F32)],
        compiler_params=_cp("arbitrary"),
        name="rwkv_scan",
    )(r4, w4, k4, an4, b4, v4)


def _to_scan_k(x, bsz, seq):
    x = x.reshape(bsz, seq, RW_HEADS, RW_HEAD_DIM).transpose(1, 3, 0, 2)
    x = jnp.broadcast_to(x[..., None], (seq, RW_HEAD_DIM, bsz, RW_HEADS, RW_VLO))
    return x.reshape(seq, RW_HEAD_DIM, LANES)


def _to_scan_v(x, bsz, seq):
    x = x.reshape(bsz, seq, RW_HEADS, RW_VHI, RW_VLO).transpose(1, 3, 0, 2, 4)
    return x.reshape(seq, RW_VHI, LANES)


def _from_scan_v(y, bsz, seq):
    y = y.reshape(seq, RW_VHI, bsz, RW_HEADS, RW_VLO).transpose(2, 0, 3, 1, 4)
    return y.reshape(bsz * seq, RW_WIDTH)


def _rw_post_kernel(y_ref, bonus_ref, gate_ref, lg_ref, lb_ref, seg_ref, o_ref):
    y = y_ref[...]
    seg = seg_ref[...] * (1.0 / RW_HEAD_DIM)
    dot = lambda a, b: jnp.dot(a, b, preferred_element_type=F32, precision=HI)
    c = y - dot(y, seg)
    var = dot(c * c, seg)
    y = c * lax.rsqrt(var + RW_LNX_EPS) * lg_ref[...] + lb_ref[...]
    o_ref[...] = (y + bonus_ref[...]) * gate_ref[...]


def _rw_post(y, bonus, gate, lnx_g, lnx_b, tm):
    t = y.shape[0]
    blk = pl.BlockSpec((tm, RW_WIDTH), lambda i: (i, 0))
    vec = pl.BlockSpec((1, RW_WIDTH), lambda i: (0, 0))
    return pl.pallas_call(
        _rw_post_kernel,
        out_shape=jax.ShapeDtypeStruct((t, RW_WIDTH), F32),
        grid=(t // tm,),
        in_specs=[blk, blk, blk, vec, vec, pl.BlockSpec((RW_WIDTH, RW_WIDTH), lambda i: (0, 0))],
        out_specs=blk,
        compiler_params=_cp("parallel"),
        name="rwkv_post",
    )(y, bonus, gate, lnx_g.reshape(1, -1), lnx_b.reshape(1, -1), _head_segments())


S5_TT = 256


def _s5_kernel(u_ref, wb_ref, wc_ref, pw_ref, ap_ref, d_ref, gw_ref, gb_ref, og_ref,
               o_ref, x_ref, carry_ref):
    @pl.when(pl.program_id(1) == 0)
    def _():
        carry_ref[...] = jnp.zeros_like(carry_ref)

    n = S5_CPLX
    u = u_ref[...]
    bu = jnp.dot(u.astype(BF16), wb_ref[...], preferred_element_type=F32)
    xr, xi = bu[:, :n], bu[:, n:]
    sub = lax.broadcasted_iota(jnp.int32, xr.shape, 0) % SUBLANES
    for step, s in enumerate((1, 2, 4)):
        ar = pw_ref[step:step + 1, :n]
        ai = pw_ref[step:step + 1, n:]
        keep = sub >= s
        sr = jnp.where(keep, pltpu.roll(xr, s, axis=0), 0.0)
        si = jnp.where(keep, pltpu.roll(xi, s, axis=0), 0.0)
        xr, xi = xr + (ar * sr - ai * si), xi + (ar * si + ai * sr)
    x_ref[:, :n] = xr
    x_ref[:, n:] = xi
    pr, pi = ap_ref[:, :n], ap_ref[:, n:]

    def group(gi, carry):
        cr, ci = carry
        rows = pl.ds(pl.multiple_of(gi * SUBLANES, SUBLANES), SUBLANES)
        gr = x_ref[rows, :n] + (pr * cr - pi * ci)
        gim = x_ref[rows, n:] + (pr * ci + pi * cr)
        x_ref[rows, :n] = gr
        x_ref[rows, n:] = gim
        return (jnp.broadcast_to(gr[SUBLANES - 1:SUBLANES, :], (SUBLANES, n)),
                jnp.broadcast_to(gim[SUBLANES - 1:SUBLANES, :], (SUBLANES, n)))

    cr, ci = lax.fori_loop(0, S5_TT // SUBLANES, group, (carry_ref[:, :n], carry_ref[:, n:]))
    carry_ref[:, :n] = cr
    carry_ref[:, n:] = ci
    y = jnp.dot(x_ref[...].astype(BF16), wc_ref[...], preferred_element_type=F32) + d_ref[...] * u
    z = jax.nn.gelu(y)
    z = z * jax.nn.sigmoid(jnp.dot(z.astype(BF16), gw_ref[...], preferred_element_type=F32) + gb_ref[...])
    o_ref[...] = z * lax.rsqrt(jnp.mean(z * z, axis=-1, keepdims=True) + RMS_EPS) * og_ref[...]


def _s5_tables(a_re, a_im, b_re, b_im, c_re, c_im, log_step):
    delta = jnp.exp(log_step.astype(F32))[:, None]
    mag = jnp.exp(delta * a_re)

    def a_pow(n):
        return (mag ** n) * jnp.cos(n * delta * a_im), (mag ** n) * jnp.sin(n * delta * a_im)

    abr, abi = a_pow(1)
    den = a_re * a_re + a_im * a_im
    qr = ((abr - 1.0) * a_re + abi * a_im) / den
    qi = (abi * a_re - (abr - 1.0) * a_im) / den
    bbr = qr[..., None] * b_re - qi[..., None] * b_im
    bbi = qr[..., None] * b_im + qi[..., None] * b_re
    eye = jnp.eye(S5_GROUPS, dtype=F32)
    blk_b = lambda m: jnp.einsum('gph,gk->ghkp', m, eye).reshape(S5_WIDTH, S5_CPLX)
    blk_c = lambda m: jnp.einsum('ghp,gk->gpkh', m, eye).reshape(S5_CPLX, S5_WIDTH)
    wb = jnp.concatenate([blk_b(bbr), blk_b(bbi)], axis=1).astype(BF16)
    wc = jnp.concatenate([blk_c(c_re), -blk_c(c_im)], axis=0).astype(BF16)
    flat = lambda pr: jnp.concatenate([pr[0].reshape(1, -1), pr[1].reshape(1, -1)], axis=1)
    pw = jnp.concatenate([flat(a_pow(s)) for s in (1, 2, 4)] + [jnp.zeros((5, 2 * S5_CPLX), F32)], axis=0)
    ap = jnp.concatenate([flat(a_pow(r + 1)) for r in range(SUBLANES)], axis=0)
    return wb, wc, pw, ap


def _s5(proj, tables, d_skip, glu_w, glu_b, out_g, bsz, seq):
    wb, wc, pw, ap = tables
    nt = seq // S5_TT
    full = lambda a: pl.BlockSpec(a.shape, lambda b, i: (0, 0))
    vec = pl.BlockSpec((1, S5_WIDTH), lambda b, i: (0, 0))
    gw = glu_w.astype(BF16)
    return pl.pallas_call(
        _s5_kernel,
        out_shape=jax.ShapeDtypeStruct((bsz * seq, S5_WIDTH), F32),
        grid=(bsz, nt),
        in_specs=[pl.BlockSpec((S5_TT, S5_WIDTH), lambda b, i: (b * nt + i, COL_S5 // S5_WIDTH)),
                  full(wb), full(wc), full(pw), full(ap), vec, full(gw), vec, vec],
        out_specs=pl.BlockSpec((S5_TT, S5_WIDTH), lambda b, i: (b * nt + i, 0)),
        scratch_shapes=[pltpu.VMEM((S5_TT, 2 * S5_CPLX), F32),
                        pltpu.VMEM((SUBLANES, 2 * S5_CPLX), F32)],
        compiler_params=_cp("parallel", "arbitrary"),
        name="s5",
    )(proj, wb, wc, pw, ap, d_skip.reshape(1, -1), gw, glu_b.reshape(1, -1), out_g.reshape(1, -1))


def _xattn_kernel(x_ref, kv_ref, wq_ref, wo_ref, g_ref, b_ref, o_ref):
    x = x_ref[...]
    q = jnp.dot(x.astype(BF16), wq_ref[...], preferred_element_type=F32)
    outs = []
    for h in range(MEM_HEADS):
        sl = slice(h * MEM_HEAD_DIM, (h + 1) * MEM_HEAD_DIM)
        kh = kv_ref[:, sl]
        vh = kv_ref[:, D_MODEL + h * MEM_HEAD_DIM:D_MODEL + (h + 1) * MEM_HEAD_DIM]
        s = lax.dot_general(q[:, sl].astype(BF16), kh, (((1,), (1,)), ((), ())),
                            preferred_element_type=F32) * (MEM_HEAD_DIM ** -0.5)
        p = jnp.exp(s - jnp.max(s, axis=-1, keepdims=True))
        p = p / jnp.sum(p, axis=-1, keepdims=True)
        outs.append(jnp.dot(p.astype(BF16), vh, preferred_element_type=F32))
    o = jnp.concatenate(outs, axis=-1)
    h_out = jnp.dot(o.astype(BF16), wo_ref[...], preferred_element_type=F32)
    o_ref[...] = _layer_norm_rows(ALPHA * x + h_out, g_ref[...], b_ref[...])


def _cross_attention(x, kv, wq, wo, g, b, seq, tm):
    t, d = x.shape
    n_mem = kv.shape[0] // (t // seq)
    tiles_per_seq = seq // tm
    w = pl.BlockSpec((d, d), lambda i: (0, 0))
    vec = pl.BlockSpec((1, d), lambda i: (0, 0))
    blk = pl.BlockSpec((tm, d), lambda i: (i, 0))
    return pl.pallas_call(
        _xattn_kernel,
        out_shape=jax.ShapeDtypeStruct((t, d), F32),
        grid=(t // tm,),
        in_specs=[blk, pl.BlockSpec((n_mem, 2 * d), lambda i: (i // tiles_per_seq, 0)), w, w, vec, vec],
        out_specs=blk,
        compiler_params=_cp("parallel"),
        name="cross_attention",
    )(x, kv, wq, wo, g.reshape(1, d), b.reshape(1, d))


FFN_TN = 256


def _ffn_up_kernel(x_ref, halo_ref, wa_ref, wg_ref, cw_ref, cb_ref, o_ref, *, tiles_per_seq):
    i = pl.program_id(0)
    x = x_ref[...].astype(BF16)
    first = jnp.where(i % tiles_per_seq == 0, 0.0, 1.0)
    xh = (halo_ref[...] * first).astype(BF16)
    a8 = jnp.dot(jnp.concatenate([xh, x], axis=0), wa_ref[...], preferred_element_type=F32)
    a0 = a8[SUBLANES:]
    a1 = pltpu.roll(a8, 1, axis=0)[SUBLANES:]
    a2 = pltpu.roll(a8, 2, axis=0)[SUBLANES:]
    conv = cb_ref[...] + cw_ref[0:1, :] * a2 + cw_ref[1:2, :] * a1 + cw_ref[2:3, :] * a0
    g = jnp.dot(x, wg_ref[...], preferred_element_type=F32)
    o_ref[...] = (jax.nn.silu(conv) * g).astype(o_ref.dtype)


def _ffn_up(x, wa, wg, conv_w, conv_b, seq, tm):
    t, d = x.shape
    hb = tm // SUBLANES
    cw = jnp.zeros((SUBLANES, D_FF), F32).at[0:3].set(conv_w)
    wspec = pl.BlockSpec((d, FFN_TN), lambda i, j: (0, j))
    return pl.pallas_call(
        functools.partial(_ffn_up_kernel, tiles_per_seq=seq // tm),
        out_shape=jax.ShapeDtypeStruct((t, D_FF), BF16),
        grid=(t // tm, D_FF // FFN_TN),
        in_specs=[pl.BlockSpec((tm, d), lambda i, j: (i, 0)),
                  pl.BlockSpec((SUBLANES, d), lambda i, j: (jnp.maximum(i * hb - 1, 0), 0)),
                  wspec, wspec,
                  pl.BlockSpec((SUBLANES, FFN_TN), lambda i, j: (0, j)),
                  pl.BlockSpec((1, FFN_TN), lambda i, j: (0, j))],
        out_specs=pl.BlockSpec((tm, FFN_TN), lambda i, j: (i, j)),
        compiler_params=_cp("parallel", "arbitrary"),
        name="ffn_up",
    )(x, x, wa, wg, cw, conv_b.reshape(1, D_FF))


def kernel(x, mem, positions, w_in, da_lam_q1, da_lam_k1, da_lam_q2, da_lam_k2, da_subln_g, rw_mu, rw_w0, rw_w2, rw_a0, rw_a2, rw_g2, rw_k_k, rw_k_a, rw_r_k, rw_lnx_g, rw_lnx_b, s5_a_re, s5_a_im, s5_b_re, s5_b_im, s5_c_re, s5_c_im, s5_d, s5_log_step, s5_glu_w, s5_glu_b, s5_out_g, w_out, ln1_g, ln1_b, ca_wq, ca_wkv, ca_wo, ln2_g, ln2_b, ffn_w_up, ffn_conv_w, ffn_conv_b, ffn_w_down, ln3_g, ln3_b):
    bsz, seq, d = x.shape
    t = bsz * seq
    tm = min(512, seq)
    n_mem = mem.shape[1]
    xf = x.reshape(t, d)
    memf = mem.reshape(bsz * n_mem, d)
    cos_t, sin_t = _rope_tables(positions, tm)

    for l in range(DEPTH):
        lambda_init = 0.8 - 0.6 * math.exp(-0.3 * l)
        wi = w_in[l]
        w_perm = jnp.concatenate([wi[:, :3 * DA_WIDTH], wi[:, 3 * DA_WIDTH + RW_IN:],
                                  wi[:, 3 * DA_WIDTH:3 * DA_WIDTH + RW_IN]], axis=1).astype(BF16)
        proj = _matmul(xf, w_perm, F32, tm, IN_WIDTH)

        qr, kr, vr = _rope(proj, cos_t, sin_t, tm)
        lam = (jnp.exp(jnp.sum(da_lam_q1[l] * da_lam_k1[l])) - jnp.exp(jnp.sum(da_lam_q2[l] * da_lam_k2[l]))
               + lambda_init)
        h_da = _attention(qr, kr, vr, jnp.full((1, LANES), lam, F32), da_subln_g[l], lambda_init, bsz, seq)

        r, w, k, an, bv, v, gate, bonus = _rw_prep(proj, rw_mu[l], rw_w0[l], rw_w2[l], rw_a0[l], rw_a2[l],
                                                   rw_g2[l], rw_k_k[l], rw_k_a[l], rw_r_k[l].reshape(-1), seq, tm)
        y4 = _rw_scan(*(_to_scan_k(a, bsz, seq) for a in (r, w, k, an, bv)), _to_scan_v(v, bsz, seq), seq)
        h_rw = _rw_post(_from_scan_v(y4, bsz, seq), bonus, gate, rw_lnx_g[l], rw_lnx_b[l], tm)

        tables = _s5_tables(s5_a_re[l], s5_a_im[l], s5_b_re[l], s5_b_im[l], s5_c_re[l], s5_c_im[l],
                            s5_log_step[l])
        h_s5 = _s5(proj, tables, s5_d[l], s5_glu_w[l], s5_glu_b[l], s5_out_g[l], bsz, seq)

        wo = w_out[l].astype(BF16)
        xf = _matmul_ln([h_da, h_rw, h_s5],
                        [wo[:DA_WIDTH], wo[DA_WIDTH:DA_WIDTH + RW_WIDTH], wo[DA_WIDTH + RW_WIDTH:]],
                        xf, ln1_g[l], ln1_b[l], tm)
        kv = _matmul(memf, ca_wkv[l].astype(BF16), BF16, n_mem, 1024)
        xf = _cross_attention(xf, kv, ca_wq[l].astype(BF16), ca_wo[l].astype(BF16), ln2_g[l], ln2_b[l], seq, tm)
        w_up = ffn_w_up[l].astype(BF16)
        hff = _ffn_up(xf, w_up[:, :D_FF], w_up[:, D_FF:], ffn_conv_w[l], ffn_conv_b[l], seq, tm)
        xf = _matmul_ln([hff], [ffn_w_down[l].astype(BF16)], xf, ln3_g[l], ln3_b[l], tm)
    return xf.reshape(bsz, seq, d)
```

```python
import functools
import math

import numpy as np
import jax
import jax.numpy as jnp
from jax import lax
from jax.experimental import pallas as pl
from jax.experimental.pallas import tpu as pltpu

F32 = jnp.float32
BF16 = jnp.bfloat16
HI = lax.Precision.HIGHEST

D_MODEL = 1024
DEPTH = 4
CHUNK = 64
LN_EPS = 1e-5
LOG2E = math.log2(math.e)
RMS_EPS = 1e-6

DA_HEAD_DIM = 64
DA_V_DIM = 128
DA_WIDTH = 512
DA_HEADS = 4
ROPE_THETA = 10000.0

RW_HEAD_DIM = 64
RW_WIDTH = 256
RW_HEADS = 4
RW_DECAY_RANK = 32
RW_AAA_RANK = 32
RW_GATE_RANK = 64
RW_IN = 3 * RW_WIDTH + RW_DECAY_RANK + RW_AAA_RANK + RW_GATE_RANK
RW_DECAY_SCALE = math.exp(-0.5)
RW_LNX_EPS = 64e-5

S5_WIDTH = 256
S5_GROUP_CH = 16
S5_GROUPS = 16
S5_STATE = 64
S5_CPLX = S5_GROUPS * S5_STATE

IN_WIDTH = 3 * DA_WIDTH + RW_IN + S5_WIDTH
MEM_HEADS = 4
MEM_HEAD_DIM = 256
D_FF = 2816
ALPHA = (2.0 * DEPTH) ** 0.25

COL_Q, COL_K, COL_V, COL_S5, COL_RW = 0, 512, 1024, 1536, 1792

VMEM_LIMIT = 48 * 1024 * 1024
LANES = 128
SUBLANES = 8


def _cp(*sem):
    return pltpu.CompilerParams(dimension_semantics=sem, vmem_limit_bytes=VMEM_LIMIT)


def _layer_norm_rows(v, g, b):
    mu = jnp.mean(v, axis=-1, keepdims=True)
    c = v - mu
    var = jnp.mean(c * c, axis=-1, keepdims=True)
    return c * lax.rsqrt(var + LN_EPS) * g + b


def _mm_kernel(a_ref, w_ref, o_ref):
    o_ref[...] = jnp.dot(a_ref[...].astype(BF16), w_ref[...],
                         preferred_element_type=F32).astype(o_ref.dtype)


def _matmul(a, w, out_dtype, tm, tn):
    m, k = a.shape
    n = w.shape[1]
    return pl.pallas_call(
        _mm_kernel,
        out_shape=jax.ShapeDtypeStruct((m, n), out_dtype),
        grid=(m // tm, n // tn),
        in_specs=[pl.BlockSpec((tm, k), lambda i, j: (i, 0)),
                  pl.BlockSpec((k, tn), lambda i, j: (0, j))],
        out_specs=pl.BlockSpec((tm, tn), lambda i, j: (i, j)),
        compiler_params=_cp("parallel", "arbitrary"),
        name="matmul",
    )(a, w)


def _mm_ln_kernel(n_in, *refs):
    a_refs = refs[:n_in]
    w_refs = refs[n_in:2 * n_in]
    x_ref, g_ref, b_ref, o_ref = refs[2 * n_in:]
    acc = ALPHA * x_ref[...]
    for a_ref, w_ref in zip(a_refs, w_refs):
        acc = acc + jnp.dot(a_ref[...].astype(BF16), w_ref[...], preferred_element_type=F32)
    o_ref[...] = _layer_norm_rows(acc, g_ref[...], b_ref[...])


def _matmul_ln(a_list, w_list, x, g, b, tm):
    m, d = x.shape
    n_in = len(a_list)
    in_specs = [pl.BlockSpec((tm, a.shape[1]), lambda i: (i, 0)) for a in a_list]
    in_specs += [pl.BlockSpec(w.shape, lambda i: (0, 0)) for w in w_list]
    in_specs += [pl.BlockSpec((tm, d), lambda i: (i, 0)),
                 pl.BlockSpec((1, d), lambda i: (0, 0)),
                 pl.BlockSpec((1, d), lambda i: (0, 0))]
    return pl.pallas_call(
        functools.partial(_mm_ln_kernel, n_in),
        out_shape=jax.ShapeDtypeStruct((m, d), F32),
        grid=(m // tm,),
        in_specs=in_specs,
        out_specs=pl.BlockSpec((tm, d), lambda i: (i, 0)),
        compiler_params=_cp("parallel"),
        name="matmul_ln",
    )(*a_list, *w_list, x, g.reshape(1, d), b.reshape(1, d))


def _rope_table_kernel(pos_ref, freq_ref, sign_ref, cos_ref, sin_ref):
    ang = pos_ref[...] * freq_ref[...]
    cos_ref[...] = jnp.cos(ang)
    sin_ref[...] = jnp.sin(ang) * sign_ref[...]


def _rope_tables(positions, tm):
    t = positions.size
    inv_freq = ROPE_THETA ** (-jnp.arange(0, DA_HEAD_DIM, 2, dtype=F32) / DA_HEAD_DIM)
    freq_row = jnp.tile(inv_freq, 4).reshape(1, LANES)
    sign_row = jnp.tile(jnp.concatenate([-jnp.ones((32,), F32), jnp.ones((32,), F32)]), 2).reshape(1, LANES)
    pos_col = positions.astype(F32).reshape(t, 1)
    row = pl.BlockSpec((1, LANES), lambda i: (0, 0))
    return pl.pallas_call(
        _rope_table_kernel,
        out_shape=(jax.ShapeDtypeStruct((t, LANES), F32),) * 2,
        grid=(t // tm,),
        in_specs=[pl.BlockSpec((tm, 1), lambda i: (i, 0)), row, row],
        out_specs=(pl.BlockSpec((tm, LANES), lambda i: (i, 0)),) * 2,
        compiler_params=_cp("parallel"),
        name="rope_tables",
    )(pos_col, freq_row, sign_row)


def _rope_kernel(q_ref, k_ref, v_ref, cos_ref, sin_ref, qo_ref, ko_ref, vt_ref):
    cos = cos_ref[...]
    sin = sin_ref[...]
    lane = lax.broadcasted_iota(jnp.int32, cos.shape, 1)
    low = (lane % DA_HEAD_DIM) < (DA_HEAD_DIM // 2)

    def rope(t):
        swapped = jnp.where(low, pltpu.roll(t, LANES - 32, axis=1), pltpu.roll(t, 32, axis=1))
        return t * cos + swapped * sin

    scale = DA_HEAD_DIM ** -0.5 * LOG2E
    for h in range(DA_HEADS):
        sl = slice(h * LANES, (h + 1) * LANES)
        qo_ref[:, sl] = (rope(q_ref[:, sl]) * scale).astype(BF16)
        ko_ref[:, sl] = rope(k_ref[:, sl]).astype(BF16)
    vt_ref[...] = v_ref[...].T.astype(BF16)


def _rope(proj, cos_t, sin_t, seq, tm):
    t = proj.shape[0]
    nt = seq // tm
    col = lambda c: pl.BlockSpec((tm, DA_WIDTH), lambda i: (i, c))
    tab = pl.BlockSpec((tm, LANES), lambda i: (i, 0))
    out = pl.BlockSpec((tm, DA_WIDTH), lambda i: (i, 0))
    return pl.pallas_call(
        _rope_kernel,
        out_shape=(jax.ShapeDtypeStruct((t, DA_WIDTH), BF16),) * 2
        + (jax.ShapeDtypeStruct((t // seq * DA_WIDTH, seq), BF16),),
        grid=(t // tm,),
        in_specs=[col(COL_Q // DA_WIDTH), col(COL_K // DA_WIDTH), col(COL_V // DA_WIDTH), tab, tab],
        out_specs=(out, out, pl.BlockSpec((DA_WIDTH, tm), lambda i: (i // nt, i % nt))),
        compiler_params=_cp("parallel"),
        name="rope",
    )(proj, proj, proj, cos_t, sin_t)


ATT_TQ = 512
ATT_TK = 512


def _attn_kernel(lam_ref, g_ref, q_ref, k_ref, vt_ref, o_ref, acc_ref, *, lambda_init):
    i = pl.program_id(2)
    nq = 2 * ATT_TQ
    q = q_ref[...]
    lane = lax.broadcasted_iota(jnp.int32, q.shape, 1)
    zero = jnp.zeros_like(q)
    qs = jnp.concatenate([jnp.where(lane < DA_HEAD_DIM, q, zero),
                          jnp.where(lane >= DA_HEAD_DIM, q, zero)], axis=0)

    def scores(j):
        start = pl.multiple_of(j * ATT_TK, ATT_TK)
        kb = k_ref[pl.ds(start, ATT_TK), :]
        return lax.dot_general(kb, qs, (((1,), (1,)), ((), ())), preferred_element_type=F32)

    def update(j, s, m, l):
        m_new = jnp.maximum(m, jnp.max(s, axis=0, keepdims=True))
        alpha = jnp.exp2(m - m_new)
        p = jnp.exp2(s - m_new)
        l = alpha * l + jnp.sum(p, axis=0, keepdims=True)
        start = pl.multiple_of(j * ATT_TK, ATT_TK)
        vt = vt_ref[:, pl.ds(start, ATT_TK)]
        acc_ref[...] = alpha * acc_ref[...] + jnp.dot(vt, p.astype(BF16), preferred_element_type=F32)
        return m_new, l

    def body(j, carry):
        s, m, l = carry
        s_next = scores(j + 1)
        m, l = update(j, s, m, l)
        return s_next, m, l

    acc_ref[...] = jnp.zeros_like(acc_ref)
    init = (scores(0), jnp.full((1, nq), -jnp.inf, F32), jnp.zeros((1, nq), F32))
    s, m, l = lax.fori_loop(0, i, body, init)
    krow = lax.broadcasted_iota(jnp.int32, s.shape, 0)
    qcol = lax.broadcasted_iota(jnp.int32, s.shape, 1) % ATT_TQ
    s = jnp.where(krow // CHUNK <= qcol // CHUNK, s, -jnp.inf)
    m, l = update(i, s, m, l)
    o = acc_ref[...] / l
    o = o[:, :ATT_TQ] - lam_ref[...] * o[:, ATT_TQ:]
    o = o * lax.rsqrt(jnp.mean(o * o, axis=0, keepdims=True) + RMS_EPS) * g_ref[...]
    o_ref[...] = (o * (1.0 - lambda_init)).T


def _attention(qr, kr, vt, lam, subln_g, lambda_init, bsz, seq):
    nq = seq // ATT_TQ
    col = pl.BlockSpec((LANES, 1), lambda b, h, i: (0, 0))
    qo = pl.BlockSpec((ATT_TQ, LANES), lambda b, h, i: (b * nq + i, h))
    return pl.pallas_call(
        functools.partial(_attn_kernel, lambda_init=lambda_init),
        out_shape=jax.ShapeDtypeStruct((bsz * seq, DA_WIDTH), F32),
        grid=(bsz, DA_HEADS, nq),
        in_specs=[pl.BlockSpec((1, 1), lambda b, h, i: (0, 0)), col, qo,
                  pl.BlockSpec((seq, LANES), lambda b, h, i: (b, h)),
                  pl.BlockSpec((DA_V_DIM, seq), lambda b, h, i: (b * DA_HEADS + h, 0))],
        out_specs=qo,
        scratch_shapes=[pltpu.VMEM((DA_V_DIM, 2 * ATT_TQ), F32)],
        compiler_params=_cp("parallel", "parallel", "arbitrary"),
        name="diff_attention",
    )(lam.reshape(1, 1), subln_g.reshape(LANES, 1), qr, kr, vt)


RW_VLO = 4
RW_VHI = RW_HEAD_DIM // RW_VLO
RW_BH = LANES // RW_VLO
RW_TB = 128
RW_PITCH = RW_HEAD_DIM + SUBLANES
RW_TT = 64
RW_NACC = 4
N_KOPS = 5


def _rw_prep_kernel(p_ref, halo_ref, mu_ref, w0_ref, a0_ref, kk_ref, ka_ref, rk_ref,
                    w2_ref, a2_ref, g2_ref, seg_ref,
                    kop_ref, v_ref, g_ref, bonus_ref, tr_ref):
    i = pl.program_id(0)
    n_op = pl.program_id(1)
    bsz = p_ref.shape[0]
    first = jnp.where(i == 0, 0.0, 1.0)
    dot = lambda a, b: jnp.dot(a, b, preferred_element_type=F32, precision=HI)
    seg = seg_ref[...]

    def per_batch(b, _):
        p = p_ref[b]
        row = lax.broadcasted_iota(jnp.int32, p.shape, 0)
        last_prev = jnp.broadcast_to(halo_ref[b, SUBLANES - 1:SUBLANES, :], p.shape) * first
        prev = jnp.where(row == 0, last_prev, pltpu.roll(p, 1, axis=0))
        p = p + (prev - p) * mu_ref[...]
        r = p[:, 0:RW_WIDTH]
        k = p[:, RW_WIDTH:2 * RW_WIDTH]
        v = p[:, 2 * RW_WIDTH:3 * RW_WIDTH]
        low = p[:, 3 * RW_WIDTH:RW_IN]
        decay = jnp.exp(-RW_DECAY_SCALE * jax.nn.sigmoid(w0_ref[...] + dot(jnp.tanh(low), w2_ref[...])))
        a = jax.nn.sigmoid(a0_ref[...] + dot(low, a2_ref[...]))
        g_ref[b] = dot(jax.nn.sigmoid(low), g2_ref[...])
        kk = k * kk_ref[...]
        kk = kk * lax.rsqrt(jnp.maximum(dot(kk * kk, seg), 1e-24))
        k = k * (1.0 + (a - 1.0) * ka_ref[...])
        bonus_ref[b] = dot(r * k * rk_ref[...], seg) * v
        for n, val in enumerate((r, decay, k, -kk, kk * a, v)):
            vt = val.T
            for h in range(RW_HEADS):
                base = pl.multiple_of((b * RW_HEADS + h) * RW_PITCH, SUBLANES)
                tr_ref[n, pl.ds(base, RW_HEAD_DIM), :] = vt[h * RW_HEAD_DIM:(h + 1) * RW_HEAD_DIM, :]
        return 0

    def per_vhi(vh, _):
        rows = [tr_ref[N_KOPS, pl.ds(vh * RW_VLO + vl, RW_BH, stride=RW_PITCH), :] for vl in range(RW_VLO)]
        v_ref[pl.ds(vh, RW_TB, stride=RW_VHI), :] = jnp.concatenate(rows, axis=0).T
        return 0

    @pl.when(n_op == 0)
    def _():
        lax.fori_loop(0, bsz, per_batch, 0)
        lax.fori_loop(0, RW_VHI, per_vhi, 0)

    def per_k(kk_i, _):
        rows = tr_ref[n_op, pl.ds(kk_i, RW_BH, stride=RW_PITCH), :]
        kop_ref[0, kk_i] = jnp.concatenate([rows] * RW_VLO, axis=0).T
        return 0
    lax.fori_loop(0, RW_HEAD_DIM, per_k, 0)


def _head_segments():
    idx = np.arange(RW_WIDTH) // RW_HEAD_DIM
    return jnp.asarray((idx[:, None] == idx[None, :]).astype(np.float32))


def _rw_prep(proj3, mu, w0, w2, a0, a2, g2, k_k, k_a, r_k):
    bsz, seq, _ = proj3.shape
    hb = RW_TB // SUBLANES
    w2p = jnp.zeros((LANES, RW_WIDTH), F32).at[0:32].set(w2)
    a2p = jnp.zeros((LANES, RW_WIDTH), F32).at[32:64].set(a2)
    g2p = jnp.zeros((LANES, RW_WIDTH), F32).at[64:128].set(g2)
    vec = lambda n: pl.BlockSpec((1, n), lambda i, j: (0, 0))
    mat = lambda r, c: pl.BlockSpec((r, c), lambda i, j: (0, 0))
    tok = pl.BlockSpec((bsz, RW_TB, RW_WIDTH), lambda i, j: (0, i, 0))
    tshape = jax.ShapeDtypeStruct((bsz, seq, RW_WIDTH), F32)
    return pl.pallas_call(
        _rw_prep_kernel,
        out_shape=(jax.ShapeDtypeStruct((N_KOPS, RW_HEAD_DIM, seq, LANES), F32),
                   jax.ShapeDtypeStruct((seq * RW_VHI, LANES), F32), tshape, tshape),
        grid=(seq // RW_TB, N_KOPS),
        in_specs=[pl.BlockSpec((bsz, RW_TB, RW_IN), lambda i, j: (0, i, COL_RW // RW_IN)),
                  pl.BlockSpec((bsz, SUBLANES, RW_IN),
                               lambda i, j: (0, jnp.maximum(i * hb - 1, 0), COL_RW // RW_IN)),
                  vec(RW_IN), vec(RW_WIDTH), vec(RW_WIDTH), vec(RW_WIDTH), vec(RW_WIDTH), vec(RW_WIDTH),
                  mat(LANES, RW_WIDTH), mat(LANES, RW_WIDTH), mat(LANES, RW_WIDTH), mat(RW_WIDTH, RW_WIDTH)],
        out_specs=(pl.BlockSpec((1, RW_HEAD_DIM, RW_TB, LANES), lambda i, j: (j, 0, i, 0)),
                   pl.BlockSpec((RW_TB * RW_VHI, LANES), lambda i, j: (i, 0)), tok, tok),
        scratch_shapes=[pltpu.VMEM((N_KOPS + 1, RW_BH * RW_PITCH, RW_TB), F32)],
        compiler_params=_cp("parallel", "arbitrary"),
        name="rwkv_prep",
    )(proj3, proj3, mu.reshape(1, RW_IN), w0.reshape(1, -1), a0.reshape(1, -1), k_k.reshape(1, -1),
      k_a.reshape(1, -1), r_k.reshape(1, -1), w2p, a2p, g2p, _head_segments())


def _rw_scan_kernel(r_ref, w_ref, k_ref, an_ref, b_ref, v_ref, y_ref, s_ref):
    @pl.when(pl.program_id(0) == 0)
    def _():
        s_ref[...] = jnp.zeros_like(s_ref)

    nvb = RW_VHI // SUBLANES

    def bc(ref, t, kk):
        return jnp.broadcast_to(ref[kk, pl.ds(t, 1), :], (SUBLANES, LANES))

    def body(t, _):
        zacc = [[jnp.zeros((SUBLANES, LANES), F32) for _ in range(RW_NACC)] for _ in range(nvb)]
        for kk in range(RW_HEAD_DIM):
            an = bc(an_ref, t, kk)
            for vb in range(nvb):
                zacc[vb][kk % RW_NACC] += s_ref[kk, vb * SUBLANES:(vb + 1) * SUBLANES, :] * an
        z = [sum(zacc[vb][1:], zacc[vb][0]) for vb in range(nvb)]
        vt = [v_ref[t, vb * SUBLANES:(vb + 1) * SUBLANES, :] for vb in range(nvb)]
        yacc = [[jnp.zeros((SUBLANES, LANES), F32) for _ in range(RW_NACC)] for _ in range(nvb)]
        for kk in range(RW_HEAD_DIM):
            wr = bc(w_ref, t, kk)
            br = bc(b_ref, t, kk)
            kr = bc(k_ref, t, kk)
            rr = bc(r_ref, t, kk)
            for vb in range(nvb):
                sl = slice(vb * SUBLANES, (vb + 1) * SUBLANES)
                s_new = s_ref[kk, sl, :] * wr + z[vb] * br + vt[vb] * kr
                s_ref[kk, sl, :] = s_new
                yacc[vb][kk % RW_NACC] += s_new * rr
        for vb in range(nvb):
            y_ref[t, vb * SUBLANES:(vb + 1) * SUBLANES, :] = sum(yacc[vb][1:], yacc[vb][0])
        return 0

    lax.fori_loop(0, RW_TT, body, 0)


def _rw_scan(kops, v4, seq):
    kspec = lambda n: pl.BlockSpec((None, RW_HEAD_DIM, RW_TT, LANES), lambda i: (n, 0, i, 0))
    vspec = pl.BlockSpec((RW_TT, RW_VHI, LANES), lambda i: (i, 0, 0))
    return pl.pallas_call(
        _rw_scan_kernel,
        out_shape=jax.ShapeDtypeStruct((seq, RW_VHI, LANES), F32),
        grid=(seq // RW_TT,),
        in_specs=[kspec(n) for n in range(N_KOPS)] + [vspec],
        out_specs=vspec,
        scratch_shapes=[pltpu.VMEM((RW_HEAD_DIM, RW_VHI, LANES), F32)],
        compiler_params=_cp("arbitrary"),
        name="rwkv_scan",
    )(*([kops] * N_KOPS), v4.reshape(seq, RW_VHI, LANES))


def _rw_post_kernel(y_ref, bonus_ref, gate_ref, lg_ref, lb_ref, seg_ref, o_ref, tr_ref):
    bsz = bonus_ref.shape[0]

    def per_vhi(vh, _):
        yt = y_ref[pl.ds(vh, RW_TB, stride=RW_VHI), :].T
        for vl in range(RW_VLO):
            tr_ref[pl.ds(vh * RW_VLO + vl, RW_BH, stride=RW_PITCH), :] = yt[vl * RW_BH:(vl + 1) * RW_BH, :]
        return 0
    lax.fori_loop(0, RW_VHI, per_vhi, 0)

    seg = seg_ref[...] * (1.0 / RW_HEAD_DIM)
    dot = lambda a, b: jnp.dot(a, b, preferred_element_type=F32, precision=HI)

    def per_batch(b, _):
        slabs = []
        for h in range(RW_HEADS):
            base = pl.multiple_of((b * RW_HEADS + h) * RW_PITCH, SUBLANES)
            slabs.append(tr_ref[pl.ds(base, RW_HEAD_DIM), :])
        y = jnp.concatenate(slabs, axis=0).T
        c = y - dot(y, seg)
        var = dot(c * c, seg)
        y = c * lax.rsqrt(var + RW_LNX_EPS) * lg_ref[...] + lb_ref[...]
        o_ref[b] = (y + bonus_ref[b]) * gate_ref[b]
        return 0
    lax.fori_loop(0, bsz, per_batch, 0)


def _rw_post(y4, bonus, gate, lnx_g, lnx_b):
    bsz, seq, _ = bonus.shape
    tok = pl.BlockSpec((bsz, RW_TB, RW_WIDTH), lambda i: (0, i, 0))
    vec = pl.BlockSpec((1, RW_WIDTH), lambda i: (0, 0))
    return pl.pallas_call(
        _rw_post_kernel,
        out_shape=jax.ShapeDtypeStruct((bsz, seq, RW_WIDTH), F32),
        grid=(seq // RW_TB,),
        in_specs=[pl.BlockSpec((RW_TB * RW_VHI, LANES), lambda i: (i, 0)), tok, tok, vec, vec,
                  pl.BlockSpec((RW_WIDTH, RW_WIDTH), lambda i: (0, 0))],
        out_specs=tok,
        scratch_shapes=[pltpu.VMEM((RW_BH * RW_PITCH, RW_TB), F32)],
        compiler_params=_cp("parallel"),
        name="rwkv_post",
    )(y4.reshape(seq * RW_VHI, LANES), bonus, gate, lnx_g.reshape(1, -1), lnx_b.reshape(1, -1), _head_segments())


S5_TT = 256


def _s5_kernel(u_ref, wb_ref, wc_ref, pw_ref, ap_ref, d_ref, gw_ref, gb_ref, og_ref,
               o_ref, x_ref, carry_ref):
    @pl.when(pl.program_id(1) == 0)
    def _():
        carry_ref[...] = jnp.zeros_like(carry_ref)

    n = S5_CPLX
    u = u_ref[...]
    bu = jnp.dot(u.astype(BF16), wb_ref[...], preferred_element_type=F32)
    xr, xi = bu[:, :n], bu[:, n:]
    sub = lax.broadcasted_iota(jnp.int32, xr.shape, 0) % SUBLANES
    for step, s in enumerate((1, 2, 4)):
        ar = pw_ref[step:step + 1, :n]
        ai = pw_ref[step:step + 1, n:]
        keep = sub >= s
        sr = jnp.where(keep, pltpu.roll(xr, s, axis=0), 0.0)
        si = jnp.where(keep, pltpu.roll(xi, s, axis=0), 0.0)
        xr, xi = xr + (ar * sr - ai * si), xi + (ar * si + ai * sr)
    x_ref[:, :n] = xr
    x_ref[:, n:] = xi
    pr, pi = ap_ref[:, :n], ap_ref[:, n:]

    def group(gi, carry):
        cr, ci = carry
        rows = pl.ds(pl.multiple_of(gi * SUBLANES, SUBLANES), SUBLANES)
        gr = x_ref[rows, :n] + (pr * cr - pi * ci)
        gim = x_ref[rows, n:] + (pr * ci + pi * cr)
        x_ref[rows, :n] = gr
        x_ref[rows, n:] = gim
        return (jnp.broadcast_to(gr[SUBLANES - 1:SUBLANES, :], (SUBLANES, n)),
                jnp.broadcast_to(gim[SUBLANES - 1:SUBLANES, :], (SUBLANES, n)))

    cr, ci = lax.fori_loop(0, S5_TT // SUBLANES, group, (carry_ref[:, :n], carry_ref[:, n:]))
    carry_ref[:, :n] = cr
    carry_ref[:, n:] = ci
    y = jnp.dot(x_ref[...].astype(BF16), wc_ref[...], preferred_element_type=F32) + d_ref[...] * u
    z = jax.nn.gelu(y)
    z = z * jax.nn.sigmoid(jnp.dot(z.astype(BF16), gw_ref[...], preferred_element_type=F32) + gb_ref[...])
    o_ref[...] = z * lax.rsqrt(jnp.mean(z * z, axis=-1, keepdims=True) + RMS_EPS) * og_ref[...]


def _s5_tables(a_re, a_im, b_re, b_im, c_re, c_im, log_step):
    delta = jnp.exp(log_step.astype(F32))[:, None]
    mag = jnp.exp(delta * a_re)

    def a_pow(n):
        return (mag ** n) * jnp.cos(n * delta * a_im), (mag ** n) * jnp.sin(n * delta * a_im)

    abr, abi = a_pow(1)
    den = a_re * a_re + a_im * a_im
    qr = ((abr - 1.0) * a_re + abi * a_im) / den
    qi = (abi * a_re - (abr - 1.0) * a_im) / den
    bbr = qr[..., None] * b_re - qi[..., None] * b_im
    bbi = qr[..., None] * b_im + qi[..., None] * b_re
    eye = jnp.eye(S5_GROUPS, dtype=F32)
    blk_b = lambda m: jnp.einsum('gph,gk->ghkp', m, eye).reshape(S5_WIDTH, S5_CPLX)
    blk_c = lambda m: jnp.einsum('ghp,gk->gpkh', m, eye).reshape(S5_CPLX, S5_WIDTH)
    wb = jnp.concatenate([blk_b(bbr), blk_b(bbi)], axis=1).astype(BF16)
    wc = jnp.concatenate([blk_c(c_re), -blk_c(c_im)], axis=0).astype(BF16)
    flat = lambda pr: jnp.concatenate([pr[0].reshape(1, -1), pr[1].reshape(1, -1)], axis=1)
    pw = jnp.concatenate([flat(a_pow(s)) for s in (1, 2, 4)] + [jnp.zeros((5, 2 * S5_CPLX), F32)], axis=0)
    ap = jnp.concatenate([flat(a_pow(r + 1)) for r in range(SUBLANES)], axis=0)
    return wb, wc, pw, ap


def _s5(proj, tables, d_skip, glu_w, glu_b, out_g, bsz, seq):
    wb, wc, pw, ap = tables
    nt = seq // S5_TT
    full = lambda a: pl.BlockSpec(a.shape, lambda b, i: (0, 0))
    vec = pl.BlockSpec((1, S5_WIDTH), lambda b, i: (0, 0))
    gw = glu_w.astype(BF16)
    return pl.pallas_call(
        _s5_kernel,
        out_shape=jax.ShapeDtypeStruct((bsz * seq, S5_WIDTH), F32),
        grid=(bsz, nt),
        in_specs=[pl.BlockSpec((S5_TT, S5_WIDTH), lambda b, i: (b * nt + i, COL_S5 // S5_WIDTH)),
                  full(wb), full(wc), full(pw), full(ap), vec, full(gw), vec, vec],
        out_specs=pl.BlockSpec((S5_TT, S5_WIDTH), lambda b, i: (b * nt + i, 0)),
        scratch_shapes=[pltpu.VMEM((S5_TT, 2 * S5_CPLX), F32),
                        pltpu.VMEM((SUBLANES, 2 * S5_CPLX), F32)],
        compiler_params=_cp("parallel", "arbitrary"),
        name="s5",
    )(proj, wb, wc, pw, ap, d_skip.reshape(1, -1), gw, glu_b.reshape(1, -1), out_g.reshape(1, -1))


def _xattn_kernel(x_ref, kv_ref, wq_ref, wo_ref, g_ref, b_ref, o_ref):
    x = x_ref[...]
    q = jnp.dot(x.astype(BF16), wq_ref[...], preferred_element_type=F32)
    outs = []
    for h in range(MEM_HEADS):
        sl = slice(h * MEM_HEAD_DIM, (h + 1) * MEM_HEAD_DIM)
        kh = kv_ref[:, sl]
        vh = kv_ref[:, D_MODEL + h * MEM_HEAD_DIM:D_MODEL + (h + 1) * MEM_HEAD_DIM]
        s = lax.dot_general(q[:, sl].astype(BF16), kh, (((1,), (1,)), ((), ())),
                            preferred_element_type=F32) * (MEM_HEAD_DIM ** -0.5)
        p = jnp.exp(s - jnp.max(s, axis=-1, keepdims=True))
        p = p / jnp.sum(p, axis=-1, keepdims=True)
        outs.append(jnp.dot(p.astype(BF16), vh, preferred_element_type=F32))
    o = jnp.concatenate(outs, axis=-1)
    h_out = jnp.dot(o.astype(BF16), wo_ref[...], preferred_element_type=F32)
    o_ref[...] = _layer_norm_rows(ALPHA * x + h_out, g_ref[...], b_ref[...])


def _cross_attention(x, kv, wq, wo, g, b, seq, tm):
    t, d = x.shape
    n_mem = kv.shape[0] // (t // seq)
    tiles_per_seq = seq // tm
    w = pl.BlockSpec((d, d), lambda i: (0, 0))
    vec = pl.BlockSpec((1, d), lambda i: (0, 0))
    blk = pl.BlockSpec((tm, d), lambda i: (i, 0))
    return pl.pallas_call(
        _xattn_kernel,
        out_shape=jax.ShapeDtypeStruct((t, d), F32),
        grid=(t // tm,),
        in_specs=[blk, pl.BlockSpec((n_mem, 2 * d), lambda i: (i // tiles_per_seq, 0)), w, w, vec, vec],
        out_specs=blk,
        compiler_params=_cp("parallel"),
        name="cross_attention",
    )(x, kv, wq, wo, g.reshape(1, d), b.reshape(1, d))


FFN_TN = 1408


def _ffn_up_kernel(x_ref, halo_ref, wa_ref, wg_ref, cw_ref, cb_ref, o_ref, *, tiles_per_seq):
    i = pl.program_id(0)
    x = x_ref[...].astype(BF16)
    first = jnp.where(i % tiles_per_seq == 0, 0.0, 1.0)
    xh = (halo_ref[...] * first).astype(BF16)
    a8 = jnp.dot(jnp.concatenate([xh, x], axis=0), wa_ref[...], preferred_element_type=F32)
    a0 = a8[SUBLANES:]
    a1 = pltpu.roll(a8, 1, axis=0)[SUBLANES:]
    a2 = pltpu.roll(a8, 2, axis=0)[SUBLANES:]
    conv = cb_ref[...] + cw_ref[0:1, :] * a2 + cw_ref[1:2, :] * a1 + cw_ref[2:3, :] * a0
    g = jnp.dot(x, wg_ref[...], preferred_element_type=F32)
    o_ref[...] = (jax.nn.silu(conv) * g).astype(o_ref.dtype)


def _ffn_up(x, wa, wg, conv_w, conv_b, seq, tm):
    t, d = x.shape
    hb = tm // SUBLANES
    cw = jnp.zeros((SUBLANES, D_FF), F32).at[0:3].set(conv_w)
    wspec = pl.BlockSpec((d, FFN_TN), lambda i, j: (0, j))
    return pl.pallas_call(
        functools.partial(_ffn_up_kernel, tiles_per_seq=seq // tm),
        out_shape=jax.ShapeDtypeStruct((t, D_FF), BF16),
        grid=(t // tm, D_FF // FFN_TN),
        in_specs=[pl.BlockSpec((tm, d), lambda i, j: (i, 0)),
                  pl.BlockSpec((SUBLANES, d), lambda i, j: (jnp.maximum(i * hb - 1, 0), 0)),
                  wspec, wspec,
                  pl.BlockSpec((SUBLANES, FFN_TN), lambda i, j: (0, j)),
                  pl.BlockSpec((1, FFN_TN), lambda i, j: (0, j))],
        out_specs=pl.BlockSpec((tm, FFN_TN), lambda i, j: (i, j)),
        compiler_params=_cp("parallel", "arbitrary"),
        name="ffn_up",
    )(x, x, wa, wg, cw, conv_b.reshape(1, D_FF))


def kernel(x, mem, positions, w_in, da_lam_q1, da_lam_k1, da_lam_q2, da_lam_k2, da_subln_g, rw_mu, rw_w0, rw_w2, rw_a0, rw_a2, rw_g2, rw_k_k, rw_k_a, rw_r_k, rw_lnx_g, rw_lnx_b, s5_a_re, s5_a_im, s5_b_re, s5_b_im, s5_c_re, s5_c_im, s5_d, s5_log_step, s5_glu_w, s5_glu_b, s5_out_g, w_out, ln1_g, ln1_b, ca_wq, ca_wkv, ca_wo, ln2_g, ln2_b, ffn_w_up, ffn_conv_w, ffn_conv_b, ffn_w_down, ln3_g, ln3_b):
    bsz, seq, d = x.shape
    t = bsz * seq
    tm = min(512, seq)
    n_mem = mem.shape[1]
    xf = x.reshape(t, d)
    memf = mem.reshape(bsz * n_mem, d)
    cos_t, sin_t = _rope_tables(positions, tm)

    for l in range(DEPTH):
        lambda_init = 0.8 - 0.6 * math.exp(-0.3 * l)
        wi = w_in[l]
        w_perm = jnp.concatenate([wi[:, :3 * DA_WIDTH], wi[:, 3 * DA_WIDTH + RW_IN:],
                                  wi[:, 3 * DA_WIDTH:3 * DA_WIDTH + RW_IN]], axis=1).astype(BF16)
        proj = _matmul(xf, w_perm, F32, tm, IN_WIDTH)

        qr, kr, vt = _rope(proj, cos_t, sin_t, seq, tm)
        lam = (jnp.exp(jnp.sum(da_lam_q1[l] * da_lam_k1[l])) - jnp.exp(jnp.sum(da_lam_q2[l] * da_lam_k2[l]))
               + lambda_init)
        h_da = _attention(qr, kr, vt, lam, da_subln_g[l], lambda_init, bsz, seq)

        kops, v4, gate, bonus = _rw_prep(proj.reshape(bsz, seq, IN_WIDTH), rw_mu[l], rw_w0[l], rw_w2[l],
                                         rw_a0[l], rw_a2[l], rw_g2[l], rw_k_k[l], rw_k_a[l],
                                         rw_r_k[l].reshape(-1))
        y4 = _rw_scan(kops, v4, seq)
        h_rw = _rw_post(y4, bonus, gate, rw_lnx_g[l], rw_lnx_b[l]).reshape(t, RW_WIDTH)

        tables = _s5_tables(s5_a_re[l], s5_a_im[l], s5_b_re[l], s5_b_im[l], s5_c_re[l], s5_c_im[l],
                            s5_log_step[l])
        h_s5 = _s5(proj, tables, s5_d[l], s5_glu_w[l], s5_glu_b[l], s5_out_g[l], bsz, seq)

        wo = w_out[l].astype(BF16)
        xf = _matmul_ln([h_da, h_rw, h_s5],
                        [wo[:DA_WIDTH], wo[DA_WIDTH:DA_WIDTH + RW_WIDTH], wo[DA_WIDTH + RW_WIDTH:]],
                        xf, ln1_g[l], ln1_b[l], tm)
        kv = _matmul(memf, ca_wkv[l].astype(BF16), BF16, n_mem, 1024)
        xf = _cross_attention(xf, kv, ca_wq[l].astype(BF16), ca_wo[l].astype(BF16), ln2_g[l], ln2_b[l], seq, tm)
        w_up = ffn_w_up[l].astype(BF16)
        hff = _ffn_up(xf, w_up[:, :D_FF], w_up[:, D_FF:], ffn_conv_w[l], ffn_conv_b[l], seq, tm)
        xf = _matmul_ln([hff], [ffn_w_down[l].astype(BF16)], xf, ln3_g[l], ln3_b[l], tm)
    return xf.reshape(bsz, seq, d)
```

```python
import functools
import math

import numpy as np
import jax
import jax.numpy as jnp
from jax import lax
from jax.experimental import pallas as pl
from jax.experimental.pallas import tpu as pltpu

F32 = jnp.float32
BF16 = jnp.bfloat16
HI = lax.Precision.HIGHEST

D_MODEL = 1024
DEPTH = 4
CHUNK = 64
LN_EPS = 1e-5
LOG2E = math.log2(math.e)
RMS_EPS = 1e-6

DA_HEAD_DIM = 64
DA_V_DIM = 128
DA_WIDTH = 512
DA_HEADS = 4
ROPE_THETA = 10000.0

RW_HEAD_DIM = 64
RW_WIDTH = 256
RW_HEADS = 4
RW_DECAY_RANK = 32
RW_AAA_RANK = 32
RW_GATE_RANK = 64
RW_IN = 3 * RW_WIDTH + RW_DECAY_RANK + RW_AAA_RANK + RW_GATE_RANK
RW_DECAY_SCALE = math.exp(-0.5)
RW_LNX_EPS = 64e-5

S5_WIDTH = 256
S5_GROUP_CH = 16
S5_GROUPS = 16
S5_STATE = 64
S5_CPLX = S5_GROUPS * S5_STATE

IN_WIDTH = 3 * DA_WIDTH + RW_IN + S5_WIDTH
MEM_HEADS = 4
MEM_HEAD_DIM = 256
D_FF = 2816
ALPHA = (2.0 * DEPTH) ** 0.25

COL_Q, COL_K, COL_V, COL_S5, COL_RW = 0, 512, 1024, 1536, 1792

VMEM_LIMIT = 48 * 1024 * 1024
LANES = 128
SUBLANES = 8


def _cp(*sem):
    return pltpu.CompilerParams(dimension_semantics=sem, vmem_limit_bytes=VMEM_LIMIT)


def _layer_norm_rows(v, g, b):
    mu = jnp.mean(v, axis=-1, keepdims=True)
    c = v - mu
    var = jnp.mean(c * c, axis=-1, keepdims=True)
    return c * lax.rsqrt(var + LN_EPS) * g + b


def _split3(x):
    p1 = x.astype(BF16)
    r1 = x - p1.astype(F32)
    p2 = r1.astype(BF16)
    p3 = (r1 - p2.astype(F32)).astype(BF16)
    return p1, p2, p3


def _segment_sum(x, seg):
    return sum(jnp.dot(p, seg, preferred_element_type=F32) for p in _split3(x))


def _dot_split(a, w_ref):
    a1, a2, _ = _split3(a)
    d = functools.partial(jnp.dot, preferred_element_type=F32)
    return d(a1, w_ref[0]) + (d(a1, w_ref[1]) + d(a2, w_ref[0]))


def _hi_lo(w):
    hi = w.astype(BF16)
    return jnp.stack([hi, (w - hi.astype(F32)).astype(BF16)])


def _mm_kernel(a_ref, w_ref, o_ref):
    o_ref[...] = jnp.dot(a_ref[...].astype(BF16), w_ref[...],
                         preferred_element_type=F32).astype(o_ref.dtype)


def _matmul(a, w, out_dtype, tm, tn):
    m, k = a.shape
    n = w.shape[1]
    return pl.pallas_call(
        _mm_kernel,
        out_shape=jax.ShapeDtypeStruct((m, n), out_dtype),
        grid=(m // tm, n // tn),
        in_specs=[pl.BlockSpec((tm, k), lambda i, j: (i, 0)),
                  pl.BlockSpec((k, tn), lambda i, j: (0, j))],
        out_specs=pl.BlockSpec((tm, tn), lambda i, j: (i, j)),
        compiler_params=_cp("parallel", "arbitrary"),
        name="matmul",
    )(a, w)


def _mm_ln_kernel(n_in, *refs):
    a_refs = refs[:n_in]
    w_refs = refs[n_in:2 * n_in]
    x_ref, g_ref, b_ref, o_ref = refs[2 * n_in:]
    acc = ALPHA * x_ref[...]
    for a_ref, w_ref in zip(a_refs, w_refs):
        acc = acc + jnp.dot(a_ref[...].astype(BF16), w_ref[...], preferred_element_type=F32)
    o_ref[...] = _layer_norm_rows(acc, g_ref[...], b_ref[...])


def _matmul_ln(a_list, w_list, x, g, b, tm):
    m, d = x.shape
    n_in = len(a_list)
    in_specs = [pl.BlockSpec((tm, a.shape[1]), lambda i: (i, 0)) for a in a_list]
    in_specs += [pl.BlockSpec(w.shape, lambda i: (0, 0)) for w in w_list]
    in_specs += [pl.BlockSpec((tm, d), lambda i: (i, 0)),
                 pl.BlockSpec((1, d), lambda i: (0, 0)),
                 pl.BlockSpec((1, d), lambda i: (0, 0))]
    return pl.pallas_call(
        functools.partial(_mm_ln_kernel, n_in),
        out_shape=jax.ShapeDtypeStruct((m, d), F32),
        grid=(m // tm,),
        in_specs=in_specs,
        out_specs=pl.BlockSpec((tm, d), lambda i: (i, 0)),
        compiler_params=_cp("parallel"),
        name="matmul_ln",
    )(*a_list, *w_list, x, g.reshape(1, d), b.reshape(1, d))


def _rope_table_kernel(pos_ref, freq_ref, sign_ref, cos_ref, sin_ref):
    ang = pos_ref[...] * freq_ref[...]
    cos_ref[...] = jnp.cos(ang)
    sin_ref[...] = jnp.sin(ang) * sign_ref[...]


def _rope_tables(positions, tm):
    t = positions.size
    inv_freq = ROPE_THETA ** (-jnp.arange(0, DA_HEAD_DIM, 2, dtype=F32) / DA_HEAD_DIM)
    freq_row = jnp.tile(inv_freq, 4).reshape(1, LANES)
    sign_row = jnp.tile(jnp.concatenate([-jnp.ones((32,), F32), jnp.ones((32,), F32)]), 2).reshape(1, LANES)
    pos_col = positions.astype(F32).reshape(t, 1)
    row = pl.BlockSpec((1, LANES), lambda i: (0, 0))
    return pl.pallas_call(
        _rope_table_kernel,
        out_shape=(jax.ShapeDtypeStruct((t, LANES), F32),) * 2,
        grid=(t // tm,),
        in_specs=[pl.BlockSpec((tm, 1), lambda i: (i, 0)), row, row],
        out_specs=(pl.BlockSpec((tm, LANES), lambda i: (i, 0)),) * 2,
        compiler_params=_cp("parallel"),
        name="rope_tables",
    )(pos_col, freq_row, sign_row)


def _rope_kernel(q_ref, k_ref, v_ref, cos_ref, sin_ref, qo_ref, ko_ref, vt_ref):
    cos = cos_ref[...]
    sin = sin_ref[...]
    lane = lax.broadcasted_iota(jnp.int32, cos.shape, 1)
    low = (lane % DA_HEAD_DIM) < (DA_HEAD_DIM // 2)

    def rope(t):
        swapped = jnp.where(low, pltpu.roll(t, LANES - 32, axis=1), pltpu.roll(t, 32, axis=1))
        return t * cos + swapped * sin

    scale = DA_HEAD_DIM ** -0.5 * LOG2E
    for h in range(DA_HEADS):
        sl = slice(h * LANES, (h + 1) * LANES)
        qo_ref[:, sl] = (rope(q_ref[:, sl]) * scale).astype(BF16)
        ko_ref[:, sl] = rope(k_ref[:, sl]).astype(BF16)
    vt_ref[...] = v_ref[...].T.astype(BF16)


def _rope(proj, cos_t, sin_t, seq, tm):
    t = proj.shape[0]
    nt = seq // tm
    col = lambda c: pl.BlockSpec((tm, DA_WIDTH), lambda i: (i, c))
    tab = pl.BlockSpec((tm, LANES), lambda i: (i, 0))
    out = pl.BlockSpec((tm, DA_WIDTH), lambda i: (i, 0))
    return pl.pallas_call(
        _rope_kernel,
        out_shape=(jax.ShapeDtypeStruct((t, DA_WIDTH), BF16),) * 2
        + (jax.ShapeDtypeStruct((t // seq * DA_WIDTH, seq), BF16),),
        grid=(t // tm,),
        in_specs=[col(COL_Q // DA_WIDTH), col(COL_K // DA_WIDTH), col(COL_V // DA_WIDTH), tab, tab],
        out_specs=(out, out, pl.BlockSpec((DA_WIDTH, tm), lambda i: (i // nt, i % nt))),
        compiler_params=_cp("parallel"),
        name="rope",
    )(proj, proj, proj, cos_t, sin_t)


ATT_TQ = 512
ATT_TK = 512
ATT_CW = 256


def _attn_kernel(lam_ref, g_ref, q_ref, k_ref, vt_ref, o_ref, qs_ref, p_ref, acc_ref, *, lambda_init):
    i = pl.program_id(2)
    nq = 2 * ATT_TQ
    q = q_ref[...]
    lane = lax.broadcasted_iota(jnp.int32, q.shape, 1)
    zero = jnp.zeros_like(q)
    qs_ref[:ATT_TQ, :] = jnp.where(lane < DA_HEAD_DIM, q, zero)
    qs_ref[ATT_TQ:, :] = jnp.where(lane >= DA_HEAD_DIM, q, zero)
    p_ref[...] = jnp.zeros_like(p_ref)
    acc_ref[...] = jnp.zeros_like(acc_ref)

    def pv_update(c, vt, alpha_prev):
        cs = slice(c * ATT_CW, (c + 1) * ATT_CW)
        pv = jnp.dot(vt, p_ref[:, cs], preferred_element_type=F32)
        acc_ref[:, cs] = alpha_prev[:, cs] * acc_ref[:, cs] + pv

    def block(j, carry, masked):
        m, l, alpha_prev = carry
        kb = k_ref[pl.ds(pl.multiple_of(j * ATT_TK, ATT_TK), ATT_TK), :]
        vt = vt_ref[:, pl.ds(pl.multiple_of(jnp.maximum(j - 1, 0) * ATT_TK, ATT_TK), ATT_TK)]
        ms, ls, alphas = [], [], []
        for c in range(nq // ATT_CW):
            cs = slice(c * ATT_CW, (c + 1) * ATT_CW)
            s = lax.dot_general(kb, qs_ref[cs, :], (((1,), (1,)), ((), ())), preferred_element_type=F32)
            if masked:
                krow = lax.broadcasted_iota(jnp.int32, s.shape, 0)
                qcol = (lax.broadcasted_iota(jnp.int32, s.shape, 1) + c * ATT_CW) % ATT_TQ
                s = jnp.where(krow // CHUNK <= qcol // CHUNK, s, -jnp.inf)
            pv_update(c, vt, alpha_prev)
            m_new = jnp.maximum(m[:, cs], jnp.max(s, axis=0, keepdims=True))
            alpha = jnp.exp2(m[:, cs] - m_new)
            p = jnp.exp2(s - m_new)
            ls.append(alpha * l[:, cs] + jnp.sum(p, axis=0, keepdims=True))
            p_ref[:, cs] = p.astype(BF16)
            ms.append(m_new)
            alphas.append(alpha)
        return jnp.concatenate(ms, axis=1), jnp.concatenate(ls, axis=1), jnp.concatenate(alphas, axis=1)

    def pair(j, carry, masked):
        return block(j + 1, block(j, carry, False), masked)

    init = (jnp.full((1, nq), -jnp.inf, F32), jnp.zeros((1, nq), F32), jnp.ones((1, nq), F32))
    n_blocks = i + 1
    odd = n_blocks % 2
    carry = lax.cond(n_blocks == 1, lambda c: block(0, c, True),
                     lambda c: lax.cond(odd == 1, lambda d: block(0, d, False), lambda d: d, c), init)
    carry = lax.fori_loop(0, n_blocks // 2 - 1, lambda t, c: pair(odd + 2 * t, c, False), carry)
    m, l, alpha = lax.cond(n_blocks >= 2, lambda c: pair(i - 1, c, True), lambda c: c, carry)
    vt = vt_ref[:, pl.ds(pl.multiple_of(i * ATT_TK, ATT_TK), ATT_TK)]
    for c in range(nq // ATT_CW):
        pv_update(c, vt, alpha)
    o = acc_ref[...] / l
    o = o[:, :ATT_TQ] - lam_ref[...] * o[:, ATT_TQ:]
    o = o * lax.rsqrt(jnp.mean(o * o, axis=0, keepdims=True) + RMS_EPS) * g_ref[...]
    o_ref[...] = (o * (1.0 - lambda_init)).T


def _attention(qr, kr, vt, lam, subln_g, lambda_init, bsz, seq):
    nq = seq // ATT_TQ
    col = pl.BlockSpec((LANES, 1), lambda b, h, i: (0, 0))
    qo = pl.BlockSpec((ATT_TQ, LANES), lambda b, h, i: (b * nq + i, h))
    return pl.pallas_call(
        functools.partial(_attn_kernel, lambda_init=lambda_init),
        out_shape=jax.ShapeDtypeStruct((bsz * seq, DA_WIDTH), F32),
        grid=(bsz, DA_HEADS, nq),
        in_specs=[pl.BlockSpec((1, 1), lambda b, h, i: (0, 0)), col, qo,
                  pl.BlockSpec((seq, LANES), lambda b, h, i: (b, h)),
                  pl.BlockSpec((DA_V_DIM, seq), lambda b, h, i: (b * DA_HEADS + h, 0))],
        out_specs=qo,
        scratch_shapes=[pltpu.VMEM((2 * ATT_TQ, LANES), BF16),
                        pltpu.VMEM((ATT_TK, 2 * ATT_TQ), BF16),
                        pltpu.VMEM((DA_V_DIM, 2 * ATT_TQ), F32)],
        compiler_params=_cp("parallel", "parallel", "arbitrary"),
        name="diff_attention",
    )(lam.reshape(1, 1), subln_g.reshape(LANES, 1), qr, kr, vt)


RW_VLO = 4
RW_VHI = RW_HEAD_DIM // RW_VLO
RW_BH = LANES // RW_VLO
RW_TB = 128
RW_PITCH = RW_HEAD_DIM + SUBLANES
RW_TT = 64
RW_NACC = 4
N_KOPS = 5


def _rw_prep_kernel(p_ref, halo_ref, mu_ref, w0_ref, a0_ref, kk_ref, ka_ref, rk_ref,
                    w2_ref, a2_ref, g2_ref, seg_ref,
                    kop_ref, v_ref, g_ref, bonus_ref, tr_ref):
    i = pl.program_id(0)
    n_op = pl.program_id(1)
    bsz = p_ref.shape[0]
    first = jnp.where(i == 0, 0.0, 1.0)
    seg = seg_ref[...]

    def per_batch(b, _):
        p = p_ref[b]
        row = lax.broadcasted_iota(jnp.int32, p.shape, 0)
        last_prev = jnp.broadcast_to(halo_ref[b, SUBLANES - 1:SUBLANES, :], p.shape) * first
        prev = jnp.where(row == 0, last_prev, pltpu.roll(p, 1, axis=0))
        p = p + (prev - p) * mu_ref[...]
        r = p[:, 0:RW_WIDTH]
        k = p[:, RW_WIDTH:2 * RW_WIDTH]
        v = p[:, 2 * RW_WIDTH:3 * RW_WIDTH]
        low = p[:, 3 * RW_WIDTH:RW_IN]
        decay = jnp.exp(-RW_DECAY_SCALE * jax.nn.sigmoid(w0_ref[...] + _dot_split(jnp.tanh(low), w2_ref)))
        a = jax.nn.sigmoid(a0_ref[...] + _dot_split(low, a2_ref))
        g_ref[b] = _dot_split(jax.nn.sigmoid(low), g2_ref)
        kk = k * kk_ref[...]
        kk = kk * lax.rsqrt(jnp.maximum(_segment_sum(kk * kk, seg), 1e-24))
        k = k * (1.0 + (a - 1.0) * ka_ref[...])
        bonus_ref[b] = _segment_sum(r * k * rk_ref[...], seg) * v
        for n, val in enumerate((r, decay, k, -kk, kk * a, v)):
            vt = val.T
            for h in range(RW_HEADS):
                base = pl.multiple_of((b * RW_HEADS + h) * RW_PITCH, SUBLANES)
                tr_ref[n, pl.ds(base, RW_HEAD_DIM), :] = vt[h * RW_HEAD_DIM:(h + 1) * RW_HEAD_DIM, :]
        return 0

    def v_relayout():
        for vh in range(RW_VHI):
            rows = [tr_ref[N_KOPS, pl.ds(vh * RW_VLO + vl, RW_BH, stride=RW_PITCH), :] for vl in range(RW_VLO)]
            v_ref[pl.ds(vh, RW_TB, stride=RW_VHI), :] = jnp.concatenate(rows, axis=0).T

    @pl.when(n_op == 0)
    def _():
        lax.fori_loop(0, bsz, per_batch, 0)
        v_relayout()

    def per_k_group(kg, _):
        for ks in range(SUBLANES):
            kk_i = kg * SUBLANES + ks
            rows = tr_ref[n_op, pl.ds(kk_i, RW_BH, stride=RW_PITCH), :]
            kop_ref[0, kk_i] = jnp.concatenate([rows] * RW_VLO, axis=0).T
        return 0
    lax.fori_loop(0, RW_HEAD_DIM // SUBLANES, per_k_group, 0)


def _head_segments():
    idx = np.arange(RW_WIDTH) // RW_HEAD_DIM
    return jnp.asarray((idx[:, None] == idx[None, :]).astype(np.float32)).astype(BF16)


def _rw_prep(proj3, mu, w0, w2, a0, a2, g2, k_k, k_a, r_k):
    bsz, seq, _ = proj3.shape
    hb = RW_TB // SUBLANES
    w2p = _hi_lo(jnp.zeros((LANES, RW_WIDTH), F32).at[0:32].set(w2))
    a2p = _hi_lo(jnp.zeros((LANES, RW_WIDTH), F32).at[32:64].set(a2))
    g2p = _hi_lo(jnp.zeros((LANES, RW_WIDTH), F32).at[64:128].set(g2))
    lowrank = pl.BlockSpec((2, LANES, RW_WIDTH), lambda i, j: (0, 0, 0))
    vec = lambda n: pl.BlockSpec((1, n), lambda i, j: (0, 0))
    mat = lambda r, c: pl.BlockSpec((r, c), lambda i, j: (0, 0))
    tok = pl.BlockSpec((bsz, RW_TB, RW_WIDTH), lambda i, j: (0, i, 0))
    tshape = jax.ShapeDtypeStruct((bsz, seq, RW_WIDTH), F32)
    return pl.pallas_call(
        _rw_prep_kernel,
        out_shape=(jax.ShapeDtypeStruct((N_KOPS, RW_HEAD_DIM, seq, LANES), F32),
                   jax.ShapeDtypeStruct((seq * RW_VHI, LANES), F32), tshape, tshape),
        grid=(seq // RW_TB, N_KOPS),
        in_specs=[pl.BlockSpec((bsz, RW_TB, RW_IN), lambda i, j: (0, i, COL_RW // RW_IN)),
                  pl.BlockSpec((bsz, SUBLANES, RW_IN),
                               lambda i, j: (0, jnp.maximum(i * hb - 1, 0), COL_RW // RW_IN)),
                  vec(RW_IN), vec(RW_WIDTH), vec(RW_WIDTH), vec(RW_WIDTH), vec(RW_WIDTH), vec(RW_WIDTH),
                  lowrank, lowrank, lowrank, mat(RW_WIDTH, RW_WIDTH)],
        out_specs=(pl.BlockSpec((1, RW_HEAD_DIM, RW_TB, LANES), lambda i, j: (j, 0, i, 0)),
                   pl.BlockSpec((RW_TB * RW_VHI, LANES), lambda i, j: (i, 0)), tok, tok),
        scratch_shapes=[pltpu.VMEM((N_KOPS + 1, RW_BH * RW_PITCH, RW_TB), F32)],
        compiler_params=_cp("parallel", "arbitrary"),
        name="rwkv_prep",
    )(proj3, proj3, mu.reshape(1, RW_IN), w0.reshape(1, -1), a0.reshape(1, -1), k_k.reshape(1, -1),
      k_a.reshape(1, -1), r_k.reshape(1, -1), w2p, a2p, g2p, _head_segments())


def _rw_scan_kernel(r_ref, w_ref, k_ref, an_ref, b_ref, v_ref, y_ref, s_ref):
    @pl.when(pl.program_id(0) == 0)
    def _():
        s_ref[...] = jnp.zeros_like(s_ref)

    nvb = RW_VHI // SUBLANES

    def bc(ref, t, kk):
        return jnp.broadcast_to(ref[kk, pl.ds(t, 1), :], (SUBLANES, LANES))

    def body(tb, _):
        for ts in range(SUBLANES):
            step(pl.multiple_of(tb * SUBLANES, SUBLANES) + ts)
        return 0

    def step(t):
        zacc = [[jnp.zeros((SUBLANES, LANES), F32) for _ in range(RW_NACC)] for _ in range(nvb)]
        for kk in range(RW_HEAD_DIM):
            an = bc(an_ref, t, kk)
            for vb in range(nvb):
                zacc[vb][kk % RW_NACC] += s_ref[kk, vb * SUBLANES:(vb + 1) * SUBLANES, :] * an
        z = [sum(zacc[vb][1:], zacc[vb][0]) for vb in range(nvb)]
        vt = [v_ref[t, vb * SUBLANES:(vb + 1) * SUBLANES, :] for vb in range(nvb)]
        yacc = [[jnp.zeros((SUBLANES, LANES), F32) for _ in range(RW_NACC)] for _ in range(nvb)]
        for kk in range(RW_HEAD_DIM):
            wr = bc(w_ref, t, kk)
            br = bc(b_ref, t, kk)
            kr = bc(k_ref, t, kk)
            rr = bc(r_ref, t, kk)
            for vb in range(nvb):
                sl = slice(vb * SUBLANES, (vb + 1) * SUBLANES)
                s_new = s_ref[kk, sl, :] * wr + z[vb] * br + vt[vb] * kr
                s_ref[kk, sl, :] = s_new
                yacc[vb][kk % RW_NACC] += s_new * rr
        for vb in range(nvb):
            y_ref[t, vb * SUBLANES:(vb + 1) * SUBLANES, :] = sum(yacc[vb][1:], yacc[vb][0])

    lax.fori_loop(0, RW_TT // SUBLANES, body, 0)


def _rw_scan(kops, v4, seq):
    kspec = lambda n: pl.BlockSpec((None, RW_HEAD_DIM, RW_TT, LANES), lambda i: (n, 0, i, 0))
    vspec = pl.BlockSpec((RW_TT, RW_VHI, LANES), lambda i: (i, 0, 0))
    return pl.pallas_call(
        _rw_scan_kernel,
        out_shape=jax.ShapeDtypeStruct((seq, RW_VHI, LANES), F32),
        grid=(seq // RW_TT,),
        in_specs=[kspec(n) for n in range(N_KOPS)] + [vspec],
        out_specs=vspec,
        scratch_shapes=[pltpu.VMEM((RW_HEAD_DIM, RW_VHI, LANES), F32)],
        compiler_params=_cp("arbitrary"),
        name="rwkv_scan",
    )(*([kops] * N_KOPS), v4.reshape(seq, RW_VHI, LANES))


def _rw_post_kernel(y_ref, bonus_ref, gate_ref, lg_ref, lb_ref, seg_ref, o_ref, tr_ref):
    bsz = bonus_ref.shape[0]

    for vh in range(RW_VHI):
        yt = y_ref[pl.ds(vh, RW_TB, stride=RW_VHI), :].T
        for vl in range(RW_VLO):
            tr_ref[pl.ds(vh * RW_VLO + vl, RW_BH, stride=RW_PITCH), :] = yt[vl * RW_BH:(vl + 1) * RW_BH, :]

    seg = seg_ref[...]
    head_mean = lambda a: _segment_sum(a, seg) * (1.0 / RW_HEAD_DIM)

    def per_batch(b, _):
        slabs = []
        for h in range(RW_HEADS):
            base = pl.multiple_of((b * RW_HEADS + h) * RW_PITCH, SUBLANES)
            slabs.append(tr_ref[pl.ds(base, RW_HEAD_DIM), :])
        y = jnp.concatenate(slabs, axis=0).T
        c = y - head_mean(y)
        var = head_mean(c * c)
        y = c * lax.rsqrt(var + RW_LNX_EPS) * lg_ref[...] + lb_ref[...]
        o_ref[b] = (y + bonus_ref[b]) * gate_ref[b]
        return 0
    lax.fori_loop(0, bsz, per_batch, 0)


def _rw_post(y4, bonus, gate, lnx_g, lnx_b):
    bsz, seq, _ = bonus.shape
    tok = pl.BlockSpec((bsz, RW_TB, RW_WIDTH), lambda i: (0, i, 0))
    vec = pl.BlockSpec((1, RW_WIDTH), lambda i: (0, 0))
    return pl.pallas_call(
        _rw_post_kernel,
        out_shape=jax.ShapeDtypeStruct((bsz, seq, RW_WIDTH), F32),
        grid=(seq // RW_TB,),
        in_specs=[pl.BlockSpec((RW_TB * RW_VHI, LANES), lambda i: (i, 0)), tok, tok, vec, vec,
                  pl.BlockSpec((RW_WIDTH, RW_WIDTH), lambda i: (0, 0))],
        out_specs=tok,
        scratch_shapes=[pltpu.VMEM((RW_BH * RW_PITCH, RW_TB), F32)],
        compiler_params=_cp("parallel"),
        name="rwkv_post",
    )(y4.reshape(seq * RW_VHI, LANES), bonus, gate, lnx_g.reshape(1, -1), lnx_b.reshape(1, -1), _head_segments())


S5_TT = 256


def _s5_kernel(u_ref, wb_ref, wc_ref, pw_ref, ap_ref, d_ref, gw_ref, gb_ref, og_ref,
               o_ref, x_ref, carry_ref):
    @pl.when(pl.program_id(1) == 0)
    def _():
        carry_ref[...] = jnp.zeros_like(carry_ref)

    n = S5_CPLX
    u = u_ref[...]
    bu = jnp.dot(u.astype(BF16), wb_ref[...], preferred_element_type=F32)
    groups = S5_TT // SUBLANES
    xr = bu[:, :n].reshape(groups, SUBLANES, n)
    xi = bu[:, n:].reshape(groups, SUBLANES, n)
    for step, s in enumerate((1, 2, 4)):
        ar = pw_ref[step, :, :n]
        ai = pw_ref[step, :, n:]
        sr = pltpu.roll(xr, s, axis=1)
        si = pltpu.roll(xi, s, axis=1)
        xr, xi = xr + (ar * sr - ai * si), xi + (ar * si + ai * sr)
    x_ref[:, :n] = xr.reshape(S5_TT, n)
    x_ref[:, n:] = xi.reshape(S5_TT, n)
    pr, pi = ap_ref[:, :n], ap_ref[:, n:]

    def group(gi, carry):
        cr, ci = carry
        rows = pl.ds(pl.multiple_of(gi * SUBLANES, SUBLANES), SUBLANES)
        gr = x_ref[rows, :n] + (pr * cr - pi * ci)
        gim = x_ref[rows, n:] + (pr * ci + pi * cr)
        x_ref[rows, :n] = gr
        x_ref[rows, n:] = gim
        return (jnp.broadcast_to(gr[SUBLANES - 1:SUBLANES, :], (SUBLANES, n)),
                jnp.broadcast_to(gim[SUBLANES - 1:SUBLANES, :], (SUBLANES, n)))

    cr, ci = lax.fori_loop(0, S5_TT // SUBLANES, group, (carry_ref[:, :n], carry_ref[:, n:]))
    carry_ref[:, :n] = cr
    carry_ref[:, n:] = ci
    y = jnp.dot(x_ref[...].astype(BF16), wc_ref[...], preferred_element_type=F32) + d_ref[...] * u
    z = jax.nn.gelu(y)
    z = z * jax.nn.sigmoid(jnp.dot(z.astype(BF16), gw_ref[...], preferred_element_type=F32) + gb_ref[...])
    o_ref[...] = z * lax.rsqrt(jnp.mean(z * z, axis=-1, keepdims=True) + RMS_EPS) * og_ref[...]


def _s5_tables(a_re, a_im, b_re, b_im, c_re, c_im, log_step):
    delta = jnp.exp(log_step.astype(F32))[:, None]
    mag = jnp.exp(delta * a_re)

    def a_pow(n):
        return (mag ** n) * jnp.cos(n * delta * a_im), (mag ** n) * jnp.sin(n * delta * a_im)

    abr, abi = a_pow(1)
    den = a_re * a_re + a_im * a_im
    qr = ((abr - 1.0) * a_re + abi * a_im) / den
    qi = (abi * a_re - (abr - 1.0) * a_im) / den
    bbr = qr[..., None] * b_re - qi[..., None] * b_im
    bbi = qr[..., None] * b_im + qi[..., None] * b_re
    eye = jnp.eye(S5_GROUPS, dtype=F32)
    blk_b = lambda m: jnp.einsum('gph,gk->ghkp', m, eye).reshape(S5_WIDTH, S5_CPLX)
    blk_c = lambda m: jnp.einsum('ghp,gk->gpkh', m, eye).reshape(S5_CPLX, S5_WIDTH)
    wb = jnp.concatenate([blk_b(bbr), blk_b(bbi)], axis=1).astype(BF16)
    wc = jnp.concatenate([blk_c(c_re), -blk_c(c_im)], axis=0).astype(BF16)
    flat = lambda pr: jnp.concatenate([pr[0].reshape(1, -1), pr[1].reshape(1, -1)], axis=1)
    row = jnp.arange(SUBLANES)[:, None]
    pw = jnp.stack([jnp.where(row >= s, flat(a_pow(s)), 0.0) for s in (1, 2, 4)])
    ap = jnp.concatenate([flat(a_pow(r + 1)) for r in range(SUBLANES)], axis=0)
    return wb, wc, pw, ap


def _s5(proj, tables, d_skip, glu_w, glu_b, out_g, bsz, seq):
    wb, wc, pw, ap = tables
    nt = seq // S5_TT
    full = lambda a: pl.BlockSpec(a.shape, lambda b, i: (0,) * a.ndim)
    vec = pl.BlockSpec((1, S5_WIDTH), lambda b, i: (0, 0))
    gw = glu_w.astype(BF16)
    return pl.pallas_call(
        _s5_kernel,
        out_shape=jax.ShapeDtypeStruct((bsz * seq, S5_WIDTH), F32),
        grid=(bsz, nt),
        in_specs=[pl.BlockSpec((S5_TT, S5_WIDTH), lambda b, i: (b * nt + i, COL_S5 // S5_WIDTH)),
                  full(wb), full(wc), full(pw), full(ap), vec, full(gw), vec, vec],
        out_specs=pl.BlockSpec((S5_TT, S5_WIDTH), lambda b, i: (b * nt + i, 0)),
        scratch_shapes=[pltpu.VMEM((S5_TT, 2 * S5_CPLX), F32),
                        pltpu.VMEM((SUBLANES, 2 * S5_CPLX), F32)],
        compiler_params=_cp("parallel", "arbitrary"),
        name="s5",
    )(proj, wb, wc, pw, ap, d_skip.reshape(1, -1), gw, glu_b.reshape(1, -1), out_g.reshape(1, -1))


def _xattn_kernel(x_ref, kv_ref, wq_ref, wo_ref, g_ref, b_ref, o_ref):
    x = x_ref[...]
    q = jnp.dot(x.astype(BF16), wq_ref[...], preferred_element_type=F32)
    outs = []
    for h in range(MEM_HEADS):
        sl = slice(h * MEM_HEAD_DIM, (h + 1) * MEM_HEAD_DIM)
        kh = kv_ref[:, sl]
        vh = kv_ref[:, D_MODEL + h * MEM_HEAD_DIM:D_MODEL + (h + 1) * MEM_HEAD_DIM]
        s = lax.dot_general(q[:, sl].astype(BF16), kh, (((1,), (1,)), ((), ())),
                            preferred_element_type=F32) * (MEM_HEAD_DIM ** -0.5)
        p = jnp.exp(s - jnp.max(s, axis=-1, keepdims=True))
        p = p / jnp.sum(p, axis=-1, keepdims=True)
        outs.append(jnp.dot(p.astype(BF16), vh, preferred_element_type=F32))
    o = jnp.concatenate(outs, axis=-1)
    h_out = jnp.dot(o.astype(BF16), wo_ref[...], preferred_element_type=F32)
    o_ref[...] = _layer_norm_rows(ALPHA * x + h_out, g_ref[...], b_ref[...])


def _cross_attention(x, kv, wq, wo, g, b, seq, tm):
    t, d = x.shape
    n_mem = kv.shape[0] // (t // seq)
    tiles_per_seq = seq // tm
    w = pl.BlockSpec((d, d), lambda i: (0, 0))
    vec = pl.BlockSpec((1, d), lambda i: (0, 0))
    blk = pl.BlockSpec((tm, d), lambda i: (i, 0))
    return pl.pallas_call(
        _xattn_kernel,
        out_shape=jax.ShapeDtypeStruct((t, d), F32),
        grid=(t // tm,),
        in_specs=[blk, pl.BlockSpec((n_mem, 2 * d), lambda i: (i // tiles_per_seq, 0)), w, w, vec, vec],
        out_specs=blk,
        compiler_params=_cp("parallel"),
        name="cross_attention",
    )(x, kv, wq, wo, g.reshape(1, d), b.reshape(1, d))


FFN_TN = 1408


def _ffn_up_kernel(x_ref, halo_ref, wa_ref, wg_ref, cw_ref, cb_ref, o_ref, *, tiles_per_seq):
    i = pl.program_id(0)
    x = x_ref[...].astype(BF16)
    first = jnp.where(i % tiles_per_seq == 0, 0.0, 1.0)
    xh = (halo_ref[...] * first).astype(BF16)
    a8 = jnp.dot(jnp.concatenate([xh, x], axis=0), wa_ref[...], preferred_element_type=F32)
    a0 = a8[SUBLANES:]
    a1 = pltpu.roll(a8, 1, axis=0)[SUBLANES:]
    a2 = pltpu.roll(a8, 2, axis=0)[SUBLANES:]
    conv = cb_ref[...] + cw_ref[0:1, :] * a2 + cw_ref[1:2, :] * a1 + cw_ref[2:3, :] * a0
    g = jnp.dot(x, wg_ref[...], preferred_element_type=F32)
    o_ref[...] = (jax.nn.silu(conv) * g).astype(o_ref.dtype)


def _ffn_up(x, wa, wg, conv_w, conv_b, seq, tm):
    t, d = x.shape
    hb = tm // SUBLANES
    cw = jnp.zeros((SUBLANES, D_FF), F32).at[0:3].set(conv_w)
    wspec = pl.BlockSpec((d, FFN_TN), lambda i, j: (0, j))
    return pl.pallas_call(
        functools.partial(_ffn_up_kernel, tiles_per_seq=seq // tm),
        out_shape=jax.ShapeDtypeStruct((t, D_FF), BF16),
        grid=(t // tm, D_FF // FFN_TN),
        in_specs=[pl.BlockSpec((tm, d), lambda i, j: (i, 0)),
                  pl.BlockSpec((SUBLANES, d), lambda i, j: (jnp.maximum(i * hb - 1, 0), 0)),
                  wspec, wspec,
                  pl.BlockSpec((SUBLANES, FFN_TN), lambda i, j: (0, j)),
                  pl.BlockSpec((1, FFN_TN), lambda i, j: (0, j))],
        out_specs=pl.BlockSpec((tm, FFN_TN), lambda i, j: (i, j)),
        compiler_params=_cp("parallel", "arbitrary"),
        name="ffn_up",
    )(x, x, wa, wg, cw, conv_b.reshape(1, D_FF))


def kernel(x, mem, positions, w_in, da_lam_q1, da_lam_k1, da_lam_q2, da_lam_k2, da_subln_g, rw_mu, rw_w0, rw_w2, rw_a0, rw_a2, rw_g2, rw_k_k, rw_k_a, rw_r_k, rw_lnx_g, rw_lnx_b, s5_a_re, s5_a_im, s5_b_re, s5_b_im, s5_c_re, s5_c_im, s5_d, s5_log_step, s5_glu_w, s5_glu_b, s5_out_g, w_out, ln1_g, ln1_b, ca_wq, ca_wkv, ca_wo, ln2_g, ln2_b, ffn_w_up, ffn_conv_w, ffn_conv_b, ffn_w_down, ln3_g, ln3_b):
    bsz, seq, d = x.shape
    t = bsz * seq
    tm = min(512, seq)
    n_mem = mem.shape[1]
    xf = x.reshape(t, d)
    memf = mem.reshape(bsz * n_mem, d)
    cos_t, sin_t = _rope_tables(positions, tm)

    for l in range(DEPTH):
        lambda_init = 0.8 - 0.6 * math.exp(-0.3 * l)
        wi = w_in[l]
        w_perm = jnp.concatenate([wi[:, :3 * DA_WIDTH], wi[:, 3 * DA_WIDTH + RW_IN:],
                                  wi[:, 3 * DA_WIDTH:3 * DA_WIDTH + RW_IN]], axis=1).astype(BF16)
        proj = _matmul(xf, w_perm, F32, tm, IN_WIDTH)

        qr, kr, vt = _rope(proj, cos_t, sin_t, seq, tm)
        lam = (jnp.exp(jnp.sum(da_lam_q1[l] * da_lam_k1[l])) - jnp.exp(jnp.sum(da_lam_q2[l] * da_lam_k2[l]))
               + lambda_init)
        h_da = _attention(qr, kr, vt, lam, da_subln_g[l], lambda_init, bsz, seq)

        kops, v4, gate, bonus = _rw_prep(proj.reshape(bsz, seq, IN_WIDTH), rw_mu[l], rw_w0[l], rw_w2[l],
                                         rw_a0[l], rw_a2[l], rw_g2[l], rw_k_k[l], rw_k_a[l],
                                         rw_r_k[l].reshape(-1))
        y4 = _rw_scan(kops, v4, seq)
        h_rw = _rw_post(y4, bonus, gate, rw_lnx_g[l], rw_lnx_b[l]).reshape(t, RW_WIDTH)

        tables = _s5_tables(s5_a_re[l], s5_a_im[l], s5_b_re[l], s5_b_im[l], s5_c_re[l], s5_c_im[l],
                            s5_log_step[l])
        h_s5 = _s5(proj, tables, s5_d[l], s5_glu_w[l], s5_glu_b[l], s5_out_g[l], bsz, seq)

        wo = w_out[l].astype(BF16)
        xf = _matmul_ln([h_da, h_rw, h_s5],
                        [wo[:DA_WIDTH], wo[DA_WIDTH:DA_WIDTH + RW_WIDTH], wo[DA_WIDTH + RW_WIDTH:]],
                        xf, ln1_g[l], ln1_b[l], tm)
        kv = _matmul(memf, ca_wkv[l].astype(BF16), BF16, n_mem, 1024)
        xf = _cross_attention(xf, kv, ca_wq[l].astype(BF16), ca_wo[l].astype(BF16), ln2_g[l], ln2_b[l], seq, tm)
        w_up = ffn_w_up[l].astype(BF16)
        hff = _ffn_up(xf, w_up[:, :D_FF], w_up[:, D_FF:], ffn_conv_w[l], ffn_conv_b[l], seq, tm)
        xf = _matmul_ln([hff], [ffn_w_down[l].astype(BF16)], xf, ln3_g[l], ln3_b[l], tm)
    return xf.reshape(bsz, seq, d)
```

```python
import functools
import math

import numpy as np
import jax
import jax.numpy as jnp
from jax import lax
from jax.experimental import pallas as pl
from jax.experimental.pallas import tpu as pltpu

F32 = jnp.float32
BF16 = jnp.bfloat16
HI = lax.Precision.HIGHEST

D_MODEL = 1024
DEPTH = 4
CHUNK = 64
LN_EPS = 1e-5
LOG2E = math.log2(math.e)
RMS_EPS = 1e-6

DA_HEAD_DIM = 64
DA_V_DIM = 128
DA_WIDTH = 512
DA_HEADS = 4
ROPE_THETA = 10000.0

RW_HEAD_DIM = 64
RW_WIDTH = 256
RW_HEADS = 4
RW_DECAY_RANK = 32
RW_AAA_RANK = 32
RW_GATE_RANK = 64
RW_IN = 3 * RW_WIDTH + RW_DECAY_RANK + RW_AAA_RANK + RW_GATE_RANK
RW_DECAY_SCALE = math.exp(-0.5)
RW_LNX_EPS = 64e-5

S5_WIDTH = 256
S5_GROUP_CH = 16
S5_GROUPS = 16
S5_STATE = 64
S5_CPLX = S5_GROUPS * S5_STATE

IN_WIDTH = 3 * DA_WIDTH + RW_IN + S5_WIDTH
MEM_HEADS = 4
MEM_HEAD_DIM = 256
D_FF = 2816
ALPHA = (2.0 * DEPTH) ** 0.25

COL_Q, COL_K, COL_V, COL_S5, COL_RW = 0, 512, 1024, 1536, 1792

VMEM_LIMIT = 48 * 1024 * 1024
LANES = 128
SUBLANES = 8


def _cp(*sem):
    return pltpu.CompilerParams(dimension_semantics=sem, vmem_limit_bytes=VMEM_LIMIT)


def _layer_norm_rows(v, g, b):
    mu = jnp.mean(v, axis=-1, keepdims=True)
    c = v - mu
    var = jnp.mean(c * c, axis=-1, keepdims=True)
    return c * lax.rsqrt(var + LN_EPS) * g + b


def _split3(x):
    p1 = x.astype(BF16)
    r1 = x - p1.astype(F32)
    p2 = r1.astype(BF16)
    p3 = (r1 - p2.astype(F32)).astype(BF16)
    return p1, p2, p3


def _segment_sum(x, seg):
    return sum(jnp.dot(p, seg, preferred_element_type=F32) for p in _split3(x))


def _dot_split(a, w_ref):
    a1, a2, _ = _split3(a)
    d = functools.partial(jnp.dot, preferred_element_type=F32)
    return d(a1, w_ref[0]) + (d(a1, w_ref[1]) + d(a2, w_ref[0]))


def _hi_lo(w):
    hi = w.astype(BF16)
    return jnp.stack([hi, (w - hi.astype(F32)).astype(BF16)])


def _mm_kernel(a_ref, w_ref, o_ref):
    o_ref[...] = jnp.dot(a_ref[...].astype(BF16), w_ref[...],
                         preferred_element_type=F32).astype(o_ref.dtype)


def _matmul(a, w, out_dtype, tm, tn):
    m, k = a.shape
    n = w.shape[1]
    return pl.pallas_call(
        _mm_kernel,
        out_shape=jax.ShapeDtypeStruct((m, n), out_dtype),
        grid=(m // tm, n // tn),
        in_specs=[pl.BlockSpec((tm, k), lambda i, j: (i, 0)),
                  pl.BlockSpec((k, tn), lambda i, j: (0, j))],
        out_specs=pl.BlockSpec((tm, tn), lambda i, j: (i, j)),
        compiler_params=_cp("parallel", "arbitrary"),
        name="matmul",
    )(a, w)


def _mm_ln_kernel(n_in, *refs):
    a_refs = refs[:n_in]
    w_refs = refs[n_in:2 * n_in]
    x_ref, g_ref, b_ref, o_ref = refs[2 * n_in:]
    acc = ALPHA * x_ref[...]
    for a_ref, w_ref in zip(a_refs, w_refs):
        acc = acc + jnp.dot(a_ref[...].astype(BF16), w_ref[...], preferred_element_type=F32)
    o_ref[...] = _layer_norm_rows(acc, g_ref[...], b_ref[...])


def _matmul_ln(a_list, w_list, x, g, b, tm):
    m, d = x.shape
    n_in = len(a_list)
    in_specs = [pl.BlockSpec((tm, a.shape[1]), lambda i: (i, 0)) for a in a_list]
    in_specs += [pl.BlockSpec(w.shape, lambda i: (0, 0)) for w in w_list]
    in_specs += [pl.BlockSpec((tm, d), lambda i: (i, 0)),
                 pl.BlockSpec((1, d), lambda i: (0, 0)),
                 pl.BlockSpec((1, d), lambda i: (0, 0))]
    return pl.pallas_call(
        functools.partial(_mm_ln_kernel, n_in),
        out_shape=jax.ShapeDtypeStruct((m, d), F32),
        grid=(m // tm,),
        in_specs=in_specs,
        out_specs=pl.BlockSpec((tm, d), lambda i: (i, 0)),
        compiler_params=_cp("parallel"),
        name="matmul_ln",
    )(*a_list, *w_list, x, g.reshape(1, d), b.reshape(1, d))


def _rope_table_kernel(pos_ref, freq_ref, sign_ref, cos_ref, sin_ref):
    ang = pos_ref[...] * freq_ref[...]
    cos_ref[...] = jnp.cos(ang)
    sin_ref[...] = jnp.sin(ang) * sign_ref[...]


def _rope_tables(positions, tm):
    t = positions.size
    inv_freq = ROPE_THETA ** (-jnp.arange(0, DA_HEAD_DIM, 2, dtype=F32) / DA_HEAD_DIM)
    freq_row = jnp.tile(inv_freq, 4).reshape(1, LANES)
    sign_row = jnp.tile(jnp.concatenate([-jnp.ones((32,), F32), jnp.ones((32,), F32)]), 2).reshape(1, LANES)
    pos_col = positions.astype(F32).reshape(t, 1)
    row = pl.BlockSpec((1, LANES), lambda i: (0, 0))
    return pl.pallas_call(
        _rope_table_kernel,
        out_shape=(jax.ShapeDtypeStruct((t, LANES), F32),) * 2,
        grid=(t // tm,),
        in_specs=[pl.BlockSpec((tm, 1), lambda i: (i, 0)), row, row],
        out_specs=(pl.BlockSpec((tm, LANES), lambda i: (i, 0)),) * 2,
        compiler_params=_cp("parallel"),
        name="rope_tables",
    )(pos_col, freq_row, sign_row)


def _rope_kernel(q_ref, k_ref, v_ref, cos_ref, sin_ref, qo_ref, ko_ref, vt_ref):
    cos = cos_ref[...]
    sin = sin_ref[...]
    lane = lax.broadcasted_iota(jnp.int32, cos.shape, 1)
    low = (lane % DA_HEAD_DIM) < (DA_HEAD_DIM // 2)

    def rope(t):
        swapped = jnp.where(low, pltpu.roll(t, LANES - 32, axis=1), pltpu.roll(t, 32, axis=1))
        return t * cos + swapped * sin

    scale = DA_HEAD_DIM ** -0.5 * LOG2E
    for h in range(DA_HEADS):
        sl = slice(h * LANES, (h + 1) * LANES)
        qo_ref[:, sl] = (rope(q_ref[:, sl]) * scale).astype(BF16)
        ko_ref[:, sl] = rope(k_ref[:, sl]).astype(BF16)
    vt_ref[...] = v_ref[...].T.astype(BF16)


def _rope(proj, cos_t, sin_t, seq, tm):
    t = proj.shape[0]
    nt = seq // tm
    col = lambda c: pl.BlockSpec((tm, DA_WIDTH), lambda i: (i, c))
    tab = pl.BlockSpec((tm, LANES), lambda i: (i, 0))
    out = pl.BlockSpec((tm, DA_WIDTH), lambda i: (i, 0))
    return pl.pallas_call(
        _rope_kernel,
        out_shape=(jax.ShapeDtypeStruct((t, DA_WIDTH), BF16),) * 2
        + (jax.ShapeDtypeStruct((t // seq * DA_WIDTH, seq), BF16),),
        grid=(t // tm,),
        in_specs=[col(COL_Q // DA_WIDTH), col(COL_K // DA_WIDTH), col(COL_V // DA_WIDTH), tab, tab],
        out_specs=(out, out, pl.BlockSpec((DA_WIDTH, tm), lambda i: (i // nt, i % nt))),
        compiler_params=_cp("parallel"),
        name="rope",
    )(proj, proj, proj, cos_t, sin_t)


ATT_TQ = 1024
ATT_TK = 512
ATT_CW = 256
ATT_KB = ATT_TQ // ATT_TK


def _attn_kernel(lam_ref, g_ref, q_ref, k_ref, vt_ref, o_ref, qs_ref, p_ref, acc_ref, *, lambda_init):
    i = pl.program_id(2)
    nq = 2 * ATT_TQ
    q = q_ref[...]
    lane = lax.broadcasted_iota(jnp.int32, q.shape, 1)
    zero = jnp.zeros_like(q)
    qs_ref[:ATT_TQ, :] = jnp.where(lane < DA_HEAD_DIM, q, zero)
    qs_ref[ATT_TQ:, :] = jnp.where(lane >= DA_HEAD_DIM, q, zero)
    p_ref[...] = jnp.zeros_like(p_ref)
    acc_ref[...] = jnp.zeros_like(acc_ref)

    def pv_update(c, vt, alpha_prev):
        cs = slice(c * ATT_CW, (c + 1) * ATT_CW)
        pv = jnp.dot(vt, p_ref[:, cs], preferred_element_type=F32)
        acc_ref[:, cs] = alpha_prev[:, cs] * acc_ref[:, cs] + pv

    def block(j, carry, diag):
        m, l, alpha_prev = carry
        kb = k_ref[pl.ds(pl.multiple_of(j * ATT_TK, ATT_TK), ATT_TK), :]
        vt = vt_ref[:, pl.ds(pl.multiple_of(jnp.maximum(j - 1, 0) * ATT_TK, ATT_TK), ATT_TK)]
        ms, ls, alphas = [], [], []
        for c in range(nq // ATT_CW):
            cs = slice(c * ATT_CW, (c + 1) * ATT_CW)
            s = lax.dot_general(kb, qs_ref[cs, :], (((1,), (1,)), ((), ())), preferred_element_type=F32)
            if diag is not None:
                krow = lax.broadcasted_iota(jnp.int32, s.shape, 0) + diag * ATT_TK
                qcol = (lax.broadcasted_iota(jnp.int32, s.shape, 1) + c * ATT_CW) % ATT_TQ
                s = jnp.where(krow // CHUNK <= qcol // CHUNK, s, -jnp.inf)
            pv_update(c, vt, alpha_prev)
            m_new = jnp.maximum(m[:, cs], jnp.max(s, axis=0, keepdims=True))
            alpha = jnp.exp2(m[:, cs] - m_new)
            p = jnp.exp2(s - m_new)
            ls.append(alpha * l[:, cs] + jnp.sum(p, axis=0, keepdims=True))
            p_ref[:, cs] = p.astype(BF16)
            ms.append(m_new)
            alphas.append(alpha)
        return jnp.concatenate(ms, axis=1), jnp.concatenate(ls, axis=1), jnp.concatenate(alphas, axis=1)

    def trip(t, carry, masked):
        for d in range(ATT_KB):
            carry = block(ATT_KB * t + d, carry, d if masked else None)
        return carry

    init = (jnp.full((1, nq), -jnp.inf, F32), jnp.zeros((1, nq), F32), jnp.ones((1, nq), F32))
    carry = lax.fori_loop(0, i, lambda t, c: trip(t, c, False), init)
    m, l, alpha = trip(i, carry, True)
    last = ATT_KB * i + ATT_KB - 1
    vt = vt_ref[:, pl.ds(pl.multiple_of(last * ATT_TK, ATT_TK), ATT_TK)]
    for c in range(nq // ATT_CW):
        pv_update(c, vt, alpha)
    o = acc_ref[...] / l
    o = o[:, :ATT_TQ] - lam_ref[...] * o[:, ATT_TQ:]
    o = o * lax.rsqrt(jnp.mean(o * o, axis=0, keepdims=True) + RMS_EPS) * g_ref[...]
    o_ref[...] = (o * (1.0 - lambda_init)).T


def _attention(qr, kr, vt, lam, subln_g, lambda_init, bsz, seq):
    nq = seq // ATT_TQ
    col = pl.BlockSpec((LANES, 1), lambda b, h, i: (0, 0))
    qo = pl.BlockSpec((ATT_TQ, LANES), lambda b, h, i: (b * nq + i, h))
    return pl.pallas_call(
        functools.partial(_attn_kernel, lambda_init=lambda_init),
        out_shape=jax.ShapeDtypeStruct((bsz * seq, DA_WIDTH), F32),
        grid=(bsz, DA_HEADS, nq),
        in_specs=[pl.BlockSpec((1, 1), lambda b, h, i: (0, 0)), col, qo,
                  pl.BlockSpec((seq, LANES), lambda b, h, i: (b, h)),
                  pl.BlockSpec((DA_V_DIM, seq), lambda b, h, i: (b * DA_HEADS + h, 0))],
        out_specs=qo,
        scratch_shapes=[pltpu.VMEM((2 * ATT_TQ, LANES), BF16),
                        pltpu.VMEM((ATT_TK, 2 * ATT_TQ), BF16),
                        pltpu.VMEM((DA_V_DIM, 2 * ATT_TQ), F32)],
        compiler_params=_cp("parallel", "parallel", "arbitrary"),
        name="diff_attention",
    )(lam.reshape(1, 1), subln_g.reshape(LANES, 1), qr, kr, vt)


RW_VLO = 4
RW_VHI = RW_HEAD_DIM // RW_VLO
RW_BH = LANES // RW_VLO
RW_TB = 128
RW_PITCH = RW_HEAD_DIM + SUBLANES
RW_TT = 64
RW_NACC = 4
N_KOPS = 5


def _rw_prep_kernel(p_ref, halo_ref, mu_ref, w0_ref, a0_ref, kk_ref, ka_ref, rk_ref,
                    w2_ref, a2_ref, g2_ref, seg_ref,
                    kop_ref, v_ref, g_ref, bonus_ref, tr_ref):
    i = pl.program_id(0)
    n_op = pl.program_id(1)
    bsz = p_ref.shape[0]
    first = jnp.where(i == 0, 0.0, 1.0)
    seg = seg_ref[...]

    def per_batch(b, _):
        p = p_ref[b]
        row = lax.broadcasted_iota(jnp.int32, p.shape, 0)
        last_prev = jnp.broadcast_to(halo_ref[b, SUBLANES - 1:SUBLANES, :], p.shape) * first
        prev = jnp.where(row == 0, last_prev, pltpu.roll(p, 1, axis=0))
        p = p + (prev - p) * mu_ref[...]
        r = p[:, 0:RW_WIDTH]
        k = p[:, RW_WIDTH:2 * RW_WIDTH]
        v = p[:, 2 * RW_WIDTH:3 * RW_WIDTH]
        low = p[:, 3 * RW_WIDTH:RW_IN]
        decay = jnp.exp(-RW_DECAY_SCALE * jax.nn.sigmoid(w0_ref[...] + _dot_split(jnp.tanh(low), w2_ref)))
        a = jax.nn.sigmoid(a0_ref[...] + _dot_split(low, a2_ref))
        g_ref[b] = _dot_split(jax.nn.sigmoid(low), g2_ref)
        kk = k * kk_ref[...]
        kk = kk * lax.rsqrt(jnp.maximum(_segment_sum(kk * kk, seg), 1e-24))
        k = k * (1.0 + (a - 1.0) * ka_ref[...])
        bonus_ref[b] = _segment_sum(r * k * rk_ref[...], seg) * v
        for n, val in enumerate((r, decay, k, -kk, kk * a, v)):
            vt = val.T
            for h in range(RW_HEADS):
                base = pl.multiple_of((b * RW_HEADS + h) * RW_PITCH, SUBLANES)
                tr_ref[n, pl.ds(base, RW_HEAD_DIM), :] = vt[h * RW_HEAD_DIM:(h + 1) * RW_HEAD_DIM, :]
        return 0

    def v_relayout():
        for vh in range(RW_VHI):
            rows = [tr_ref[N_KOPS, pl.ds(vh * RW_VLO + vl, RW_BH, stride=RW_PITCH), :] for vl in range(RW_VLO)]
            v_ref[pl.ds(vh, RW_TB, stride=RW_VHI), :] = jnp.concatenate(rows, axis=0).T

    @pl.when(n_op == 0)
    def _():
        lax.fori_loop(0, bsz, per_batch, 0)
        v_relayout()

    def per_k_group(kg, _):
        for ks in range(SUBLANES):
            kk_i = kg * SUBLANES + ks
            rows = tr_ref[n_op, pl.ds(kk_i, RW_BH, stride=RW_PITCH), :]
            kop_ref[0, kk_i] = jnp.concatenate([rows] * RW_VLO, axis=0).T
        return 0
    lax.fori_loop(0, RW_HEAD_DIM // SUBLANES, per_k_group, 0)


def _head_segments():
    idx = np.arange(RW_WIDTH) // RW_HEAD_DIM
    return jnp.asarray((idx[:, None] == idx[None, :]).astype(np.float32)).astype(BF16)


def _rw_prep(proj3, mu, w0, w2, a0, a2, g2, k_k, k_a, r_k):
    bsz, seq, _ = proj3.shape
    hb = RW_TB // SUBLANES
    w2p = _hi_lo(jnp.zeros((LANES, RW_WIDTH), F32).at[0:32].set(w2))
    a2p = _hi_lo(jnp.zeros((LANES, RW_WIDTH), F32).at[32:64].set(a2))
    g2p = _hi_lo(jnp.zeros((LANES, RW_WIDTH), F32).at[64:128].set(g2))
    lowrank = pl.BlockSpec((2, LANES, RW_WIDTH), lambda i, j: (0, 0, 0))
    vec = lambda n: pl.BlockSpec((1, n), lambda i, j: (0, 0))
    mat = lambda r, c: pl.BlockSpec((r, c), lambda i, j: (0, 0))
    tok = pl.BlockSpec((bsz, RW_TB, RW_WIDTH), lambda i, j: (0, i, 0))
    tshape = jax.ShapeDtypeStruct((bsz, seq, RW_WIDTH), F32)
    return pl.pallas_call(
        _rw_prep_kernel,
        out_shape=(jax.ShapeDtypeStruct((N_KOPS, RW_HEAD_DIM, seq, LANES), F32),
                   jax.ShapeDtypeStruct((seq * RW_VHI, LANES), F32), tshape, tshape),
        grid=(seq // RW_TB, N_KOPS),
        in_specs=[pl.BlockSpec((bsz, RW_TB, RW_IN), lambda i, j: (0, i, COL_RW // RW_IN)),
                  pl.BlockSpec((bsz, SUBLANES, RW_IN),
                               lambda i, j: (0, jnp.maximum(i * hb - 1, 0), COL_RW // RW_IN)),
                  vec(RW_IN), vec(RW_WIDTH), vec(RW_WIDTH), vec(RW_WIDTH), vec(RW_WIDTH), vec(RW_WIDTH),
                  lowrank, lowrank, lowrank, mat(RW_WIDTH, RW_WIDTH)],
        out_specs=(pl.BlockSpec((1, RW_HEAD_DIM, RW_TB, LANES), lambda i, j: (j, 0, i, 0)),
                   pl.BlockSpec((RW_TB * RW_VHI, LANES), lambda i, j: (i, 0)), tok, tok),
        scratch_shapes=[pltpu.VMEM((N_KOPS + 1, RW_BH * RW_PITCH, RW_TB), F32)],
        compiler_params=_cp("parallel", "arbitrary"),
        name="rwkv_prep",
    )(proj3, proj3, mu.reshape(1, RW_IN), w0.reshape(1, -1), a0.reshape(1, -1), k_k.reshape(1, -1),
      k_a.reshape(1, -1), r_k.reshape(1, -1), w2p, a2p, g2p, _head_segments())


def _rw_scan_step(r_ref, w_ref, k_ref, an_ref, b_ref, v_ref, y_ref, s_ref, t):
    nvb = RW_VHI // SUBLANES

    def bc(ref, kk):
        return jnp.broadcast_to(ref[kk, pl.ds(t, 1), :], (SUBLANES, LANES))

    zacc = [[jnp.zeros((SUBLANES, LANES), F32) for _ in range(RW_NACC)] for _ in range(nvb)]
    for kk in range(RW_HEAD_DIM):
        an = bc(an_ref, kk)
        for vb in range(nvb):
            zacc[vb][kk % RW_NACC] += s_ref[kk, vb * SUBLANES:(vb + 1) * SUBLANES, :] * an
    z = [sum(zacc[vb][1:], zacc[vb][0]) for vb in range(nvb)]
    vt = [v_ref[t, vb * SUBLANES:(vb + 1) * SUBLANES, :] for vb in range(nvb)]
    yacc = [[jnp.zeros((SUBLANES, LANES), F32) for _ in range(RW_NACC)] for _ in range(nvb)]
    for kk in range(RW_HEAD_DIM):
        wr = bc(w_ref, kk)
        br = bc(b_ref, kk)
        kr = bc(k_ref, kk)
        rr = bc(r_ref, kk)
        for vb in range(nvb):
            sl = slice(vb * SUBLANES, (vb + 1) * SUBLANES)
            s_new = s_ref[kk, sl, :] * wr + z[vb] * br + vt[vb] * kr
            s_ref[kk, sl, :] = s_new
            yacc[vb][kk % RW_NACC] += s_new * rr
    for vb in range(nvb):
        y_ref[t, vb * SUBLANES:(vb + 1) * SUBLANES, :] = sum(yacc[vb][1:], yacc[vb][0])


def _rw_scan_kernel(r_ref, w_ref, k_ref, an_ref, b_ref, v_ref, y_ref, s_ref):
    @pl.when(pl.program_id(0) == 0)
    def _():
        s_ref[...] = jnp.zeros_like(s_ref)

    def body(tb, _):
        base = pl.multiple_of(tb * SUBLANES, SUBLANES)
        for ts in range(SUBLANES):
            _rw_scan_step(r_ref, w_ref, k_ref, an_ref, b_ref, v_ref, y_ref, s_ref, base + ts)
        return 0

    lax.fori_loop(0, RW_TT // SUBLANES, body, 0)


def _rw_scan(kops, v4, seq):
    kspec = lambda n: pl.BlockSpec((None, RW_HEAD_DIM, RW_TT, LANES), lambda i: (n, 0, i, 0))
    vspec = pl.BlockSpec((RW_TT, RW_VHI, LANES), lambda i: (i, 0, 0))
    return pl.pallas_call(
        _rw_scan_kernel,
        out_shape=jax.ShapeDtypeStruct((seq, RW_VHI, LANES), F32),
        grid=(seq // RW_TT,),
        in_specs=[kspec(n) for n in range(N_KOPS)] + [vspec],
        out_specs=vspec,
        scratch_shapes=[pltpu.VMEM((RW_HEAD_DIM, RW_VHI, LANES), F32)],
        compiler_params=_cp("arbitrary"),
        name="rwkv_scan",
    )(*([kops] * N_KOPS), v4.reshape(seq, RW_VHI, LANES))


def _rw_post_kernel(y_ref, bonus_ref, gate_ref, lg_ref, lb_ref, seg_ref, o_ref, tr_ref):
    bsz = bonus_ref.shape[0]

    for vh in range(RW_VHI):
        yt = y_ref[pl.ds(vh, RW_TB, stride=RW_VHI), :].T
        for vl in range(RW_VLO):
            tr_ref[pl.ds(vh * RW_VLO + vl, RW_BH, stride=RW_PITCH), :] = yt[vl * RW_BH:(vl + 1) * RW_BH, :]

    seg = seg_ref[...]
    head_mean = lambda a: _segment_sum(a, seg) * (1.0 / RW_HEAD_DIM)

    def per_batch(b, _):
        slabs = []
        for h in range(RW_HEADS):
            base = pl.multiple_of((b * RW_HEADS + h) * RW_PITCH, SUBLANES)
            slabs.append(tr_ref[pl.ds(base, RW_HEAD_DIM), :])
        y = jnp.concatenate(slabs, axis=0).T
        c = y - head_mean(y)
        var = head_mean(c * c)
        y = c * lax.rsqrt(var + RW_LNX_EPS) * lg_ref[...] + lb_ref[...]
        o_ref[b] = (y + bonus_ref[b]) * gate_ref[b]
        return 0
    lax.fori_loop(0, bsz, per_batch, 0)


def _rw_post(y4, bonus, gate, lnx_g, lnx_b):
    bsz, seq, _ = bonus.shape
    tok = pl.BlockSpec((bsz, RW_TB, RW_WIDTH), lambda i: (0, i, 0))
    vec = pl.BlockSpec((1, RW_WIDTH), lambda i: (0, 0))
    return pl.pallas_call(
        _rw_post_kernel,
        out_shape=jax.ShapeDtypeStruct((bsz, seq, RW_WIDTH), F32),
        grid=(seq // RW_TB,),
        in_specs=[pl.BlockSpec((RW_TB * RW_VHI, LANES), lambda i: (i, 0)), tok, tok, vec, vec,
                  pl.BlockSpec((RW_WIDTH, RW_WIDTH), lambda i: (0, 0))],
        out_specs=tok,
        scratch_shapes=[pltpu.VMEM((RW_BH * RW_PITCH, RW_TB), F32)],
        compiler_params=_cp("parallel"),
        name="rwkv_post",
    )(y4.reshape(seq * RW_VHI, LANES), bonus, gate, lnx_g.reshape(1, -1), lnx_b.reshape(1, -1), _head_segments())


S5_TT = 128
S5_PITCH = S5_TT + SUBLANES


def _s5_kernel(u_ref, wb_ref, wc_ref, a_ref, d_ref, gw_ref, gb_ref, og_ref, o_ref, x_ref, carry_ref):
    @pl.when(pl.program_id(0) == 0)
    def _():
        carry_ref[...] = jnp.zeros_like(carry_ref)

    n = S5_CPLX
    nc = n // LANES
    bsz = u_ref.shape[0]

    def project(b, _):
        base = b * S5_PITCH
        bu = jnp.dot(u_ref[b].astype(BF16), wb_ref[...], preferred_element_type=F32)
        for c in range(2 * nc):
            x_ref[c, pl.ds(base, S5_TT), :] = bu[:, c * LANES:(c + 1) * LANES]
        return 0
    for b in range(bsz):
        project(b, 0)

    ar = [jnp.broadcast_to(a_ref[:, c * LANES:(c + 1) * LANES], (bsz, LANES)) for c in range(nc)]
    ai = [jnp.broadcast_to(a_ref[:, n + c * LANES:n + (c + 1) * LANES], (bsz, LANES)) for c in range(nc)]

    def steps(tb, carry):
        cr, ci = list(carry[0]), list(carry[1])
        for ts in range(SUBLANES):
            rows = pl.ds(tb * SUBLANES + ts, bsz, stride=S5_PITCH)
            for c in range(nc):
                cr[c], ci[c] = (ar[c] * cr[c] - ai[c] * ci[c] + x_ref[c, rows, :],
                                ar[c] * ci[c] + ai[c] * cr[c] + x_ref[nc + c, rows, :])
                x_ref[c, rows, :] = cr[c]
                x_ref[nc + c, rows, :] = ci[c]
        return tuple(cr), tuple(ci)

    init = (tuple(carry_ref[c] for c in range(nc)), tuple(carry_ref[nc + c] for c in range(nc)))
    cr, ci = lax.fori_loop(0, S5_TT // SUBLANES, steps, init)
    for c in range(nc):
        carry_ref[c] = cr[c]
        carry_ref[nc + c] = ci[c]

    def readout(b, _):
        base = b * S5_PITCH
        u = u_ref[b]
        x = jnp.concatenate([x_ref[c, pl.ds(base, S5_TT), :] for c in range(2 * nc)], axis=1)
        y = jnp.dot(x.astype(BF16), wc_ref[...], preferred_element_type=F32) + d_ref[...] * u
        z = jax.nn.gelu(y)
        z = z * jax.nn.sigmoid(jnp.dot(z.astype(BF16), gw_ref[...], preferred_element_type=F32) + gb_ref[...])
        o_ref[b] = z * lax.rsqrt(jnp.mean(z * z, axis=-1, keepdims=True) + RMS_EPS) * og_ref[...]
        return 0
    for b in range(bsz):
        readout(b, 0)


def _s5_tables(a_re, a_im, b_re, b_im, c_re, c_im, log_step):
    delta = jnp.exp(log_step.astype(F32))[:, None]
    mag = jnp.exp(delta * a_re)

    def a_pow(n):
        return (mag ** n) * jnp.cos(n * delta * a_im), (mag ** n) * jnp.sin(n * delta * a_im)

    abr, abi = a_pow(1)
    den = a_re * a_re + a_im * a_im
    qr = ((abr - 1.0) * a_re + abi * a_im) / den
    qi = (abi * a_re - (abr - 1.0) * a_im) / den
    bbr = qr[..., None] * b_re - qi[..., None] * b_im
    bbi = qr[..., None] * b_im + qi[..., None] * b_re
    eye = jnp.eye(S5_GROUPS, dtype=F32)
    blk_b = lambda m: jnp.einsum('gph,gk->ghkp', m, eye).reshape(S5_WIDTH, S5_CPLX)
    blk_c = lambda m: jnp.einsum('ghp,gk->gpkh', m, eye).reshape(S5_CPLX, S5_WIDTH)
    wb = jnp.concatenate([blk_b(bbr), blk_b(bbi)], axis=1).astype(BF16)
    wc = jnp.concatenate([blk_c(c_re), -blk_c(c_im)], axis=0).astype(BF16)
    a_row = jnp.concatenate([abr.reshape(1, -1), abi.reshape(1, -1)], axis=1)
    return wb, wc, a_row


def _s5(proj3, tables, d_skip, glu_w, glu_b, out_g):
    wb, wc, a_row = tables
    bsz, seq, _ = proj3.shape
    full = lambda a: pl.BlockSpec(a.shape, lambda i: (0,) * a.ndim)
    vec = pl.BlockSpec((1, S5_WIDTH), lambda i: (0, 0))
    gw = glu_w.astype(BF16)
    return pl.pallas_call(
        _s5_kernel,
        out_shape=jax.ShapeDtypeStruct((bsz, seq, S5_WIDTH), F32),
        grid=(seq // S5_TT,),
        in_specs=[pl.BlockSpec((bsz, S5_TT, S5_WIDTH), lambda i: (0, i, COL_S5 // S5_WIDTH)),
                  full(wb), full(wc), full(a_row), vec, full(gw), vec, vec],
        out_specs=pl.BlockSpec((bsz, S5_TT, S5_WIDTH), lambda i: (0, i, 0)),
        scratch_shapes=[pltpu.VMEM((2 * S5_CPLX // LANES, bsz * S5_PITCH, LANES), F32),
                        pltpu.VMEM((2 * S5_CPLX // LANES, bsz, LANES), F32)],
        compiler_params=_cp("arbitrary"),
        name="s5",
    )(proj3, wb, wc, a_row, d_skip.reshape(1, -1), gw, glu_b.reshape(1, -1), out_g.reshape(1, -1))


def _xattn_kernel(x_ref, kv_ref, wq_ref, wo_ref, g_ref, b_ref, o_ref):
    x = x_ref[...]
    q = jnp.dot(x.astype(BF16), wq_ref[...], preferred_element_type=F32)
    outs = []
    for h in range(MEM_HEADS):
        sl = slice(h * MEM_HEAD_DIM, (h + 1) * MEM_HEAD_DIM)
        kh = kv_ref[:, sl]
        vh = kv_ref[:, D_MODEL + h * MEM_HEAD_DIM:D_MODEL + (h + 1) * MEM_HEAD_DIM]
        s = lax.dot_general(q[:, sl].astype(BF16), kh, (((1,), (1,)), ((), ())),
                            preferred_element_type=F32) * (MEM_HEAD_DIM ** -0.5)
        p = jnp.exp(s - jnp.max(s, axis=-1, keepdims=True))
        p = p / jnp.sum(p, axis=-1, keepdims=True)
        outs.append(jnp.dot(p.astype(BF16), vh, preferred_element_type=F32))
    o = jnp.concatenate(outs, axis=-1)
    h_out = jnp.dot(o.astype(BF16), wo_ref[...], preferred_element_type=F32)
    o_ref[...] = _layer_norm_rows(ALPHA * x + h_out, g_ref[...], b_ref[...])


def _cross_attention(x, kv, wq, wo, g, b, seq, tm):
    t, d = x.shape
    n_mem = kv.shape[0] // (t // seq)
    tiles_per_seq = seq // tm
    w = pl.BlockSpec((d, d), lambda i: (0, 0))
    vec = pl.BlockSpec((1, d), lambda i: (0, 0))
    blk = pl.BlockSpec((tm, d), lambda i: (i, 0))
    return pl.pallas_call(
        _xattn_kernel,
        out_shape=jax.ShapeDtypeStruct((t, d), F32),
        grid=(t // tm,),
        in_specs=[blk, pl.BlockSpec((n_mem, 2 * d), lambda i: (i // tiles_per_seq, 0)), w, w, vec, vec],
        out_specs=blk,
        compiler_params=_cp("parallel"),
        name="cross_attention",
    )(x, kv, wq, wo, g.reshape(1, d), b.reshape(1, d))


FFN_TN = 1408


def _ffn_up_kernel(x_ref, halo_ref, wa_ref, wg_ref, cw_ref, cb_ref, o_ref, *, tiles_per_seq):
    i = pl.program_id(0)
    x = x_ref[...].astype(BF16)
    first = jnp.where(i % tiles_per_seq == 0, 0.0, 1.0)
    xh = (halo_ref[...] * first).astype(BF16)
    a8 = jnp.dot(jnp.concatenate([xh, x], axis=0), wa_ref[...], preferred_element_type=F32)
    a0 = a8[SUBLANES:]
    a1 = pltpu.roll(a8, 1, axis=0)[SUBLANES:]
    a2 = pltpu.roll(a8, 2, axis=0)[SUBLANES:]
    conv = cb_ref[...] + cw_ref[0:1, :] * a2 + cw_ref[1:2, :] * a1 + cw_ref[2:3, :] * a0
    g = jnp.dot(x, wg_ref[...], preferred_element_type=F32)
    o_ref[...] = (jax.nn.silu(conv) * g).astype(o_ref.dtype)


def _ffn_up(x, wa, wg, conv_w, conv_b, seq, tm):
    t, d = x.shape
    hb = tm // SUBLANES
    cw = jnp.zeros((SUBLANES, D_FF), F32).at[0:3].set(conv_w)
    wspec = pl.BlockSpec((d, FFN_TN), lambda i, j: (0, j))
    return pl.pallas_call(
        functools.partial(_ffn_up_kernel, tiles_per_seq=seq // tm),
        out_shape=jax.ShapeDtypeStruct((t, D_FF), BF16),
        grid=(t // tm, D_FF // FFN_TN),
        in_specs=[pl.BlockSpec((tm, d), lambda i, j: (i, 0)),
                  pl.BlockSpec((SUBLANES, d), lambda i, j: (jnp.maximum(i * hb - 1, 0), 0)),
                  wspec, wspec,
                  pl.BlockSpec((SUBLANES, FFN_TN), lambda i, j: (0, j)),
                  pl.BlockSpec((1, FFN_TN), lambda i, j: (0, j))],
        out_specs=pl.BlockSpec((tm, FFN_TN), lambda i, j: (i, j)),
        compiler_params=_cp("parallel", "arbitrary"),
        name="ffn_up",
    )(x, x, wa, wg, cw, conv_b.reshape(1, D_FF))


def kernel(x, mem, positions, w_in, da_lam_q1, da_lam_k1, da_lam_q2, da_lam_k2, da_subln_g, rw_mu, rw_w0, rw_w2, rw_a0, rw_a2, rw_g2, rw_k_k, rw_k_a, rw_r_k, rw_lnx_g, rw_lnx_b, s5_a_re, s5_a_im, s5_b_re, s5_b_im, s5_c_re, s5_c_im, s5_d, s5_log_step, s5_glu_w, s5_glu_b, s5_out_g, w_out, ln1_g, ln1_b, ca_wq, ca_wkv, ca_wo, ln2_g, ln2_b, ffn_w_up, ffn_conv_w, ffn_conv_b, ffn_w_down, ln3_g, ln3_b):
    bsz, seq, d = x.shape
    t = bsz * seq
    tm = min(512, seq)
    n_mem = mem.shape[1]
    xf = x.reshape(t, d)
    memf = mem.reshape(bsz * n_mem, d)
    cos_t, sin_t = _rope_tables(positions, tm)

    for l in range(DEPTH):
        lambda_init = 0.8 - 0.6 * math.exp(-0.3 * l)
        wi = w_in[l]
        w_perm = jnp.concatenate([wi[:, :3 * DA_WIDTH], wi[:, 3 * DA_WIDTH + RW_IN:],
                                  wi[:, 3 * DA_WIDTH:3 * DA_WIDTH + RW_IN]], axis=1).astype(BF16)
        proj = _matmul(xf, w_perm, F32, tm, IN_WIDTH)

        qr, kr, vt = _rope(proj, cos_t, sin_t, seq, tm)
        lam = (jnp.exp(jnp.sum(da_lam_q1[l] * da_lam_k1[l])) - jnp.exp(jnp.sum(da_lam_q2[l] * da_lam_k2[l]))
               + lambda_init)
        kops, v4, gate, bonus = _rw_prep(proj.reshape(bsz, seq, IN_WIDTH), rw_mu[l], rw_w0[l], rw_w2[l],
                                         rw_a0[l], rw_a2[l], rw_g2[l], rw_k_k[l], rw_k_a[l],
                                         rw_r_k[l].reshape(-1))
        h_da = _attention(qr, kr, vt, lam, da_subln_g[l], lambda_init, bsz, seq)
        y4 = _rw_scan(kops, v4, seq)
        h_rw = _rw_post(y4, bonus, gate, rw_lnx_g[l], rw_lnx_b[l]).reshape(t, RW_WIDTH)

        tables = _s5_tables(s5_a_re[l], s5_a_im[l], s5_b_re[l], s5_b_im[l], s5_c_re[l], s5_c_im[l],
                            s5_log_step[l])
        h_s5 = _s5(proj.reshape(bsz, seq, IN_WIDTH), tables, s5_d[l], s5_glu_w[l], s5_glu_b[l],
                   s5_out_g[l]).reshape(t, S5_WIDTH)

        wo = w_out[l].astype(BF16)
        xf = _matmul_ln([h_da, h_rw, h_s5],
                        [wo[:DA_WIDTH], wo[DA_WIDTH:DA_WIDTH + RW_WIDTH], wo[DA_WIDTH + RW_WIDTH:]],
                        xf, ln1_g[l], ln1_b[l], tm)
        kv = _matmul(memf, ca_wkv[l].astype(BF16), BF16, n_mem, 1024)
        xf = _cross_attention(xf, kv, ca_wq[l].astype(BF16), ca_wo[l].astype(BF16), ln2_g[l], ln2_b[l], seq, tm)
        w_up = ffn_w_up[l].astype(BF16)
        hff = _ffn_up(xf, w_up[:, :D_FF], w_up[:, D_FF:], ffn_conv_w[l], ffn_conv_b[l], seq, tm)
        xf = _matmul_ln([hff], [ffn_w_down[l].astype(BF16)], xf, ln3_g[l], ln3_b[l], tm)
    return xf.reshape(bsz, seq, d)
```

```python
import functools
import math

import numpy as np
import jax
import jax.numpy as jnp
from jax import lax
from jax.experimental import pallas as pl
from jax.experimental.pallas import tpu as pltpu

F32 = jnp.float32
BF16 = jnp.bfloat16
HI = lax.Precision.HIGHEST

D_MODEL = 1024
DEPTH = 4
CHUNK = 64
LN_EPS = 1e-5
LOG2E = math.log2(math.e)
RMS_EPS = 1e-6

DA_HEAD_DIM = 64
DA_V_DIM = 128
DA_WIDTH = 512
DA_HEADS = 4
ROPE_THETA = 10000.0

RW_HEAD_DIM = 64
RW_WIDTH = 256
RW_HEADS = 4
RW_DECAY_RANK = 32
RW_AAA_RANK = 32
RW_GATE_RANK = 64
RW_IN = 3 * RW_WIDTH + RW_DECAY_RANK + RW_AAA_RANK + RW_GATE_RANK
RW_DECAY_SCALE = math.exp(-0.5)
RW_LNX_EPS = 64e-5

S5_WIDTH = 256
S5_GROUP_CH = 16
S5_GROUPS = 16
S5_STATE = 64
S5_CPLX = S5_GROUPS * S5_STATE

IN_WIDTH = 3 * DA_WIDTH + RW_IN + S5_WIDTH
MEM_HEADS = 4
MEM_HEAD_DIM = 256
D_FF = 2816
ALPHA = (2.0 * DEPTH) ** 0.25

COL_Q, COL_K, COL_V, COL_S5, COL_RW = 0, 512, 1024, 1536, 1792

VMEM_LIMIT = 48 * 1024 * 1024
LANES = 128
SUBLANES = 8


def _cp(*sem):
    return pltpu.CompilerParams(dimension_semantics=sem, vmem_limit_bytes=VMEM_LIMIT)


def _layer_norm_rows(v, g, b):
    mu = jnp.mean(v, axis=-1, keepdims=True)
    c = v - mu
    var = jnp.mean(c * c, axis=-1, keepdims=True)
    return c * lax.rsqrt(var + LN_EPS) * g + b


def _split3(x):
    p1 = x.astype(BF16)
    r1 = x - p1.astype(F32)
    p2 = r1.astype(BF16)
    p3 = (r1 - p2.astype(F32)).astype(BF16)
    return p1, p2, p3


def _segment_sum(x, seg):
    return sum(jnp.dot(p, seg, preferred_element_type=F32) for p in _split3(x))


def _dot_split(a, w_ref):
    a1, a2, _ = _split3(a)
    d = functools.partial(jnp.dot, preferred_element_type=F32)
    return d(a1, w_ref[0]) + (d(a1, w_ref[1]) + d(a2, w_ref[0]))


def _hi_lo(w):
    hi = w.astype(BF16)
    return jnp.stack([hi, (w - hi.astype(F32)).astype(BF16)])


def _mm_kernel(a_ref, w_ref, o_ref):
    o_ref[...] = jnp.dot(a_ref[...].astype(BF16), w_ref[...],
                         preferred_element_type=F32).astype(o_ref.dtype)


def _matmul(a, w, out_dtype, tm, tn):
    m, k = a.shape
    n = w.shape[1]
    return pl.pallas_call(
        _mm_kernel,
        out_shape=jax.ShapeDtypeStruct((m, n), out_dtype),
        grid=(m // tm, n // tn),
        in_specs=[pl.BlockSpec((tm, k), lambda i, j: (i, 0)),
                  pl.BlockSpec((k, tn), lambda i, j: (0, j))],
        out_specs=pl.BlockSpec((tm, tn), lambda i, j: (i, j)),
        compiler_params=_cp("parallel", "arbitrary"),
        name="matmul",
    )(a, w)


def _mm_ln_kernel(n_in, *refs):
    a_refs = refs[:n_in]
    w_refs = refs[n_in:2 * n_in]
    x_ref, g_ref, b_ref, o_ref = refs[2 * n_in:]
    acc = ALPHA * x_ref[...]
    for a_ref, w_ref in zip(a_refs, w_refs):
        acc = acc + jnp.dot(a_ref[...].astype(BF16), w_ref[...], preferred_element_type=F32)
    o_ref[...] = _layer_norm_rows(acc, g_ref[...], b_ref[...])


def _matmul_ln(a_list, w_list, x, g, b, tm):
    m, d = x.shape
    n_in = len(a_list)
    in_specs = [pl.BlockSpec((tm, a.shape[1]), lambda i: (i, 0)) for a in a_list]
    in_specs += [pl.BlockSpec(w.shape, lambda i: (0, 0)) for w in w_list]
    in_specs += [pl.BlockSpec((tm, d), lambda i: (i, 0)),
                 pl.BlockSpec((1, d), lambda i: (0, 0)),
                 pl.BlockSpec((1, d), lambda i: (0, 0))]
    return pl.pallas_call(
        functools.partial(_mm_ln_kernel, n_in),
        out_shape=jax.ShapeDtypeStruct((m, d), F32),
        grid=(m // tm,),
        in_specs=in_specs,
        out_specs=pl.BlockSpec((tm, d), lambda i: (i, 0)),
        compiler_params=_cp("parallel"),
        name="matmul_ln",
    )(*a_list, *w_list, x, g.reshape(1, d), b.reshape(1, d))


def _rope_table_kernel(pos_ref, freq_ref, sign_ref, cos_ref, sin_ref):
    ang = pos_ref[...] * freq_ref[...]
    cos_ref[...] = jnp.cos(ang)
    sin_ref[...] = jnp.sin(ang) * sign_ref[...]


def _rope_tables(positions, tm):
    t = positions.size
    inv_freq = ROPE_THETA ** (-jnp.arange(0, DA_HEAD_DIM, 2, dtype=F32) / DA_HEAD_DIM)
    freq_row = jnp.tile(inv_freq, 4).reshape(1, LANES)
    sign_row = jnp.tile(jnp.concatenate([-jnp.ones((32,), F32), jnp.ones((32,), F32)]), 2).reshape(1, LANES)
    pos_col = positions.astype(F32).reshape(t, 1)
    row = pl.BlockSpec((1, LANES), lambda i: (0, 0))
    return pl.pallas_call(
        _rope_table_kernel,
        out_shape=(jax.ShapeDtypeStruct((t, LANES), F32),) * 2,
        grid=(t // tm,),
        in_specs=[pl.BlockSpec((tm, 1), lambda i: (i, 0)), row, row],
        out_specs=(pl.BlockSpec((tm, LANES), lambda i: (i, 0)),) * 2,
        compiler_params=_cp("parallel"),
        name="rope_tables",
    )(pos_col, freq_row, sign_row)


def _in_proj_kernel(x_ref, w_ref, cos_ref, sin_ref, qo_ref, ko_ref, vt_ref, rw_ref, s5_ref):
    p = jnp.dot(x_ref[...].astype(BF16), w_ref[...], preferred_element_type=F32)
    cos = cos_ref[...]
    sin = sin_ref[...]
    lane = lax.broadcasted_iota(jnp.int32, cos.shape, 1)
    low = (lane % DA_HEAD_DIM) < (DA_HEAD_DIM // 2)

    def rope(t):
        swapped = jnp.where(low, pltpu.roll(t, LANES - 32, axis=1), pltpu.roll(t, 32, axis=1))
        return t * cos + swapped * sin

    scale = DA_HEAD_DIM ** -0.5 * LOG2E
    for h in range(DA_HEADS):
        sl = slice(h * LANES, (h + 1) * LANES)
        qo_ref[:, sl] = (rope(p[:, COL_Q + h * LANES:COL_Q + (h + 1) * LANES]) * scale).astype(BF16)
        ko_ref[:, sl] = rope(p[:, COL_K + h * LANES:COL_K + (h + 1) * LANES]).astype(BF16)
    vt_ref[...] = p[:, COL_V:COL_V + DA_WIDTH].T.astype(BF16)
    s5_ref[...] = p[:, COL_S5:COL_S5 + S5_WIDTH]
    rw_ref[...] = p[:, COL_RW:COL_RW + RW_IN]


def _in_proj(x, w, cos_t, sin_t, seq, tm):
    t, d = x.shape
    nt = seq // tm
    tab = pl.BlockSpec((tm, LANES), lambda i: (i, 0))
    row = lambda n: pl.BlockSpec((tm, n), lambda i: (i, 0))
    return pl.pallas_call(
        _in_proj_kernel,
        out_shape=(jax.ShapeDtypeStruct((t, DA_WIDTH), BF16), jax.ShapeDtypeStruct((t, DA_WIDTH), BF16),
                   jax.ShapeDtypeStruct((t // seq * DA_WIDTH, seq), BF16),
                   jax.ShapeDtypeStruct((t, RW_IN), F32), jax.ShapeDtypeStruct((t, S5_WIDTH), F32)),
        grid=(t // tm,),
        in_specs=[row(d), pl.BlockSpec(w.shape, lambda i: (0, 0)), tab, tab],
        out_specs=(row(DA_WIDTH), row(DA_WIDTH), pl.BlockSpec((DA_WIDTH, tm), lambda i: (i // nt, i % nt)),
                   row(RW_IN), row(S5_WIDTH)),
        compiler_params=_cp("parallel"),
        name="in_proj",
    )(x, w, cos_t, sin_t)


ATT_TQ = 1024
ATT_TK = 512
ATT_CW = 256
ATT_KB = ATT_TQ // ATT_TK


def _attn_kernel(lam_ref, g_ref, q_ref, k_ref, vt_ref, o_ref, qs_ref, p_ref, acc_ref, *, lambda_init):
    i = pl.program_id(2)
    nq = 2 * ATT_TQ
    q = q_ref[...]
    lane = lax.broadcasted_iota(jnp.int32, q.shape, 1)
    zero = jnp.zeros_like(q)
    qs_ref[:ATT_TQ, :] = jnp.where(lane < DA_HEAD_DIM, q, zero)
    qs_ref[ATT_TQ:, :] = jnp.where(lane >= DA_HEAD_DIM, q, zero)
    p_ref[...] = jnp.zeros_like(p_ref)
    acc_ref[...] = jnp.zeros_like(acc_ref)

    def pv_update(c, vt, alpha_prev):
        cs = slice(c * ATT_CW, (c + 1) * ATT_CW)
        pv = jnp.dot(vt, p_ref[:, cs], preferred_element_type=F32)
        acc_ref[:, cs] = alpha_prev[:, cs] * acc_ref[:, cs] + pv

    def block(j, carry, diag):
        m, l, alpha_prev = carry
        kb = k_ref[pl.ds(pl.multiple_of(j * ATT_TK, ATT_TK), ATT_TK), :]
        vt = vt_ref[:, pl.ds(pl.multiple_of(jnp.maximum(j - 1, 0) * ATT_TK, ATT_TK), ATT_TK)]
        ms, ls, alphas = [], [], []
        for c in range(nq // ATT_CW):
            cs = slice(c * ATT_CW, (c + 1) * ATT_CW)
            s = lax.dot_general(kb, qs_ref[cs, :], (((1,), (1,)), ((), ())), preferred_element_type=F32)
            if diag is not None:
                krow = lax.broadcasted_iota(jnp.int32, s.shape, 0) + diag * ATT_TK
                qcol = (lax.broadcasted_iota(jnp.int32, s.shape, 1) + c * ATT_CW) % ATT_TQ
                s = jnp.where(krow // CHUNK <= qcol // CHUNK, s, -jnp.inf)
            pv_update(c, vt, alpha_prev)
            m_new = jnp.maximum(m[:, cs], jnp.max(s, axis=0, keepdims=True))
            alpha = jnp.exp2(m[:, cs] - m_new)
            p = jnp.exp2(s - m_new)
            ls.append(alpha * l[:, cs] + jnp.sum(p, axis=0, keepdims=True))
            p_ref[:, cs] = p.astype(BF16)
            ms.append(m_new)
            alphas.append(alpha)
        return jnp.concatenate(ms, axis=1), jnp.concatenate(ls, axis=1), jnp.concatenate(alphas, axis=1)

    def trip(t, carry, masked):
        for d in range(ATT_KB):
            carry = block(ATT_KB * t + d, carry, d if masked else None)
        return carry

    init = (jnp.full((1, nq), -jnp.inf, F32), jnp.zeros((1, nq), F32), jnp.ones((1, nq), F32))
    carry = lax.fori_loop(0, i, lambda t, c: trip(t, c, False), init)
    m, l, alpha = trip(i, carry, True)
    last = ATT_KB * i + ATT_KB - 1
    vt = vt_ref[:, pl.ds(pl.multiple_of(last * ATT_TK, ATT_TK), ATT_TK)]
    for c in range(nq // ATT_CW):
        pv_update(c, vt, alpha)
    o = acc_ref[...] / l
    o = o[:, :ATT_TQ] - lam_ref[...] * o[:, ATT_TQ:]
    o = o * lax.rsqrt(jnp.mean(o * o, axis=0, keepdims=True) + RMS_EPS) * g_ref[...]
    o_ref[...] = (o * (1.0 - lambda_init)).T


def _attention(qr, kr, vt, lam, subln_g, lambda_init, bsz, seq):
    nq = seq // ATT_TQ
    col = pl.BlockSpec((LANES, 1), lambda b, h, i: (0, 0))
    qo = pl.BlockSpec((ATT_TQ, LANES), lambda b, h, i: (b * nq + i, h))
    return pl.pallas_call(
        functools.partial(_attn_kernel, lambda_init=lambda_init),
        out_shape=jax.ShapeDtypeStruct((bsz * seq, DA_WIDTH), F32),
        grid=(bsz, DA_HEADS, nq),
        in_specs=[pl.BlockSpec((1, 1), lambda b, h, i: (0, 0)), col, qo,
                  pl.BlockSpec((seq, LANES), lambda b, h, i: (b, h)),
                  pl.BlockSpec((DA_V_DIM, seq), lambda b, h, i: (b * DA_HEADS + h, 0))],
        out_specs=qo,
        scratch_shapes=[pltpu.VMEM((2 * ATT_TQ, LANES), BF16),
                        pltpu.VMEM((ATT_TK, 2 * ATT_TQ), BF16),
                        pltpu.VMEM((DA_V_DIM, 2 * ATT_TQ), F32)],
        compiler_params=_cp("parallel", "parallel", "arbitrary"),
        name="diff_attention",
    )(lam.reshape(1, 1), subln_g.reshape(LANES, 1), qr, kr, vt)


RW_VLO = 4
RW_VHI = RW_HEAD_DIM // RW_VLO
RW_BH = LANES // RW_VLO
RW_TB = 128
RW_PITCH = RW_HEAD_DIM + SUBLANES
RW_TT = 64
RW_NACC = 4
N_KOPS = 5


def _rw_prep_kernel(p_ref, halo_ref, mu_ref, w0_ref, a0_ref, kk_ref, ka_ref, rk_ref,
                    w2_ref, a2_ref, g2_ref, seg_ref,
                    kop_ref, v_ref, g_ref, bonus_ref, tr_ref):
    i = pl.program_id(0)
    n_op = pl.program_id(1)
    bsz = p_ref.shape[0]
    first = jnp.where(i == 0, 0.0, 1.0)
    seg = seg_ref[...]

    def per_batch(b, _):
        p = p_ref[b]
        row = lax.broadcasted_iota(jnp.int32, p.shape, 0)
        last_prev = jnp.broadcast_to(halo_ref[b, SUBLANES - 1:SUBLANES, :], p.shape) * first
        prev = jnp.where(row == 0, last_prev, pltpu.roll(p, 1, axis=0))
        p = p + (prev - p) * mu_ref[...]
        r = p[:, 0:RW_WIDTH]
        k = p[:, RW_WIDTH:2 * RW_WIDTH]
        v = p[:, 2 * RW_WIDTH:3 * RW_WIDTH]
        low = p[:, 3 * RW_WIDTH:RW_IN]
        decay = jnp.exp(-RW_DECAY_SCALE * jax.nn.sigmoid(w0_ref[...] + _dot_split(jnp.tanh(low), w2_ref)))
        a = jax.nn.sigmoid(a0_ref[...] + _dot_split(low, a2_ref))
        g_ref[b] = _dot_split(jax.nn.sigmoid(low), g2_ref)
        kk = k * kk_ref[...]
        kk = kk * lax.rsqrt(jnp.maximum(_segment_sum(kk * kk, seg), 1e-24))
        k = k * (1.0 + (a - 1.0) * ka_ref[...])
        bonus_ref[b] = _segment_sum(r * k * rk_ref[...], seg) * v
        for n, val in enumerate((r, decay, k, -kk, kk * a, v)):
            vt = val.T
            for h in range(RW_HEADS):
                base = pl.multiple_of((b * RW_HEADS + h) * RW_PITCH, SUBLANES)
                tr_ref[n, pl.ds(base, RW_HEAD_DIM), :] = vt[h * RW_HEAD_DIM:(h + 1) * RW_HEAD_DIM, :]
        return 0

    def v_relayout():
        for vh in range(RW_VHI):
            rows = [tr_ref[N_KOPS, pl.ds(vh * RW_VLO + vl, RW_BH, stride=RW_PITCH), :] for vl in range(RW_VLO)]
            v_ref[pl.ds(vh, RW_TB, stride=RW_VHI), :] = jnp.concatenate(rows, axis=0).T

    @pl.when(n_op == 0)
    def _():
        lax.fori_loop(0, bsz, per_batch, 0)
        v_relayout()

    def per_k_group(kg, _):
        for ks in range(SUBLANES):
            kk_i = kg * SUBLANES + ks
            rows = tr_ref[n_op, pl.ds(kk_i, RW_BH, stride=RW_PITCH), :]
            tile = jnp.concatenate([rows] * RW_VLO, axis=0).T
            kop_ref[0, :, kk_i] = tile.reshape(RW_TB // SUBLANES, SUBLANES, LANES)
        return 0
    lax.fori_loop(0, RW_HEAD_DIM // SUBLANES, per_k_group, 0)


def _head_segments():
    idx = np.arange(RW_WIDTH) // RW_HEAD_DIM
    return jnp.asarray((idx[:, None] == idx[None, :]).astype(np.float32)).astype(BF16)


def _rw_prep(proj3, mu, w0, w2, a0, a2, g2, k_k, k_a, r_k):
    bsz, seq, _ = proj3.shape
    hb = RW_TB // SUBLANES
    w2p = _hi_lo(jnp.zeros((LANES, RW_WIDTH), F32).at[0:32].set(w2))
    a2p = _hi_lo(jnp.zeros((LANES, RW_WIDTH), F32).at[32:64].set(a2))
    g2p = _hi_lo(jnp.zeros((LANES, RW_WIDTH), F32).at[64:128].set(g2))
    lowrank = pl.BlockSpec((2, LANES, RW_WIDTH), lambda i, j: (0, 0, 0))
    vec = lambda n: pl.BlockSpec((1, n), lambda i, j: (0, 0))
    mat = lambda r, c: pl.BlockSpec((r, c), lambda i, j: (0, 0))
    tok = pl.BlockSpec((bsz, RW_TB, RW_WIDTH), lambda i, j: (0, i, 0))
    tshape = jax.ShapeDtypeStruct((bsz, seq, RW_WIDTH), F32)
    return pl.pallas_call(
        _rw_prep_kernel,
        out_shape=(jax.ShapeDtypeStruct((N_KOPS, seq // SUBLANES, RW_HEAD_DIM, SUBLANES, LANES), F32),
                   jax.ShapeDtypeStruct((seq * RW_VHI, LANES), F32), tshape, tshape),
        grid=(seq // RW_TB, N_KOPS),
        in_specs=[pl.BlockSpec((bsz, RW_TB, RW_IN), lambda i, j: (0, i, 0)),
                  pl.BlockSpec((bsz, SUBLANES, RW_IN),
                               lambda i, j: (0, jnp.maximum(i * hb - 1, 0), 0)),
                  vec(RW_IN), vec(RW_WIDTH), vec(RW_WIDTH), vec(RW_WIDTH), vec(RW_WIDTH), vec(RW_WIDTH),
                  lowrank, lowrank, lowrank, mat(RW_WIDTH, RW_WIDTH)],
        out_specs=(pl.BlockSpec((1, RW_TB // SUBLANES, RW_HEAD_DIM, SUBLANES, LANES),
                                lambda i, j: (j, i, 0, 0, 0)),
                   pl.BlockSpec((RW_TB * RW_VHI, LANES), lambda i, j: (i, 0)), tok, tok),
        scratch_shapes=[pltpu.VMEM((N_KOPS + 1, RW_BH * RW_PITCH, RW_TB), F32)],
        compiler_params=_cp("parallel", "arbitrary"),
        name="rwkv_prep",
    )(proj3, proj3, mu.reshape(1, RW_IN), w0.reshape(1, -1), a0.reshape(1, -1), k_k.reshape(1, -1),
      k_a.reshape(1, -1), r_k.reshape(1, -1), w2p, a2p, g2p, _head_segments())


def _rw_scan_kernel(r_ref, w_ref, k_ref, an_ref, b_ref, v_ref, y_ref, s_ref):
    @pl.when(pl.program_id(0) == 0)
    def _():
        s_ref[...] = jnp.zeros_like(s_ref)

    nvb = RW_VHI // SUBLANES
    vsl = [slice(vb * SUBLANES, (vb + 1) * SUBLANES) for vb in range(nvb)]
    zeros = lambda: [[jnp.zeros((SUBLANES, LANES), F32) for _ in range(RW_NACC)] for _ in range(nvb)]
    total = lambda acc: [sum(acc[vb][1:], acc[vb][0]) for vb in range(nvb)]

    def body(tb, _):
        def row(ref, kk, ts):
            return jnp.broadcast_to(ref[tb, kk, ts:ts + 1, :], (SUBLANES, LANES))

        zacc = zeros()
        for kk in range(RW_HEAD_DIM):
            an = row(an_ref, kk, 0)
            for vb in range(nvb):
                zacc[vb][kk % RW_NACC] += s_ref[kk, vsl[vb], :] * an
        z = total(zacc)
        for ts in range(SUBLANES):
            t = tb * SUBLANES + ts
            vt = [v_ref[t, vsl[vb], :] for vb in range(nvb)]
            yacc, zacc = zeros(), zeros()
            for kk in range(RW_HEAD_DIM):
                wr, br, kr, rr = row(w_ref, kk, ts), row(b_ref, kk, ts), row(k_ref, kk, ts), row(r_ref, kk, ts)
                an = row(an_ref, kk, ts + 1) if ts + 1 < SUBLANES else None
                for vb in range(nvb):
                    s_new = s_ref[kk, vsl[vb], :] * wr + z[vb] * br + vt[vb] * kr
                    s_ref[kk, vsl[vb], :] = s_new
                    yacc[vb][kk % RW_NACC] += s_new * rr
                    if an is not None:
                        zacc[vb][kk % RW_NACC] += s_new * an
            for vb, y in enumerate(total(yacc)):
                y_ref[t, vsl[vb], :] = y
            z = total(zacc)
        return 0

    lax.fori_loop(0, RW_TT // SUBLANES, body, 0)


def _rw_scan(kops, v4, seq):
    kspec = lambda n: pl.BlockSpec((None, RW_TT // SUBLANES, RW_HEAD_DIM, SUBLANES, LANES),
                                   lambda i: (n, i, 0, 0, 0))
    vspec = pl.BlockSpec((RW_TT, RW_VHI, LANES), lambda i: (i, 0, 0))
    return pl.pallas_call(
        _rw_scan_kernel,
        out_shape=jax.ShapeDtypeStruct((seq, RW_VHI, LANES), F32),
        grid=(seq // RW_TT,),
        in_specs=[kspec(n) for n in range(N_KOPS)] + [vspec],
        out_specs=vspec,
        scratch_shapes=[pltpu.VMEM((RW_HEAD_DIM, RW_VHI, LANES), F32)],
        compiler_params=_cp("arbitrary"),
        name="rwkv_scan",
    )(*([kops] * N_KOPS), v4.reshape(seq, RW_VHI, LANES))


def _rw_post_kernel(y_ref, bonus_ref, gate_ref, lg_ref, lb_ref, seg_ref, o_ref, tr_ref):
    bsz = bonus_ref.shape[0]

    for vh in range(RW_VHI):
        yt = y_ref[pl.ds(vh, RW_TB, stride=RW_VHI), :].T
        for vl in range(RW_VLO):
            tr_ref[pl.ds(vh * RW_VLO + vl, RW_BH, stride=RW_PITCH), :] = yt[vl * RW_BH:(vl + 1) * RW_BH, :]

    seg = seg_ref[...]
    head_mean = lambda a: _segment_sum(a, seg) * (1.0 / RW_HEAD_DIM)

    def per_batch(b, _):
        slabs = []
        for h in range(RW_HEADS):
            base = pl.multiple_of((b * RW_HEADS + h) * RW_PITCH, SUBLANES)
            slabs.append(tr_ref[pl.ds(base, RW_HEAD_DIM), :])
        y = jnp.concatenate(slabs, axis=0).T
        c = y - head_mean(y)
        var = head_mean(c * c)
        y = c * lax.rsqrt(var + RW_LNX_EPS) * lg_ref[...] + lb_ref[...]
        o_ref[b] = (y + bonus_ref[b]) * gate_ref[b]
        return 0
    lax.fori_loop(0, bsz, per_batch, 0)


def _rw_post(y4, bonus, gate, lnx_g, lnx_b):
    bsz, seq, _ = bonus.shape
    tok = pl.BlockSpec((bsz, RW_TB, RW_WIDTH), lambda i: (0, i, 0))
    vec = pl.BlockSpec((1, RW_WIDTH), lambda i: (0, 0))
    return pl.pallas_call(
        _rw_post_kernel,
        out_shape=jax.ShapeDtypeStruct((bsz, seq, RW_WIDTH), F32),
        grid=(seq // RW_TB,),
        in_specs=[pl.BlockSpec((RW_TB * RW_VHI, LANES), lambda i: (i, 0)), tok, tok, vec, vec,
                  pl.BlockSpec((RW_WIDTH, RW_WIDTH), lambda i: (0, 0))],
        out_specs=tok,
        scratch_shapes=[pltpu.VMEM((RW_BH * RW_PITCH, RW_TB), F32)],
        compiler_params=_cp("parallel"),
        name="rwkv_post",
    )(y4.reshape(seq * RW_VHI, LANES), bonus, gate, lnx_g.reshape(1, -1), lnx_b.reshape(1, -1), _head_segments())


S5_TT = 128
S5_PITCH = S5_TT + SUBLANES


def _s5_kernel(u_ref, wb_ref, wc_ref, a_ref, d_ref, gw_ref, gb_ref, og_ref, o_ref, x_ref, carry_ref):
    @pl.when(pl.program_id(0) == 0)
    def _():
        carry_ref[...] = jnp.zeros_like(carry_ref)

    n = S5_CPLX
    nc = n // LANES
    bsz = u_ref.shape[0]

    def project(b, _):
        base = b * S5_PITCH
        bu = jnp.dot(u_ref[b].astype(BF16), wb_ref[...], preferred_element_type=F32)
        for c in range(2 * nc):
            x_ref[c, pl.ds(base, S5_TT), :] = bu[:, c * LANES:(c + 1) * LANES]
        return 0
    for b in range(bsz):
        project(b, 0)

    ar = [jnp.broadcast_to(a_ref[:, c * LANES:(c + 1) * LANES], (bsz, LANES)) for c in range(nc)]
    ai = [jnp.broadcast_to(a_ref[:, n + c * LANES:n + (c + 1) * LANES], (bsz, LANES)) for c in range(nc)]

    def steps(tb, carry):
        cr, ci = list(carry[0]), list(carry[1])
        for ts in range(SUBLANES):
            rows = pl.ds(tb * SUBLANES + ts, bsz, stride=S5_PITCH)
            for c in range(nc):
                cr[c], ci[c] = (ar[c] * cr[c] - ai[c] * ci[c] + x_ref[c, rows, :],
                                ar[c] * ci[c] + ai[c] * cr[c] + x_ref[nc + c, rows, :])
                x_ref[c, rows, :] = cr[c]
                x_ref[nc + c, rows, :] = ci[c]
        return tuple(cr), tuple(ci)

    init = (tuple(carry_ref[c] for c in range(nc)), tuple(carry_ref[nc + c] for c in range(nc)))
    cr, ci = lax.fori_loop(0, S5_TT // SUBLANES, steps, init)
    for c in range(nc):
        carry_ref[c] = cr[c]
        carry_ref[nc + c] = ci[c]

    def readout(b, _):
        base = b * S5_PITCH
        u = u_ref[b]
        x = jnp.concatenate([x_ref[c, pl.ds(base, S5_TT), :] for c in range(2 * nc)], axis=1)
        y = jnp.dot(x.astype(BF16), wc_ref[...], preferred_element_type=F32) + d_ref[...] * u
        z = jax.nn.gelu(y)
        z = z * jax.nn.sigmoid(jnp.dot(z.astype(BF16), gw_ref[...], preferred_element_type=F32) + gb_ref[...])
        o_ref[b] = z * lax.rsqrt(jnp.mean(z * z, axis=-1, keepdims=True) + RMS_EPS) * og_ref[...]
        return 0
    for b in range(bsz):
        readout(b, 0)


def _s5_tables(a_re, a_im, b_re, b_im, c_re, c_im, log_step):
    delta = jnp.exp(log_step.astype(F32))[:, None]
    mag = jnp.exp(delta * a_re)

    def a_pow(n):
        return (mag ** n) * jnp.cos(n * delta * a_im), (mag ** n) * jnp.sin(n * delta * a_im)

    abr, abi = a_pow(1)
    den = a_re * a_re + a_im * a_im
    qr = ((abr - 1.0) * a_re + abi * a_im) / den
    qi = (abi * a_re - (abr - 1.0) * a_im) / den
    bbr = qr[..., None] * b_re - qi[..., None] * b_im
    bbi = qr[..., None] * b_im + qi[..., None] * b_re
    eye = jnp.eye(S5_GROUPS, dtype=F32)
    blk_b = lambda m: jnp.einsum('gph,gk->ghkp', m, eye).reshape(S5_WIDTH, S5_CPLX)
    blk_c = lambda m: jnp.einsum('ghp,gk->gpkh', m, eye).reshape(S5_CPLX, S5_WIDTH)
    wb = jnp.concatenate([blk_b(bbr), blk_b(bbi)], axis=1).astype(BF16)
    wc = jnp.concatenate([blk_c(c_re), -blk_c(c_im)], axis=0).astype(BF16)
    a_row = jnp.concatenate([abr.reshape(1, -1), abi.reshape(1, -1)], axis=1)
    return wb, wc, a_row


def _s5(proj3, tables, d_skip, glu_w, glu_b, out_g):
    wb, wc, a_row = tables
    bsz, seq, _ = proj3.shape
    full = lambda a: pl.BlockSpec(a.shape, lambda i: (0,) * a.ndim)
    vec = pl.BlockSpec((1, S5_WIDTH), lambda i: (0, 0))
    gw = glu_w.astype(BF16)
    return pl.pallas_call(
        _s5_kernel,
        out_shape=jax.ShapeDtypeStruct((bsz, seq, S5_WIDTH), F32),
        grid=(seq // S5_TT,),
        in_specs=[pl.BlockSpec((bsz, S5_TT, S5_WIDTH), lambda i: (0, i, 0)),
                  full(wb), full(wc), full(a_row), vec, full(gw), vec, vec],
        out_specs=pl.BlockSpec((bsz, S5_TT, S5_WIDTH), lambda i: (0, i, 0)),
        scratch_shapes=[pltpu.VMEM((2 * S5_CPLX // LANES, bsz * S5_PITCH, LANES), F32),
                        pltpu.VMEM((2 * S5_CPLX // LANES, bsz, LANES), F32)],
        compiler_params=_cp("arbitrary"),
        name="s5",
    )(proj3, wb, wc, a_row, d_skip.reshape(1, -1), gw, glu_b.reshape(1, -1), out_g.reshape(1, -1))


def _xattn_kernel(x_ref, kv_ref, wq_ref, wo_ref, g_ref, b_ref, o_ref):
    x = x_ref[...]
    q = jnp.dot(x.astype(BF16), wq_ref[...], preferred_element_type=F32)
    outs = []
    for h in range(MEM_HEADS):
        sl = slice(h * MEM_HEAD_DIM, (h + 1) * MEM_HEAD_DIM)
        kh = kv_ref[:, sl]
        vh = kv_ref[:, D_MODEL + h * MEM_HEAD_DIM:D_MODEL + (h + 1) * MEM_HEAD_DIM]
        s = lax.dot_general(q[:, sl].astype(BF16), kh, (((1,), (1,)), ((), ())),
                            preferred_element_type=F32) * (MEM_HEAD_DIM ** -0.5)
        p = jnp.exp(s - jnp.max(s, axis=-1, keepdims=True))
        p = p / jnp.sum(p, axis=-1, keepdims=True)
        outs.append(jnp.dot(p.astype(BF16), vh, preferred_element_type=F32))
    o = jnp.concatenate(outs, axis=-1)
    h_out = jnp.dot(o.astype(BF16), wo_ref[...], preferred_element_type=F32)
    o_ref[...] = _layer_norm_rows(ALPHA * x + h_out, g_ref[...], b_ref[...])


def _cross_attention(x, kv, wq, wo, g, b, seq, tm):
    t, d = x.shape
    n_mem = kv.shape[0] // (t // seq)
    tiles_per_seq = seq // tm
    w = pl.BlockSpec((d, d), lambda i: (0, 0))
    vec = pl.BlockSpec((1, d), lambda i: (0, 0))
    blk = pl.BlockSpec((tm, d), lambda i: (i, 0))
    return pl.pallas_call(
        _xattn_kernel,
        out_shape=jax.ShapeDtypeStruct((t, d), F32),
        grid=(t // tm,),
        in_specs=[blk, pl.BlockSpec((n_mem, 2 * d), lambda i: (i // tiles_per_seq, 0)), w, w, vec, vec],
        out_specs=blk,
        compiler_params=_cp("parallel"),
        name="cross_attention",
    )(x, kv, wq, wo, g.reshape(1, d), b.reshape(1, d))


FFN_TN = 1408


def _ffn_up_kernel(x_ref, halo_ref, wa_ref, wg_ref, cw_ref, cb_ref, o_ref, *, tiles_per_seq):
    i = pl.program_id(0)
    x = x_ref[...].astype(BF16)
    first = jnp.where(i % tiles_per_seq == 0, 0.0, 1.0)
    xh = (halo_ref[...] * first).astype(BF16)
    a8 = jnp.dot(jnp.concatenate([xh, x], axis=0), wa_ref[...], preferred_element_type=F32)
    a0 = a8[SUBLANES:]
    a1 = pltpu.roll(a8, 1, axis=0)[SUBLANES:]
    a2 = pltpu.roll(a8, 2, axis=0)[SUBLANES:]
    conv = cb_ref[...] + cw_ref[0:1, :] * a2 + cw_ref[1:2, :] * a1 + cw_ref[2:3, :] * a0
    g = jnp.dot(x, wg_ref[...], preferred_element_type=F32)
    o_ref[...] = (jax.nn.silu(conv) * g).astype(o_ref.dtype)


def _ffn_up(x, wa, wg, conv_w, conv_b, seq, tm):
    t, d = x.shape
    hb = tm // SUBLANES
    cw = jnp.zeros((SUBLANES, D_FF), F32).at[0:3].set(conv_w)
    wspec = pl.BlockSpec((d, FFN_TN), lambda i, j: (0, j))
    return pl.pallas_call(
        functools.partial(_ffn_up_kernel, tiles_per_seq=seq // tm),
        out_shape=jax.ShapeDtypeStruct((t, D_FF), BF16),
        grid=(t // tm, D_FF // FFN_TN),
        in_specs=[pl.BlockSpec((tm, d), lambda i, j: (i, 0)),
                  pl.BlockSpec((SUBLANES, d), lambda i, j: (jnp.maximum(i * hb - 1, 0), 0)),
                  wspec, wspec,
                  pl.BlockSpec((SUBLANES, FFN_TN), lambda i, j: (0, j)),
                  pl.BlockSpec((1, FFN_TN), lambda i, j: (0, j))],
        out_specs=pl.BlockSpec((tm, FFN_TN), lambda i, j: (i, j)),
        compiler_params=_cp("parallel", "arbitrary"),
        name="ffn_up",
    )(x, x, wa, wg, cw, conv_b.reshape(1, D_FF))


def kernel(x, mem, positions, w_in, da_lam_q1, da_lam_k1, da_lam_q2, da_lam_k2, da_subln_g, rw_mu, rw_w0, rw_w2, rw_a0, rw_a2, rw_g2, rw_k_k, rw_k_a, rw_r_k, rw_lnx_g, rw_lnx_b, s5_a_re, s5_a_im, s5_b_re, s5_b_im, s5_c_re, s5_c_im, s5_d, s5_log_step, s5_glu_w, s5_glu_b, s5_out_g, w_out, ln1_g, ln1_b, ca_wq, ca_wkv, ca_wo, ln2_g, ln2_b, ffn_w_up, ffn_conv_w, ffn_conv_b, ffn_w_down, ln3_g, ln3_b):
    bsz, seq, d = x.shape
    t = bsz * seq
    tm = min(512, seq)
    n_mem = mem.shape[1]
    xf = x.reshape(t, d)
    memf = mem.reshape(bsz * n_mem, d)
    cos_t, sin_t = _rope_tables(positions, tm)

    for l in range(DEPTH):
        lambda_init = 0.8 - 0.6 * math.exp(-0.3 * l)
        wi = w_in[l]
        w_perm = jnp.concatenate([wi[:, :3 * DA_WIDTH], wi[:, 3 * DA_WIDTH + RW_IN:],
                                  wi[:, 3 * DA_WIDTH:3 * DA_WIDTH + RW_IN]], axis=1).astype(BF16)
        qr, kr, vt, p_rw, p_s5 = _in_proj(xf, w_perm, cos_t, sin_t, seq, tm)
        lam = (jnp.exp(jnp.sum(da_lam_q1[l] * da_lam_k1[l])) - jnp.exp(jnp.sum(da_lam_q2[l] * da_lam_k2[l]))
               + lambda_init)
        kops, v4, gate, bonus = _rw_prep(p_rw.reshape(bsz, seq, RW_IN), rw_mu[l], rw_w0[l], rw_w2[l],
                                         rw_a0[l], rw_a2[l], rw_g2[l], rw_k_k[l], rw_k_a[l],
                                         rw_r_k[l].reshape(-1))
        h_da = _attention(qr, kr, vt, lam, da_subln_g[l], lambda_init, bsz, seq)
        y4 = _rw_scan(kops, v4, seq)
        h_rw = _rw_post(y4, bonus, gate, rw_lnx_g[l], rw_lnx_b[l]).reshape(t, RW_WIDTH)

        tables = _s5_tables(s5_a_re[l], s5_a_im[l], s5_b_re[l], s5_b_im[l], s5_c_re[l], s5_c_im[l],
                            s5_log_step[l])
        h_s5 = _s5(p_s5.reshape(bsz, seq, S5_WIDTH), tables, s5_d[l], s5_glu_w[l], s5_glu_b[l],
                   s5_out_g[l]).reshape(t, S5_WIDTH)

        wo = w_out[l].astype(BF16)
        xf = _matmul_ln([h_da, h_rw, h_s5],
                        [wo[:DA_WIDTH], wo[DA_WIDTH:DA_WIDTH + RW_WIDTH], wo[DA_WIDTH + RW_WIDTH:]],
                        xf, ln1_g[l], ln1_b[l], tm)
        kv = _matmul(memf, ca_wkv[l].astype(BF16), BF16, n_mem, 1024)
        xf = _cross_attention(xf, kv, ca_wq[l].astype(BF16), ca_wo[l].astype(BF16), ln2_g[l], ln2_b[l], seq, tm)
        w_up = ffn_w_up[l].astype(BF16)
        hff = _ffn_up(xf, w_up[:, :D_FF], w_up[:, D_FF:], ffn_conv_w[l], ffn_conv_b[l], seq, tm)
        xf = _matmul_ln([hff], [ffn_w_down[l].astype(BF16)], xf, ln3_g[l], ln3_b[l], tm)
    return xf.reshape(bsz, seq, d)
```

```python
import functools
import math

import numpy as np
import jax
import jax.numpy as jnp
from jax import lax
from jax.experimental import pallas as pl
from jax.experimental.pallas import tpu as pltpu

F32 = jnp.float32
BF16 = jnp.bfloat16
HI = lax.Precision.HIGHEST

D_MODEL = 1024
DEPTH = 4
CHUNK = 64
LN_EPS = 1e-5
LOG2E = math.log2(math.e)
RMS_EPS = 1e-6

DA_HEAD_DIM = 64
DA_V_DIM = 128
DA_WIDTH = 512
DA_HEADS = 4
ROPE_THETA = 10000.0

RW_HEAD_DIM = 64
RW_WIDTH = 256
RW_HEADS = 4
RW_DECAY_RANK = 32
RW_AAA_RANK = 32
RW_GATE_RANK = 64
RW_IN = 3 * RW_WIDTH + RW_DECAY_RANK + RW_AAA_RANK + RW_GATE_RANK
RW_DECAY_SCALE = math.exp(-0.5)
RW_LNX_EPS = 64e-5

S5_WIDTH = 256
S5_GROUP_CH = 16
S5_GROUPS = 16
S5_STATE = 64
S5_CPLX = S5_GROUPS * S5_STATE

IN_WIDTH = 3 * DA_WIDTH + RW_IN + S5_WIDTH
MEM_HEADS = 4
MEM_HEAD_DIM = 256
D_FF = 2816
ALPHA = (2.0 * DEPTH) ** 0.25

COL_Q, COL_K, COL_V, COL_S5, COL_RW = 0, 512, 1024, 1536, 1792

VMEM_LIMIT = 48 * 1024 * 1024
LANES = 128
SUBLANES = 8


def _cp(*sem):
    return pltpu.CompilerParams(dimension_semantics=sem, vmem_limit_bytes=VMEM_LIMIT)


def _layer_norm_rows(v, g, b):
    mu = jnp.mean(v, axis=-1, keepdims=True)
    c = v - mu
    var = jnp.mean(c * c, axis=-1, keepdims=True)
    return c * lax.rsqrt(var + LN_EPS) * g + b


def _split3(x):
    p1 = x.astype(BF16)
    r1 = x - p1.astype(F32)
    p2 = r1.astype(BF16)
    p3 = (r1 - p2.astype(F32)).astype(BF16)
    return p1, p2, p3


def _segment_sum(x, seg):
    return sum(jnp.dot(p, seg, preferred_element_type=F32) for p in _split3(x))


def _dot_split(a, w_ref):
    a1, a2, _ = _split3(a)
    d = functools.partial(jnp.dot, preferred_element_type=F32)
    return d(a1, w_ref[0]) + (d(a1, w_ref[1]) + d(a2, w_ref[0]))


def _hi_lo(w):
    hi = w.astype(BF16)
    return jnp.stack([hi, (w - hi.astype(F32)).astype(BF16)])


def _mm_kernel(a_ref, w_ref, o_ref):
    o_ref[...] = jnp.dot(a_ref[...].astype(BF16), w_ref[...],
                         preferred_element_type=F32).astype(o_ref.dtype)


def _matmul(a, w, out_dtype, tm, tn):
    m, k = a.shape
    n = w.shape[1]
    return pl.pallas_call(
        _mm_kernel,
        out_shape=jax.ShapeDtypeStruct((m, n), out_dtype),
        grid=(m // tm, n // tn),
        in_specs=[pl.BlockSpec((tm, k), lambda i, j: (i, 0)),
                  pl.BlockSpec((k, tn), lambda i, j: (0, j))],
        out_specs=pl.BlockSpec((tm, tn), lambda i, j: (i, j)),
        compiler_params=_cp("parallel", "arbitrary"),
        name="matmul",
    )(a, w)


def _mm_ln_kernel(n_in, *refs):
    a_refs = refs[:n_in]
    w_refs = refs[n_in:2 * n_in]
    x_ref, g_ref, b_ref, o_ref = refs[2 * n_in:]
    acc = ALPHA * x_ref[...]
    for a_ref, w_ref in zip(a_refs, w_refs):
        acc = acc + jnp.dot(a_ref[...].astype(BF16), w_ref[...], preferred_element_type=F32)
    o_ref[...] = _layer_norm_rows(acc, g_ref[...], b_ref[...])


def _matmul_ln(a_list, w_list, x, g, b, tm):
    m, d = x.shape
    n_in = len(a_list)
    in_specs = [pl.BlockSpec((tm, a.shape[1]), lambda i: (i, 0)) for a in a_list]
    in_specs += [pl.BlockSpec(w.shape, lambda i: (0, 0)) for w in w_list]
    in_specs += [pl.BlockSpec((tm, d), lambda i: (i, 0)),
                 pl.BlockSpec((1, d), lambda i: (0, 0)),
                 pl.BlockSpec((1, d), lambda i: (0, 0))]
    return pl.pallas_call(
        functools.partial(_mm_ln_kernel, n_in),
        out_shape=jax.ShapeDtypeStruct((m, d), F32),
        grid=(m // tm,),
        in_specs=in_specs,
        out_specs=pl.BlockSpec((tm, d), lambda i: (i, 0)),
        compiler_params=_cp("parallel"),
        name="matmul_ln",
    )(*a_list, *w_list, x, g.reshape(1, d), b.reshape(1, d))


def _rope_table_kernel(pos_ref, freq_ref, sign_ref, cos_ref, sin_ref):
    ang = pos_ref[...] * freq_ref[...]
    cos_ref[...] = jnp.cos(ang)
    sin_ref[...] = jnp.sin(ang) * sign_ref[...]


def _rope_tables(positions, tm):
    t = positions.size
    inv_freq = ROPE_THETA ** (-jnp.arange(0, DA_HEAD_DIM, 2, dtype=F32) / DA_HEAD_DIM)
    freq_row = jnp.tile(inv_freq, 4).reshape(1, LANES)
    sign_row = jnp.tile(jnp.concatenate([-jnp.ones((32,), F32), jnp.ones((32,), F32)]), 2).reshape(1, LANES)
    pos_col = positions.astype(F32).reshape(t, 1)
    row = pl.BlockSpec((1, LANES), lambda i: (0, 0))
    return pl.pallas_call(
        _rope_table_kernel,
        out_shape=(jax.ShapeDtypeStruct((t, LANES), F32),) * 2,
        grid=(t // tm,),
        in_specs=[pl.BlockSpec((tm, 1), lambda i: (i, 0)), row, row],
        out_specs=(pl.BlockSpec((tm, LANES), lambda i: (i, 0)),) * 2,
        compiler_params=_cp("parallel"),
        name="rope_tables",
    )(pos_col, freq_row, sign_row)


def _in_proj_kernel(x_ref, w_ref, cos_ref, sin_ref, qo_ref, ko_ref, vt_ref, rw_ref, s5_ref):
    p = jnp.dot(x_ref[...].astype(BF16), w_ref[...], preferred_element_type=F32)
    cos = cos_ref[...]
    sin = sin_ref[...]
    lane = lax.broadcasted_iota(jnp.int32, cos.shape, 1)
    low = (lane % DA_HEAD_DIM) < (DA_HEAD_DIM // 2)

    def rope(t):
        swapped = jnp.where(low, pltpu.roll(t, LANES - 32, axis=1), pltpu.roll(t, 32, axis=1))
        return t * cos + swapped * sin

    scale = DA_HEAD_DIM ** -0.5 * LOG2E
    for h in range(DA_HEADS):
        sl = slice(h * LANES, (h + 1) * LANES)
        qo_ref[:, sl] = (rope(p[:, COL_Q + h * LANES:COL_Q + (h + 1) * LANES]) * scale).astype(BF16)
        ko_ref[:, sl] = rope(p[:, COL_K + h * LANES:COL_K + (h + 1) * LANES]).astype(BF16)
    vt_ref[...] = p[:, COL_V:COL_V + DA_WIDTH].T.astype(BF16)
    s5_ref[...] = p[:, COL_S5:COL_S5 + S5_WIDTH]
    rw_ref[...] = p[:, COL_RW:COL_RW + RW_IN]


def _in_proj(x, w, cos_t, sin_t, seq, tm):
    t, d = x.shape
    nt = seq // tm
    tab = pl.BlockSpec((tm, LANES), lambda i: (i, 0))
    row = lambda n: pl.BlockSpec((tm, n), lambda i: (i, 0))
    return pl.pallas_call(
        _in_proj_kernel,
        out_shape=(jax.ShapeDtypeStruct((t, DA_WIDTH), BF16), jax.ShapeDtypeStruct((t, DA_WIDTH), BF16),
                   jax.ShapeDtypeStruct((t // seq * DA_WIDTH, seq), BF16),
                   jax.ShapeDtypeStruct((t, RW_IN), F32), jax.ShapeDtypeStruct((t, S5_WIDTH), F32)),
        grid=(t // tm,),
        in_specs=[row(d), pl.BlockSpec(w.shape, lambda i: (0, 0)), tab, tab],
        out_specs=(row(DA_WIDTH), row(DA_WIDTH), pl.BlockSpec((DA_WIDTH, tm), lambda i: (i // nt, i % nt)),
                   row(RW_IN), row(S5_WIDTH)),
        compiler_params=_cp("parallel"),
        name="in_proj",
    )(x, w, cos_t, sin_t)


ATT_TQ = 1024
ATT_TK = 512
ATT_CW = 256
ATT_KB = ATT_TQ // ATT_TK


def _attn_kernel(lam_ref, g_ref, q_ref, k_ref, vt_ref, o_ref, qs_ref, p_ref, acc_ref, *, lambda_init):
    i = pl.program_id(2)
    nq = 2 * ATT_TQ
    q = q_ref[...]
    lane = lax.broadcasted_iota(jnp.int32, q.shape, 1)
    zero = jnp.zeros_like(q)
    qs_ref[:ATT_TQ, :] = jnp.where(lane < DA_HEAD_DIM, q, zero)
    qs_ref[ATT_TQ:, :] = jnp.where(lane >= DA_HEAD_DIM, q, zero)
    p_ref[...] = jnp.zeros_like(p_ref)
    acc_ref[...] = jnp.zeros_like(acc_ref)

    def visible_rows(diag, c):
        if diag is None:
            return ATT_TK
        first_query = (c * ATT_CW) % ATT_TQ
        return max(0, min(ATT_TK, first_query + ATT_CW - diag * ATT_TK))

    def pv_update(c, vt, alpha_prev, rows):
        if rows == 0:
            return
        cs = slice(c * ATT_CW, (c + 1) * ATT_CW)
        pv = jnp.dot(vt[:, :rows], p_ref[:rows, cs], preferred_element_type=F32)
        acc_ref[:, cs] = alpha_prev[:, cs] * acc_ref[:, cs] + pv

    def block(j, carry, diag, prev_diag):
        m, l, alpha_prev = carry
        kb = k_ref[pl.ds(pl.multiple_of(j * ATT_TK, ATT_TK), ATT_TK), :]
        vt = vt_ref[:, pl.ds(pl.multiple_of(jnp.maximum(j - 1, 0) * ATT_TK, ATT_TK), ATT_TK)]
        ms, ls, alphas = [], [], []
        for c in range(nq // ATT_CW):
            cs = slice(c * ATT_CW, (c + 1) * ATT_CW)
            rows = visible_rows(diag, c)
            if rows:
                s = lax.dot_general(kb[:rows], qs_ref[cs, :], (((1,), (1,)), ((), ())),
                                    preferred_element_type=F32)
            if rows and diag is not None:
                krow = lax.broadcasted_iota(jnp.int32, s.shape, 0) + diag * ATT_TK
                qcol = (lax.broadcasted_iota(jnp.int32, s.shape, 1) + c * ATT_CW) % ATT_TQ
                s = jnp.where(krow // CHUNK <= qcol // CHUNK, s, -jnp.inf)
            pv_update(c, vt, alpha_prev, visible_rows(prev_diag, c))
            if not rows:
                ms.append(m[:, cs])
                ls.append(l[:, cs])
                alphas.append(jnp.ones_like(m[:, cs]))
                continue
            m_new = jnp.maximum(m[:, cs], jnp.max(s, axis=0, keepdims=True))
            alpha = jnp.exp2(m[:, cs] - m_new)
            p = jnp.exp2(s - m_new)
            ls.append(alpha * l[:, cs] + jnp.sum(p, axis=0, keepdims=True))
            p_ref[:rows, cs] = p.astype(BF16)
            ms.append(m_new)
            alphas.append(alpha)
        return jnp.concatenate(ms, axis=1), jnp.concatenate(ls, axis=1), jnp.concatenate(alphas, axis=1)

    def trip(t, carry, masked):
        for d in range(ATT_KB):
            diag = d if masked else None
            prev_diag = d - 1 if masked and d > 0 else None
            carry = block(ATT_KB * t + d, carry, diag, prev_diag)
        return carry

    init = (jnp.full((1, nq), -jnp.inf, F32), jnp.zeros((1, nq), F32), jnp.ones((1, nq), F32))
    carry = lax.fori_loop(0, i, lambda t, c: trip(t, c, False), init)
    m, l, alpha = trip(i, carry, True)
    last = ATT_KB * i + ATT_KB - 1
    vt = vt_ref[:, pl.ds(pl.multiple_of(last * ATT_TK, ATT_TK), ATT_TK)]
    for c in range(nq // ATT_CW):
        pv_update(c, vt, alpha, visible_rows(ATT_KB - 1, c))
    o = acc_ref[...] / l
    o = o[:, :ATT_TQ] - lam_ref[...] * o[:, ATT_TQ:]
    o = o * lax.rsqrt(jnp.mean(o * o, axis=0, keepdims=True) + RMS_EPS) * g_ref[...]
    o_ref[...] = (o * (1.0 - lambda_init)).T


def _attention(qr, kr, vt, lam, subln_g, lambda_init, bsz, seq):
    nq = seq // ATT_TQ
    col = pl.BlockSpec((LANES, 1), lambda b, h, i: (0, 0))
    qo = pl.BlockSpec((ATT_TQ, LANES), lambda b, h, i: (b * nq + i, h))
    return pl.pallas_call(
        functools.partial(_attn_kernel, lambda_init=lambda_init),
        out_shape=jax.ShapeDtypeStruct((bsz * seq, DA_WIDTH), F32),
        grid=(bsz, DA_HEADS, nq),
        in_specs=[pl.BlockSpec((1, 1), lambda b, h, i: (0, 0)), col, qo,
                  pl.BlockSpec((seq, LANES), lambda b, h, i: (b, h)),
                  pl.BlockSpec((DA_V_DIM, seq), lambda b, h, i: (b * DA_HEADS + h, 0))],
        out_specs=qo,
        scratch_shapes=[pltpu.VMEM((2 * ATT_TQ, LANES), BF16),
                        pltpu.VMEM((ATT_TK, 2 * ATT_TQ), BF16),
                        pltpu.VMEM((DA_V_DIM, 2 * ATT_TQ), F32)],
        compiler_params=_cp("parallel", "parallel", "arbitrary"),
        name="diff_attention",
    )(lam.reshape(1, 1), subln_g.reshape(LANES, 1), qr, kr, vt)


RW_VLO = 4
RW_VHI = RW_HEAD_DIM // RW_VLO
RW_BH = LANES // RW_VLO
RW_TB = 128
RW_PITCH = RW_HEAD_DIM + SUBLANES
RW_TT = 64
RW_NACC = 4
N_KOPS = 5


def _rw_prep_kernel(p_ref, halo_ref, mu_ref, w0_ref, a0_ref, kk_ref, ka_ref, rk_ref,
                    w2_ref, a2_ref, g2_ref, seg_ref,
                    kop_ref, v_ref, g_ref, bonus_ref, tr_ref):
    i = pl.program_id(0)
    n_op = pl.program_id(1)
    bsz = p_ref.shape[0]
    first = jnp.where(i == 0, 0.0, 1.0)
    seg = seg_ref[...]

    def per_batch(b, _):
        p = p_ref[b]
        row = lax.broadcasted_iota(jnp.int32, p.shape, 0)
        last_prev = jnp.broadcast_to(halo_ref[b, SUBLANES - 1:SUBLANES, :], p.shape) * first
        prev = jnp.where(row == 0, last_prev, pltpu.roll(p, 1, axis=0))
        p = p + (prev - p) * mu_ref[...]
        r = p[:, 0:RW_WIDTH]
        k = p[:, RW_WIDTH:2 * RW_WIDTH]
        v = p[:, 2 * RW_WIDTH:3 * RW_WIDTH]
        low = p[:, 3 * RW_WIDTH:RW_IN]
        decay = jnp.exp(-RW_DECAY_SCALE * jax.nn.sigmoid(w0_ref[...] + _dot_split(jnp.tanh(low), w2_ref)))
        a = jax.nn.sigmoid(a0_ref[...] + _dot_split(low, a2_ref))
        g_ref[b] = _dot_split(jax.nn.sigmoid(low), g2_ref)
        kk = k * kk_ref[...]
        kk = kk * lax.rsqrt(jnp.maximum(_segment_sum(kk * kk, seg), 1e-24))
        k = k * (1.0 + (a - 1.0) * ka_ref[...])
        bonus_ref[b] = _segment_sum(r * k * rk_ref[...], seg) * v
        for n, val in enumerate((r, decay, k, -kk, kk * a, v)):
            vt = val.T
            for h in range(RW_HEADS):
                base = pl.multiple_of((b * RW_HEADS + h) * RW_PITCH, SUBLANES)
                tr_ref[n, pl.ds(base, RW_HEAD_DIM), :] = vt[h * RW_HEAD_DIM:(h + 1) * RW_HEAD_DIM, :]
        return 0

    def v_relayout():
        for vh in range(RW_VHI):
            rows = [tr_ref[N_KOPS, pl.ds(vh * RW_VLO + vl, RW_BH, stride=RW_PITCH), :] for vl in range(RW_VLO)]
            v_ref[pl.ds(vh, RW_TB, stride=RW_VHI), :] = jnp.concatenate(rows, axis=0).T

    @pl.when(n_op == 0)
    def _():
        lax.fori_loop(0, bsz, per_batch, 0)
        v_relayout()

    def per_k_group(kg, _):
        for ks in range(SUBLANES):
            kk_i = kg * SUBLANES + ks
            rows = tr_ref[n_op, pl.ds(kk_i, RW_BH, stride=RW_PITCH), :]
            tile = jnp.concatenate([rows] * RW_VLO, axis=0).T
            kop_ref[0, :, kk_i] = tile.reshape(RW_TB // SUBLANES, SUBLANES, LANES)
        return 0
    lax.fori_loop(0, RW_HEAD_DIM // SUBLANES, per_k_group, 0)


def _head_segments():
    idx = np.arange(RW_WIDTH) // RW_HEAD_DIM
    return jnp.asarray((idx[:, None] == idx[None, :]).astype(np.float32)).astype(BF16)


def _rw_prep(proj3, mu, w0, w2, a0, a2, g2, k_k, k_a, r_k):
    bsz, seq, _ = proj3.shape
    hb = RW_TB // SUBLANES
    w2p = _hi_lo(jnp.zeros((LANES, RW_WIDTH), F32).at[0:32].set(w2))
    a2p = _hi_lo(jnp.zeros((LANES, RW_WIDTH), F32).at[32:64].set(a2))
    g2p = _hi_lo(jnp.zeros((LANES, RW_WIDTH), F32).at[64:128].set(g2))
    lowrank = pl.BlockSpec((2, LANES, RW_WIDTH), lambda i, j: (0, 0, 0))
    vec = lambda n: pl.BlockSpec((1, n), lambda i, j: (0, 0))
    mat = lambda r, c: pl.BlockSpec((r, c), lambda i, j: (0, 0))
    tok = pl.BlockSpec((bsz, RW_TB, RW_WIDTH), lambda i, j: (0, i, 0))
    tshape = jax.ShapeDtypeStruct((bsz, seq, RW_WIDTH), F32)
    return pl.pallas_call(
        _rw_prep_kernel,
        out_shape=(jax.ShapeDtypeStruct((N_KOPS, seq // SUBLANES, RW_HEAD_DIM, SUBLANES, LANES), F32),
                   jax.ShapeDtypeStruct((seq * RW_VHI, LANES), F32), tshape, tshape),
        grid=(seq // RW_TB, N_KOPS),
        in_specs=[pl.BlockSpec((bsz, RW_TB, RW_IN), lambda i, j: (0, i, 0)),
                  pl.BlockSpec((bsz, SUBLANES, RW_IN),
                               lambda i, j: (0, jnp.maximum(i * hb - 1, 0), 0)),
                  vec(RW_IN), vec(RW_WIDTH), vec(RW_WIDTH), vec(RW_WIDTH), vec(RW_WIDTH), vec(RW_WIDTH),
                  lowrank, lowrank, lowrank, mat(RW_WIDTH, RW_WIDTH)],
        out_specs=(pl.BlockSpec((1, RW_TB // SUBLANES, RW_HEAD_DIM, SUBLANES, LANES),
                                lambda i, j: (j, i, 0, 0, 0)),
                   pl.BlockSpec((RW_TB * RW_VHI, LANES), lambda i, j: (i, 0)), tok, tok),
        scratch_shapes=[pltpu.VMEM((N_KOPS + 1, RW_BH * RW_PITCH, RW_TB), F32)],
        compiler_params=_cp("parallel", "arbitrary"),
        name="rwkv_prep",
    )(proj3, proj3, mu.reshape(1, RW_IN), w0.reshape(1, -1), a0.reshape(1, -1), k_k.reshape(1, -1),
      k_a.reshape(1, -1), r_k.reshape(1, -1), w2p, a2p, g2p, _head_segments())


def _rw_scan_kernel(r_ref, w_ref, k_ref, an_ref, b_ref, v_ref, y_ref, s_ref):
    @pl.when(pl.program_id(0) == 0)
    def _():
        s_ref[...] = jnp.zeros_like(s_ref)

    nvb = RW_VHI // SUBLANES
    vsl = [slice(vb * SUBLANES, (vb + 1) * SUBLANES) for vb in range(nvb)]
    zeros = lambda: [[jnp.zeros((SUBLANES, LANES), F32) for _ in range(RW_NACC)] for _ in range(nvb)]
    total = lambda acc: [sum(acc[vb][1:], acc[vb][0]) for vb in range(nvb)]

    def body(tb, _):
        def row(ref, kk, ts):
            return jnp.broadcast_to(ref[tb, kk, ts:ts + 1, :], (SUBLANES, LANES))

        zacc = zeros()
        for kk in range(RW_HEAD_DIM):
            an = row(an_ref, kk, 0)
            for vb in range(nvb):
                zacc[vb][kk % RW_NACC] += s_ref[kk, vsl[vb], :] * an
        z = total(zacc)
        for ts in range(SUBLANES):
            t = tb * SUBLANES + ts
            vt = [v_ref[t, vsl[vb], :] for vb in range(nvb)]
            yacc, zacc = zeros(), zeros()
            for kk in range(RW_HEAD_DIM):
                wr, br, kr, rr = row(w_ref, kk, ts), row(b_ref, kk, ts), row(k_ref, kk, ts), row(r_ref, kk, ts)
                an = row(an_ref, kk, ts + 1) if ts + 1 < SUBLANES else None
                for vb in range(nvb):
                    s_new = s_ref[kk, vsl[vb], :] * wr + z[vb] * br + vt[vb] * kr
                    s_ref[kk, vsl[vb], :] = s_new
                    yacc[vb][kk % RW_NACC] += s_new * rr
                    if an is not None:
                        zacc[vb][kk % RW_NACC] += s_new * an
            for vb, y in enumerate(total(yacc)):
                y_ref[t, vsl[vb], :] = y
            z = total(zacc)
        return 0

    lax.fori_loop(0, RW_TT // SUBLANES, body, 0)


def _rw_scan(kops, v4, seq):
    kspec = lambda n: pl.BlockSpec((None, RW_TT // SUBLANES, RW_HEAD_DIM, SUBLANES, LANES),
                                   lambda i: (n, i, 0, 0, 0))
    vspec = pl.BlockSpec((RW_TT, RW_VHI, LANES), lambda i: (i, 0, 0))
    return pl.pallas_call(
        _rw_scan_kernel,
        out_shape=jax.ShapeDtypeStruct((seq, RW_VHI, LANES), F32),
        grid=(seq // RW_TT,),
        in_specs=[kspec(n) for n in range(N_KOPS)] + [vspec],
        out_specs=vspec,
        scratch_shapes=[pltpu.VMEM((RW_HEAD_DIM, RW_VHI, LANES), F32)],
        compiler_params=_cp("arbitrary"),
        name="rwkv_scan",
    )(*([kops] * N_KOPS), v4.reshape(seq, RW_VHI, LANES))


def _rw_post_kernel(y_ref, bonus_ref, gate_ref, lg_ref, lb_ref, seg_ref, o_ref, tr_ref):
    bsz = bonus_ref.shape[0]

    for vh in range(RW_VHI):
        yt = y_ref[pl.ds(vh, RW_TB, stride=RW_VHI), :].T
        for vl in range(RW_VLO):
            tr_ref[pl.ds(vh * RW_VLO + vl, RW_BH, stride=RW_PITCH), :] = yt[vl * RW_BH:(vl + 1) * RW_BH, :]

    seg = seg_ref[...]
    head_mean = lambda a: _segment_sum(a, seg) * (1.0 / RW_HEAD_DIM)

    def per_batch(b, _):
        slabs = []
        for h in range(RW_HEADS):
            base = pl.multiple_of((b * RW_HEADS + h) * RW_PITCH, SUBLANES)
            slabs.append(tr_ref[pl.ds(base, RW_HEAD_DIM), :])
        y = jnp.concatenate(slabs, axis=0).T
        c = y - head_mean(y)
        var = head_mean(c * c)
        y = c * lax.rsqrt(var + RW_LNX_EPS) * lg_ref[...] + lb_ref[...]
        o_ref[b] = (y + bonus_ref[b]) * gate_ref[b]
        return 0
    lax.fori_loop(0, bsz, per_batch, 0)


def _rw_post(y4, bonus, gate, lnx_g, lnx_b):
    bsz, seq, _ = bonus.shape
    tok = pl.BlockSpec((bsz, RW_TB, RW_WIDTH), lambda i: (0, i, 0))
    vec = pl.BlockSpec((1, RW_WIDTH), lambda i: (0, 0))
    return pl.pallas_call(
        _rw_post_kernel,
        out_shape=jax.ShapeDtypeStruct((bsz, seq, RW_WIDTH), F32),
        grid=(seq // RW_TB,),
        in_specs=[pl.BlockSpec((RW_TB * RW_VHI, LANES), lambda i: (i, 0)), tok, tok, vec, vec,
                  pl.BlockSpec((RW_WIDTH, RW_WIDTH), lambda i: (0, 0))],
        out_specs=tok,
        scratch_shapes=[pltpu.VMEM((RW_BH * RW_PITCH, RW_TB), F32)],
        compiler_params=_cp("parallel"),
        name="rwkv_post",
    )(y4.reshape(seq * RW_VHI, LANES), bonus, gate, lnx_g.reshape(1, -1), lnx_b.reshape(1, -1), _head_segments())


S5_TT = 128
S5_PITCH = S5_TT + SUBLANES


def _s5_kernel(u_ref, wb_ref, wc_ref, a_ref, d_ref, gw_ref, gb_ref, og_ref, o_ref, x_ref, carry_ref):
    @pl.when(pl.program_id(0) == 0)
    def _():
        carry_ref[...] = jnp.zeros_like(carry_ref)

    n = S5_CPLX
    nc = n // LANES
    bsz = u_ref.shape[0]

    def project(b, _):
        base = b * S5_PITCH
        bu = jnp.dot(u_ref[b].astype(BF16), wb_ref[...], preferred_element_type=F32)
        for c in range(2 * nc):
            x_ref[c, pl.ds(base, S5_TT), :] = bu[:, c * LANES:(c + 1) * LANES]
        return 0
    for b in range(bsz):
        project(b, 0)

    ar = [jnp.broadcast_to(a_ref[:, c * LANES:(c + 1) * LANES], (bsz, LANES)) for c in range(nc)]
    ai = [jnp.broadcast_to(a_ref[:, n + c * LANES:n + (c + 1) * LANES], (bsz, LANES)) for c in range(nc)]

    def steps(tb, carry):
        cr, ci = list(carry[0]), list(carry[1])
        for ts in range(SUBLANES):
            rows = pl.ds(tb * SUBLANES + ts, bsz, stride=S5_PITCH)
            for c in range(nc):
                cr[c], ci[c] = (ar[c] * cr[c] - ai[c] * ci[c] + x_ref[c, rows, :],
                                ar[c] * ci[c] + ai[c] * cr[c] + x_ref[nc + c, rows, :])
                x_ref[c, rows, :] = cr[c]
                x_ref[nc + c, rows, :] = ci[c]
        return tuple(cr), tuple(ci)

    init = (tuple(carry_ref[c] for c in range(nc)), tuple(carry_ref[nc + c] for c in range(nc)))
    cr, ci = lax.fori_loop(0, S5_TT // SUBLANES, steps, init)
    for c in range(nc):
        carry_ref[c] = cr[c]
        carry_ref[nc + c] = ci[c]

    def readout(b, _):
        base = b * S5_PITCH
        u = u_ref[b]
        x = jnp.concatenate([x_ref[c, pl.ds(base, S5_TT), :] for c in range(2 * nc)], axis=1)
        y = jnp.dot(x.astype(BF16), wc_ref[...], preferred_element_type=F32) + d_ref[...] * u
        z = jax.nn.gelu(y)
        z = z * jax.nn.sigmoid(jnp.dot(z.astype(BF16), gw_ref[...], preferred_element_type=F32) + gb_ref[...])
        o_ref[b] = z * lax.rsqrt(jnp.mean(z * z, axis=-1, keepdims=True) + RMS_EPS) * og_ref[...]
        return 0
    for b in range(bsz):
        readout(b, 0)


def _s5_tables(a_re, a_im, b_re, b_im, c_re, c_im, log_step):
    delta = jnp.exp(log_step.astype(F32))[:, None]
    mag = jnp.exp(delta * a_re)

    def a_pow(n):
        return (mag ** n) * jnp.cos(n * delta * a_im), (mag ** n) * jnp.sin(n * delta * a_im)

    abr, abi = a_pow(1)
    den = a_re * a_re + a_im * a_im
    qr = ((abr - 1.0) * a_re + abi * a_im) / den
    qi = (abi * a_re - (abr - 1.0) * a_im) / den
    bbr = qr[..., None] * b_re - qi[..., None] * b_im
    bbi = qr[..., None] * b_im + qi[..., None] * b_re
    eye = jnp.eye(S5_GROUPS, dtype=F32)
    blk_b = lambda m: jnp.einsum('gph,gk->ghkp', m, eye).reshape(S5_WIDTH, S5_CPLX)
    blk_c = lambda m: jnp.einsum('ghp,gk->gpkh', m, eye).reshape(S5_CPLX, S5_WIDTH)
    wb = jnp.concatenate([blk_b(bbr), blk_b(bbi)], axis=1).astype(BF16)
    wc = jnp.concatenate([blk_c(c_re), -blk_c(c_im)], axis=0).astype(BF16)
    a_row = jnp.concatenate([abr.reshape(1, -1), abi.reshape(1, -1)], axis=1)
    return wb, wc, a_row


def _s5(proj3, tables, d_skip, glu_w, glu_b, out_g):
    wb, wc, a_row = tables
    bsz, seq, _ = proj3.shape
    full = lambda a: pl.BlockSpec(a.shape, lambda i: (0,) * a.ndim)
    vec = pl.BlockSpec((1, S5_WIDTH), lambda i: (0, 0))
    gw = glu_w.astype(BF16)
    return pl.pallas_call(
        _s5_kernel,
        out_shape=jax.ShapeDtypeStruct((bsz, seq, S5_WIDTH), F32),
        grid=(seq // S5_TT,),
        in_specs=[pl.BlockSpec((bsz, S5_TT, S5_WIDTH), lambda i: (0, i, 0)),
                  full(wb), full(wc), full(a_row), vec, full(gw), vec, vec],
        out_specs=pl.BlockSpec((bsz, S5_TT, S5_WIDTH), lambda i: (0, i, 0)),
        scratch_shapes=[pltpu.VMEM((2 * S5_CPLX // LANES, bsz * S5_PITCH, LANES), F32),
                        pltpu.VMEM((2 * S5_CPLX // LANES, bsz, LANES), F32)],
        compiler_params=_cp("arbitrary"),
        name="s5",
    )(proj3, wb, wc, a_row, d_skip.reshape(1, -1), gw, glu_b.reshape(1, -1), out_g.reshape(1, -1))


def _xattn_kernel(x_ref, kv_ref, wq_ref, wo_ref, g_ref, b_ref, o_ref):
    x = x_ref[...]
    q = jnp.dot(x.astype(BF16), wq_ref[...], preferred_element_type=F32)
    outs = []
    for h in range(MEM_HEADS):
        sl = slice(h * MEM_HEAD_DIM, (h + 1) * MEM_HEAD_DIM)
        kh = kv_ref[:, sl]
        vh = kv_ref[:, D_MODEL + h * MEM_HEAD_DIM:D_MODEL + (h + 1) * MEM_HEAD_DIM]
        s = lax.dot_general(q[:, sl].astype(BF16), kh, (((1,), (1,)), ((), ())),
                            preferred_element_type=F32) * (MEM_HEAD_DIM ** -0.5)
        p = jnp.exp(s - jnp.max(s, axis=-1, keepdims=True))
        p = p / jnp.sum(p, axis=-1, keepdims=True)
        outs.append(jnp.dot(p.astype(BF16), vh, preferred_element_type=F32))
    o = jnp.concatenate(outs, axis=-1)
    h_out = jnp.dot(o.astype(BF16), wo_ref[...], preferred_element_type=F32)
    o_ref[...] = _layer_norm_rows(ALPHA * x + h_out, g_ref[...], b_ref[...])


def _cross_attention(x, kv, wq, wo, g, b, seq, tm):
    t, d = x.shape
    n_mem = kv.shape[0] // (t // seq)
    tiles_per_seq = seq // tm
    w = pl.BlockSpec((d, d), lambda i: (0, 0))
    vec = pl.BlockSpec((1, d), lambda i: (0, 0))
    blk = pl.BlockSpec((tm, d), lambda i: (i, 0))
    return pl.pallas_call(
        _xattn_kernel,
        out_shape=jax.ShapeDtypeStruct((t, d), F32),
        grid=(t // tm,),
        in_specs=[blk, pl.BlockSpec((n_mem, 2 * d), lambda i: (i // tiles_per_seq, 0)), w, w, vec, vec],
        out_specs=blk,
        compiler_params=_cp("parallel"),
        name="cross_attention",
    )(x, kv, wq, wo, g.reshape(1, d), b.reshape(1, d))


FFN_TN = 1408


def _ffn_up_kernel(x_ref, halo_ref, wa_ref, wg_ref, cw_ref, cb_ref, o_ref, *, tiles_per_seq):
    i = pl.program_id(0)
    x = x_ref[...].astype(BF16)
    first = jnp.where(i % tiles_per_seq == 0, 0.0, 1.0)
    xh = (halo_ref[...] * first).astype(BF16)
    a8 = jnp.dot(jnp.concatenate([xh, x], axis=0), wa_ref[...], preferred_element_type=F32)
    a0 = a8[SUBLANES:]
    a1 = pltpu.roll(a8, 1, axis=0)[SUBLANES:]
    a2 = pltpu.roll(a8, 2, axis=0)[SUBLANES:]
    conv = cb_ref[...] + cw_ref[0:1, :] * a2 + cw_ref[1:2, :] * a1 + cw_ref[2:3, :] * a0
    g = jnp.dot(x, wg_ref[...], preferred_element_type=F32)
    o_ref[...] = (jax.nn.silu(conv) * g).astype(o_ref.dtype)


def _ffn_up(x, wa, wg, conv_w, conv_b, seq, tm):
    t, d = x.shape
    hb = tm // SUBLANES
    cw = jnp.zeros((SUBLANES, D_FF), F32).at[0:3].set(conv_w)
    wspec = pl.BlockSpec((d, FFN_TN), lambda i, j: (0, j))
    return pl.pallas_call(
        functools.partial(_ffn_up_kernel, tiles_per_seq=seq // tm),
        out_shape=jax.ShapeDtypeStruct((t, D_FF), BF16),
        grid=(t // tm, D_FF // FFN_TN),
        in_specs=[pl.BlockSpec((tm, d), lambda i, j: (i, 0)),
                  pl.BlockSpec((SUBLANES, d), lambda i, j: (jnp.maximum(i * hb - 1, 0), 0)),
                  wspec, wspec,
                  pl.BlockSpec((SUBLANES, FFN_TN), lambda i, j: (0, j)),
                  pl.BlockSpec((1, FFN_TN), lambda i, j: (0, j))],
        out_specs=pl.BlockSpec((tm, FFN_TN), lambda i, j: (i, j)),
        compiler_params=_cp("parallel", "arbitrary"),
        name="ffn_up",
    )(x, x, wa, wg, cw, conv_b.reshape(1, D_FF))


def kernel(x, mem, positions, w_in, da_lam_q1, da_lam_k1, da_lam_q2, da_lam_k2, da_subln_g, rw_mu, rw_w0, rw_w2, rw_a0, rw_a2, rw_g2, rw_k_k, rw_k_a, rw_r_k, rw_lnx_g, rw_lnx_b, s5_a_re, s5_a_im, s5_b_re, s5_b_im, s5_c_re, s5_c_im, s5_d, s5_log_step, s5_glu_w, s5_glu_b, s5_out_g, w_out, ln1_g, ln1_b, ca_wq, ca_wkv, ca_wo, ln2_g, ln2_b, ffn_w_up, ffn_conv_w, ffn_conv_b, ffn_w_down, ln3_g, ln3_b):
    bsz, seq, d = x.shape
    t = bsz * seq
    tm = min(512, seq)
    n_mem = mem.shape[1]
    xf = x.reshape(t, d)
    memf = mem.reshape(bsz * n_mem, d)
    cos_t, sin_t = _rope_tables(positions, tm)

    for l in range(DEPTH):
        lambda_init = 0.8 - 0.6 * math.exp(-0.3 * l)
        wi = w_in[l]
        w_perm = jnp.concatenate([wi[:, :3 * DA_WIDTH], wi[:, 3 * DA_WIDTH + RW_IN:],
                                  wi[:, 3 * DA_WIDTH:3 * DA_WIDTH + RW_IN]], axis=1).astype(BF16)
        qr, kr, vt, p_rw, p_s5 = _in_proj(xf, w_perm, cos_t, sin_t, seq, tm)
        lam = (jnp.exp(jnp.sum(da_lam_q1[l] * da_lam_k1[l])) - jnp.exp(jnp.sum(da_lam_q2[l] * da_lam_k2[l]))
               + lambda_init)
        kops, v4, gate, bonus = _rw_prep(p_rw.reshape(bsz, seq, RW_IN), rw_mu[l], rw_w0[l], rw_w2[l],
                                         rw_a0[l], rw_a2[l], rw_g2[l], rw_k_k[l], rw_k_a[l],
                                         rw_r_k[l].reshape(-1))
        h_da = _attention(qr, kr, vt, lam, da_subln_g[l], lambda_init, bsz, seq)
        y4 = _rw_scan(kops, v4, seq)
        h_rw = _rw_post(y4, bonus, gate, rw_lnx_g[l], rw_lnx_b[l]).reshape(t, RW_WIDTH)

        tables = _s5_tables(s5_a_re[l], s5_a_im[l], s5_b_re[l], s5_b_im[l], s5_c_re[l], s5_c_im[l],
                            s5_log_step[l])
        h_s5 = _s5(p_s5.reshape(bsz, seq, S5_WIDTH), tables, s5_d[l], s5_glu_w[l], s5_glu_b[l],
                   s5_out_g[l]).reshape(t, S5_WIDTH)

        wo = w_out[l].astype(BF16)
        xf = _matmul_ln([h_da, h_rw, h_s5],
                        [wo[:DA_WIDTH], wo[DA_WIDTH:DA_WIDTH + RW_WIDTH], wo[DA_WIDTH + RW_WIDTH:]],
                        xf, ln1_g[l], ln1_b[l], tm)
        kv = _matmul(memf, ca_wkv[l].astype(BF16), BF16, n_mem, 1024)
        xf = _cross_attention(xf, kv, ca_wq[l].astype(BF16), ca_wo[l].astype(BF16), ln2_g[l], ln2_b[l], seq, tm)
        w_up = ffn_w_up[l].astype(BF16)
        hff = _ffn_up(xf, w_up[:, :D_FF], w_up[:, D_FF:], ffn_conv_w[l], ffn_conv_b[l], seq, tm)
        xf = _matmul_ln([hff], [ffn_w_down[l].astype(BF16)], xf, ln3_g[l], ln3_b[l], tm)
    return xf.reshape(bsz, seq, d)
```

```python
import functools
import math

import numpy as np
import jax
import jax.numpy as jnp
from jax import lax
from jax.experimental import pallas as pl
from jax.experimental.pallas import tpu as pltpu

F32 = jnp.float32
BF16 = jnp.bfloat16
HI = lax.Precision.HIGHEST

D_MODEL = 1024
DEPTH = 4
CHUNK = 64
LN_EPS = 1e-5
LOG2E = math.log2(math.e)
RMS_EPS = 1e-6

DA_HEAD_DIM = 64
DA_V_DIM = 128
DA_WIDTH = 512
DA_HEADS = 4
ROPE_THETA = 10000.0

RW_HEAD_DIM = 64
RW_WIDTH = 256
RW_HEADS = 4
RW_DECAY_RANK = 32
RW_AAA_RANK = 32
RW_GATE_RANK = 64
RW_IN = 3 * RW_WIDTH + RW_DECAY_RANK + RW_AAA_RANK + RW_GATE_RANK
RW_DECAY_SCALE = math.exp(-0.5)
RW_LNX_EPS = 64e-5

S5_WIDTH = 256
S5_GROUP_CH = 16
S5_GROUPS = 16
S5_STATE = 64
S5_CPLX = S5_GROUPS * S5_STATE

IN_WIDTH = 3 * DA_WIDTH + RW_IN + S5_WIDTH
MEM_HEADS = 4
MEM_HEAD_DIM = 256
D_FF = 2816
ALPHA = (2.0 * DEPTH) ** 0.25

COL_Q, COL_K, COL_V, COL_S5, COL_RW = 0, 512, 1024, 1536, 1792

VMEM_LIMIT = 48 * 1024 * 1024
LANES = 128
SUBLANES = 8


def _cp(*sem):
    return pltpu.CompilerParams(dimension_semantics=sem, vmem_limit_bytes=VMEM_LIMIT)


def _layer_norm_rows(v, g, b):
    mu = jnp.mean(v, axis=-1, keepdims=True)
    c = v - mu
    var = jnp.mean(c * c, axis=-1, keepdims=True)
    return c * lax.rsqrt(var + LN_EPS) * g + b


def _split3(x):
    p1 = x.astype(BF16)
    r1 = x - p1.astype(F32)
    p2 = r1.astype(BF16)
    p3 = (r1 - p2.astype(F32)).astype(BF16)
    return p1, p2, p3


def _segment_sum(x, seg):
    return sum(jnp.dot(p, seg, preferred_element_type=F32) for p in _split3(x))


def _dot_split(a, w_ref):
    a1, a2, _ = _split3(a)
    d = functools.partial(jnp.dot, preferred_element_type=F32)
    return d(a1, w_ref[0]) + (d(a1, w_ref[1]) + d(a2, w_ref[0]))


def _hi_lo(w):
    hi = w.astype(BF16)
    return jnp.stack([hi, (w - hi.astype(F32)).astype(BF16)])


def _mm_kernel(a_ref, w_ref, o_ref):
    o_ref[...] = jnp.dot(a_ref[...].astype(BF16), w_ref[...],
                         preferred_element_type=F32).astype(o_ref.dtype)


def _matmul(a, w, out_dtype, tm, tn):
    m, k = a.shape
    n = w.shape[1]
    return pl.pallas_call(
        _mm_kernel,
        out_shape=jax.ShapeDtypeStruct((m, n), out_dtype),
        grid=(m // tm, n // tn),
        in_specs=[pl.BlockSpec((tm, k), lambda i, j: (i, 0)),
                  pl.BlockSpec((k, tn), lambda i, j: (0, j))],
        out_specs=pl.BlockSpec((tm, tn), lambda i, j: (i, j)),
        compiler_params=_cp("parallel", "arbitrary"),
        name="matmul",
    )(a, w)


def _mm_ln_kernel(n_in, *refs):
    a_refs = refs[:n_in]
    w_refs = refs[n_in:2 * n_in]
    x_ref, g_ref, b_ref, o_ref = refs[2 * n_in:]
    acc = ALPHA * x_ref[...]
    for a_ref, w_ref in zip(a_refs, w_refs):
        acc = acc + jnp.dot(a_ref[...].astype(BF16), w_ref[...], preferred_element_type=F32)
    o_ref[...] = _layer_norm_rows(acc, g_ref[...], b_ref[...])


def _matmul_ln(a_list, w_list, x, g, b, tm):
    m, d = x.shape
    n_in = len(a_list)
    in_specs = [pl.BlockSpec((tm, a.shape[1]), lambda i: (i, 0)) for a in a_list]
    in_specs += [pl.BlockSpec(w.shape, lambda i: (0, 0)) for w in w_list]
    in_specs += [pl.BlockSpec((tm, d), lambda i: (i, 0)),
                 pl.BlockSpec((1, d), lambda i: (0, 0)),
                 pl.BlockSpec((1, d), lambda i: (0, 0))]
    return pl.pallas_call(
        functools.partial(_mm_ln_kernel, n_in),
        out_shape=jax.ShapeDtypeStruct((m, d), F32),
        grid=(m // tm,),
        in_specs=in_specs,
        out_specs=pl.BlockSpec((tm, d), lambda i: (i, 0)),
        compiler_params=_cp("parallel"),
        name="matmul_ln",
    )(*a_list, *w_list, x, g.reshape(1, d), b.reshape(1, d))


def _rope_table_kernel(pos_ref, freq_ref, sign_ref, cos_ref, sin_ref):
    ang = pos_ref[...] * freq_ref[...]
    cos_ref[...] = jnp.cos(ang)
    sin_ref[...] = jnp.sin(ang) * sign_ref[...]


def _rope_tables(positions, tm):
    t = positions.size
    inv_freq = ROPE_THETA ** (-jnp.arange(0, DA_HEAD_DIM, 2, dtype=F32) / DA_HEAD_DIM)
    freq_row = jnp.tile(inv_freq, 4).reshape(1, LANES)
    sign_row = jnp.tile(jnp.concatenate([-jnp.ones((32,), F32), jnp.ones((32,), F32)]), 2).reshape(1, LANES)
    pos_col = positions.astype(F32).reshape(t, 1)
    row = pl.BlockSpec((1, LANES), lambda i: (0, 0))
    return pl.pallas_call(
        _rope_table_kernel,
        out_shape=(jax.ShapeDtypeStruct((t, LANES), F32),) * 2,
        grid=(t // tm,),
        in_specs=[pl.BlockSpec((tm, 1), lambda i: (i, 0)), row, row],
        out_specs=(pl.BlockSpec((tm, LANES), lambda i: (i, 0)),) * 2,
        compiler_params=_cp("parallel"),
        name="rope_tables",
    )(pos_col, freq_row, sign_row)


def _in_proj_kernel(x_ref, w_ref, cos_ref, sin_ref, qo_ref, ko_ref, vt_ref, rw_ref, s5_ref):
    p = jnp.dot(x_ref[...].astype(BF16), w_ref[...], preferred_element_type=F32)
    cos = cos_ref[...]
    sin = sin_ref[...]
    lane = lax.broadcasted_iota(jnp.int32, cos.shape, 1)
    low = (lane % DA_HEAD_DIM) < (DA_HEAD_DIM // 2)

    def rope(t):
        swapped = jnp.where(low, pltpu.roll(t, LANES - 32, axis=1), pltpu.roll(t, 32, axis=1))
        return t * cos + swapped * sin

    scale = DA_HEAD_DIM ** -0.5 * LOG2E
    for h in range(DA_HEADS):
        sl = slice(h * LANES, (h + 1) * LANES)
        qo_ref[:, sl] = (rope(p[:, COL_Q + h * LANES:COL_Q + (h + 1) * LANES]) * scale).astype(BF16)
        ko_ref[:, sl] = rope(p[:, COL_K + h * LANES:COL_K + (h + 1) * LANES]).astype(BF16)
    vt_ref[...] = p[:, COL_V:COL_V + DA_WIDTH].T.astype(BF16)
    s5_ref[...] = p[:, COL_S5:COL_S5 + S5_WIDTH]
    rw_ref[...] = p[:, COL_RW:COL_RW + RW_IN]


def _in_proj(x, w, cos_t, sin_t, seq, tm):
    t, d = x.shape
    nt = seq // tm
    tab = pl.BlockSpec((tm, LANES), lambda i: (i, 0))
    row = lambda n: pl.BlockSpec((tm, n), lambda i: (i, 0))
    return pl.pallas_call(
        _in_proj_kernel,
        out_shape=(jax.ShapeDtypeStruct((t, DA_WIDTH), BF16), jax.ShapeDtypeStruct((t, DA_WIDTH), BF16),
                   jax.ShapeDtypeStruct((t // seq * DA_WIDTH, seq), BF16),
                   jax.ShapeDtypeStruct((t, RW_IN), F32), jax.ShapeDtypeStruct((t, S5_WIDTH), F32)),
        grid=(t // tm,),
        in_specs=[row(d), pl.BlockSpec(w.shape, lambda i: (0, 0)), tab, tab],
        out_specs=(row(DA_WIDTH), row(DA_WIDTH), pl.BlockSpec((DA_WIDTH, tm), lambda i: (i // nt, i % nt)),
                   row(RW_IN), row(S5_WIDTH)),
        compiler_params=_cp("parallel"),
        name="in_proj",
    )(x, w, cos_t, sin_t)


ATT_TQ = 1024
ATT_TK = 512
ATT_CW = 256
ATT_KB = ATT_TQ // ATT_TK


def _attn_kernel(lam_ref, g_ref, q_ref, k_ref, vt_ref, o_ref, qs_ref, p_ref, acc_ref, *, lambda_init):
    i = pl.program_id(2)
    nq = 2 * ATT_TQ
    q = q_ref[...]
    lane = lax.broadcasted_iota(jnp.int32, q.shape, 1)
    zero = jnp.zeros_like(q)
    qs_ref[:ATT_TQ, :] = jnp.where(lane < DA_HEAD_DIM, q, zero)
    qs_ref[ATT_TQ:, :] = jnp.where(lane >= DA_HEAD_DIM, q, zero)
    p_ref[...] = jnp.zeros_like(p_ref)
    acc_ref[...] = jnp.zeros_like(acc_ref)

    def visible_rows(diag, c):
        if diag is None:
            return ATT_TK
        first_query = (c * ATT_CW) % ATT_TQ
        return max(0, min(ATT_TK, first_query + ATT_CW - diag * ATT_TK))

    def pv_update(c, vt, alpha_prev, rows):
        if rows == 0:
            return
        cs = slice(c * ATT_CW, (c + 1) * ATT_CW)
        pv = jnp.dot(vt[:, :rows], p_ref[:rows, cs], preferred_element_type=F32)
        acc_ref[:, cs] = alpha_prev[:, cs] * acc_ref[:, cs] + pv

    def block(j, carry, diag, prev_diag):
        m, l, alpha_prev = carry
        kb = k_ref[pl.ds(pl.multiple_of(j * ATT_TK, ATT_TK), ATT_TK), :]
        vt = vt_ref[:, pl.ds(pl.multiple_of(jnp.maximum(j - 1, 0) * ATT_TK, ATT_TK), ATT_TK)]
        ms, ls, alphas = [], [], []
        for c in range(nq // ATT_CW):
            cs = slice(c * ATT_CW, (c + 1) * ATT_CW)
            rows = visible_rows(diag, c)
            if rows:
                s = lax.dot_general(kb[:rows], qs_ref[cs, :], (((1,), (1,)), ((), ())),
                                    preferred_element_type=F32)
            if rows and diag is not None:
                krow = lax.broadcasted_iota(jnp.int32, s.shape, 0) + diag * ATT_TK
                qcol = (lax.broadcasted_iota(jnp.int32, s.shape, 1) + c * ATT_CW) % ATT_TQ
                s = jnp.where(krow // CHUNK <= qcol // CHUNK, s, -jnp.inf)
            pv_update(c, vt, alpha_prev, visible_rows(prev_diag, c))
            if not rows:
                ms.append(m[:, cs])
                ls.append(l[:, cs])
                alphas.append(jnp.ones_like(m[:, cs]))
                continue
            m_new = jnp.maximum(m[:, cs], jnp.max(s, axis=0, keepdims=True))
            alpha = jnp.exp2(m[:, cs] - m_new)
            p = jnp.exp2(s - m_new)
            ls.append(alpha * l[:, cs] + jnp.sum(p, axis=0, keepdims=True))
            p_ref[:rows, cs] = p.astype(BF16)
            ms.append(m_new)
            alphas.append(alpha)
        return jnp.concatenate(ms, axis=1), jnp.concatenate(ls, axis=1), jnp.concatenate(alphas, axis=1)

    def trip(t, carry, masked):
        for d in range(ATT_KB):
            diag = d if masked else None
            prev_diag = d - 1 if masked and d > 0 else None
            carry = block(ATT_KB * t + d, carry, diag, prev_diag)
        return carry

    init = (jnp.full((1, nq), -jnp.inf, F32), jnp.zeros((1, nq), F32), jnp.ones((1, nq), F32))
    carry = lax.fori_loop(0, i, lambda t, c: trip(t, c, False), init)
    m, l, alpha = trip(i, carry, True)
    last = ATT_KB * i + ATT_KB - 1
    vt = vt_ref[:, pl.ds(pl.multiple_of(last * ATT_TK, ATT_TK), ATT_TK)]
    for c in range(nq // ATT_CW):
        pv_update(c, vt, alpha, visible_rows(ATT_KB - 1, c))
    o = acc_ref[...] / l
    o = o[:, :ATT_TQ] - lam_ref[...] * o[:, ATT_TQ:]
    o = o * lax.rsqrt(jnp.mean(o * o, axis=0, keepdims=True) + RMS_EPS) * g_ref[...]
    o_ref[...] = (o * (1.0 - lambda_init)).T


def _attention(qr, kr, vt, lam, subln_g, lambda_init, bsz, seq):
    nq = seq // ATT_TQ
    col = pl.BlockSpec((LANES, 1), lambda b, h, i: (0, 0))
    qo = pl.BlockSpec((ATT_TQ, LANES), lambda b, h, i: (b * nq + i, h))
    return pl.pallas_call(
        functools.partial(_attn_kernel, lambda_init=lambda_init),
        out_shape=jax.ShapeDtypeStruct((bsz * seq, DA_WIDTH), F32),
        grid=(bsz, DA_HEADS, nq),
        in_specs=[pl.BlockSpec((1, 1), lambda b, h, i: (0, 0)), col, qo,
                  pl.BlockSpec((seq, LANES), lambda b, h, i: (b, h)),
                  pl.BlockSpec((DA_V_DIM, seq), lambda b, h, i: (b * DA_HEADS + h, 0))],
        out_specs=qo,
        scratch_shapes=[pltpu.VMEM((2 * ATT_TQ, LANES), BF16),
                        pltpu.VMEM((ATT_TK, 2 * ATT_TQ), BF16),
                        pltpu.VMEM((DA_V_DIM, 2 * ATT_TQ), F32)],
        compiler_params=_cp("parallel", "parallel", "arbitrary"),
        name="diff_attention",
    )(lam.reshape(1, 1), subln_g.reshape(LANES, 1), qr, kr, vt)


RW_VLO = 4
RW_VHI = RW_HEAD_DIM // RW_VLO
RW_BH = LANES // RW_VLO
RW_TB = 128
RW_PITCH = RW_HEAD_DIM + SUBLANES
RW_TT = 64
RW_NACC = 4
N_KOPS = 5


def _rw_prep_kernel(p_ref, halo_ref, mu_ref, w0_ref, a0_ref, kk_ref, ka_ref, rk_ref,
                    w2_ref, a2_ref, g2_ref, seg_ref,
                    kop_ref, v_ref, g_ref, bonus_ref, tr_ref, *, n_blocks):
    i = pl.program_id(0)
    n_op = pl.program_id(1)
    blk = jnp.minimum(i, n_blocks - 1)
    cur = i % 2
    prv = 1 - cur
    first = jnp.where(blk == 0, 0.0, 1.0)
    seg = seg_ref[...]

    @pl.when((i == 0) & (n_op == 0))
    def _():
        tr_ref[1] = jnp.zeros(tr_ref.shape[1:], F32)

    def compute(b):
        p = p_ref[b]
        row = lax.broadcasted_iota(jnp.int32, p.shape, 0)
        last_prev = jnp.broadcast_to(halo_ref[b, SUBLANES - 1:SUBLANES, :], p.shape) * first
        prev = jnp.where(row == 0, last_prev, pltpu.roll(p, 1, axis=0))
        p = p + (prev - p) * mu_ref[...]
        r = p[:, 0:RW_WIDTH]
        k = p[:, RW_WIDTH:2 * RW_WIDTH]
        v = p[:, 2 * RW_WIDTH:3 * RW_WIDTH]
        low = p[:, 3 * RW_WIDTH:RW_IN]
        decay = jnp.exp(-RW_DECAY_SCALE * jax.nn.sigmoid(w0_ref[...] + _dot_split(jnp.tanh(low), w2_ref)))
        a = jax.nn.sigmoid(a0_ref[...] + _dot_split(low, a2_ref))
        g_ref[b] = _dot_split(jax.nn.sigmoid(low), g2_ref)
        kk = k * kk_ref[...]
        kk = kk * lax.rsqrt(jnp.maximum(_segment_sum(kk * kk, seg), 1e-24))
        k = k * (1.0 + (a - 1.0) * ka_ref[...])
        bonus_ref[b] = _segment_sum(r * k * rk_ref[...], seg) * v
        for n, val in enumerate((r, decay, k, -kk, kk * a, v)):
            vt = val.T
            for h in range(RW_HEADS):
                base = pl.multiple_of((b * RW_HEADS + h) * RW_PITCH, SUBLANES)
                tr_ref[cur, n, pl.ds(base, RW_HEAD_DIM), :] = vt[h * RW_HEAD_DIM:(h + 1) * RW_HEAD_DIM, :]

    def v_relayout():
        for vh in range(RW_VHI):
            rows = [tr_ref[cur, N_KOPS, pl.ds(vh * RW_VLO + vl, RW_BH, stride=RW_PITCH), :]
                    for vl in range(RW_VLO)]
            v_ref[pl.ds(vh, RW_TB, stride=RW_VHI), :] = jnp.concatenate(rows, axis=0).T

    def emit():
        for kk_i in range(RW_HEAD_DIM):
            rows = tr_ref[prv, n_op, pl.ds(kk_i, RW_BH, stride=RW_PITCH), :]
            tile = jnp.concatenate([rows] * RW_VLO, axis=0).T
            kop_ref[0, :, kk_i] = tile.reshape(RW_TB // SUBLANES, SUBLANES, LANES)

    per_step = p_ref.shape[0] // (N_KOPS - 1)

    @pl.when(n_op < N_KOPS - 1)
    def _():
        emit()
        for s in range(per_step):
            compute(n_op * per_step + s)

    @pl.when(n_op == N_KOPS - 1)
    def _():
        emit()
        v_relayout()


def _head_segments():
    idx = np.arange(RW_WIDTH) // RW_HEAD_DIM
    return jnp.asarray((idx[:, None] == idx[None, :]).astype(np.float32)).astype(BF16)


def _rw_prep(proj3, mu, w0, w2, a0, a2, g2, k_k, k_a, r_k):
    bsz, seq, _ = proj3.shape
    hb = RW_TB // SUBLANES
    w2p = _hi_lo(jnp.zeros((LANES, RW_WIDTH), F32).at[0:32].set(w2))
    a2p = _hi_lo(jnp.zeros((LANES, RW_WIDTH), F32).at[32:64].set(a2))
    g2p = _hi_lo(jnp.zeros((LANES, RW_WIDTH), F32).at[64:128].set(g2))
    lowrank = pl.BlockSpec((2, LANES, RW_WIDTH), lambda i, j: (0, 0, 0))
    vec = lambda n: pl.BlockSpec((1, n), lambda i, j: (0, 0))
    mat = lambda r, c: pl.BlockSpec((r, c), lambda i, j: (0, 0))
    n_blocks = seq // RW_TB
    assert bsz % (N_KOPS - 1) == 0
    blk = lambda i: jnp.minimum(i, n_blocks - 1)
    tok = pl.BlockSpec((bsz, RW_TB, RW_WIDTH), lambda i, j: (0, blk(i), 0))
    tshape = jax.ShapeDtypeStruct((bsz, seq, RW_WIDTH), F32)
    return pl.pallas_call(
        functools.partial(_rw_prep_kernel, n_blocks=n_blocks),
        out_shape=(jax.ShapeDtypeStruct((N_KOPS, seq // SUBLANES, RW_HEAD_DIM, SUBLANES, LANES), F32),
                   jax.ShapeDtypeStruct((seq * RW_VHI, LANES), F32), tshape, tshape),
        grid=(n_blocks + 1, N_KOPS),
        in_specs=[pl.BlockSpec((bsz, RW_TB, RW_IN), lambda i, j: (0, blk(i), 0)),
                  pl.BlockSpec((bsz, SUBLANES, RW_IN),
                               lambda i, j: (0, jnp.maximum(blk(i) * hb - 1, 0), 0)),
                  vec(RW_IN), vec(RW_WIDTH), vec(RW_WIDTH), vec(RW_WIDTH), vec(RW_WIDTH), vec(RW_WIDTH),
                  lowrank, lowrank, lowrank, mat(RW_WIDTH, RW_WIDTH)],
        out_specs=(pl.BlockSpec((1, RW_TB // SUBLANES, RW_HEAD_DIM, SUBLANES, LANES),
                                lambda i, j: (j, jnp.maximum(i - 1, 0), 0, 0, 0)),
                   pl.BlockSpec((RW_TB * RW_VHI, LANES), lambda i, j: (blk(i), 0)), tok, tok),
        scratch_shapes=[pltpu.VMEM((2, N_KOPS + 1, RW_BH * RW_PITCH, RW_TB), F32)],
        compiler_params=_cp("arbitrary", "arbitrary"),
        name="rwkv_prep",
    )(proj3, proj3, mu.reshape(1, RW_IN), w0.reshape(1, -1), a0.reshape(1, -1), k_k.reshape(1, -1),
      k_a.reshape(1, -1), r_k.reshape(1, -1), w2p, a2p, g2p, _head_segments())


def _rw_scan_kernel(r_ref, w_ref, k_ref, an_ref, b_ref, v_ref, y_ref, s_ref):
    @pl.when(pl.program_id(0) == 0)
    def _():
        s_ref[...] = jnp.zeros_like(s_ref)

    nvb = RW_VHI // SUBLANES
    vsl = [slice(vb * SUBLANES, (vb + 1) * SUBLANES) for vb in range(nvb)]
    zeros = lambda: [[jnp.zeros((SUBLANES, LANES), F32) for _ in range(RW_NACC)] for _ in range(nvb)]
    total = lambda acc: [sum(acc[vb][1:], acc[vb][0]) for vb in range(nvb)]

    def body(tb, _):
        def row(ref, kk, ts):
            return jnp.broadcast_to(ref[tb, kk, ts:ts + 1, :], (SUBLANES, LANES))

        zacc = zeros()
        for kk in range(RW_HEAD_DIM):
            an = row(an_ref, kk, 0)
            for vb in range(nvb):
                zacc[vb][kk % RW_NACC] += s_ref[kk, vsl[vb], :] * an
        z = total(zacc)
        for ts in range(SUBLANES):
            t = tb * SUBLANES + ts
            vt = [v_ref[t, vsl[vb], :] for vb in range(nvb)]
            yacc, zacc = zeros(), zeros()
            for kk in range(RW_HEAD_DIM):
                wr, br, kr, rr = row(w_ref, kk, ts), row(b_ref, kk, ts), row(k_ref, kk, ts), row(r_ref, kk, ts)
                an = row(an_ref, kk, ts + 1) if ts + 1 < SUBLANES else None
                for vb in range(nvb):
                    s_new = s_ref[kk, vsl[vb], :] * wr + z[vb] * br + vt[vb] * kr
                    s_ref[kk, vsl[vb], :] = s_new
                    yacc[vb][kk % RW_NACC] += s_new * rr
                    if an is not None:
                        zacc[vb][kk % RW_NACC] += s_new * an
            for vb, y in enumerate(total(yacc)):
                y_ref[t, vsl[vb], :] = y
            z = total(zacc)
        return 0

    lax.fori_loop(0, RW_TT // SUBLANES, body, 0)


def _rw_scan(kops, v4, seq):
    kspec = lambda n: pl.BlockSpec((None, RW_TT // SUBLANES, RW_HEAD_DIM, SUBLANES, LANES),
                                   lambda i: (n, i, 0, 0, 0))
    vspec = pl.BlockSpec((RW_TT, RW_VHI, LANES), lambda i: (i, 0, 0))
    return pl.pallas_call(
        _rw_scan_kernel,
        out_shape=jax.ShapeDtypeStruct((seq, RW_VHI, LANES), F32),
        grid=(seq // RW_TT,),
        in_specs=[kspec(n) for n in range(N_KOPS)] + [vspec],
        out_specs=vspec,
        scratch_shapes=[pltpu.VMEM((RW_HEAD_DIM, RW_VHI, LANES), F32)],
        compiler_params=_cp("arbitrary"),
        name="rwkv_scan",
    )(*([kops] * N_KOPS), v4.reshape(seq, RW_VHI, LANES))


def _rw_post_kernel(y_ref, bonus_ref, gate_ref, lg_ref, lb_ref, seg_ref, o_ref, tr_ref):
    bsz = bonus_ref.shape[0]

    for vh in range(RW_VHI):
        yt = y_ref[pl.ds(vh, RW_TB, stride=RW_VHI), :].T
        for vl in range(RW_VLO):
            tr_ref[pl.ds(vh * RW_VLO + vl, RW_BH, stride=RW_PITCH), :] = yt[vl * RW_BH:(vl + 1) * RW_BH, :]

    seg = seg_ref[...]
    head_mean = lambda a: _segment_sum(a, seg) * (1.0 / RW_HEAD_DIM)

    def per_batch(b, _):
        slabs = []
        for h in range(RW_HEADS):
            base = pl.multiple_of((b * RW_HEADS + h) * RW_PITCH, SUBLANES)
            slabs.append(tr_ref[pl.ds(base, RW_HEAD_DIM), :])
        y = jnp.concatenate(slabs, axis=0).T
        c = y - head_mean(y)
        var = head_mean(c * c)
        y = c * lax.rsqrt(var + RW_LNX_EPS) * lg_ref[...] + lb_ref[...]
        o_ref[b] = (y + bonus_ref[b]) * gate_ref[b]
        return 0
    lax.fori_loop(0, bsz, per_batch, 0)


def _rw_post(y4, bonus, gate, lnx_g, lnx_b):
    bsz, seq, _ = bonus.shape
    tok = pl.BlockSpec((bsz, RW_TB, RW_WIDTH), lambda i: (0, i, 0))
    vec = pl.BlockSpec((1, RW_WIDTH), lambda i: (0, 0))
    return pl.pallas_call(
        _rw_post_kernel,
        out_shape=jax.ShapeDtypeStruct((bsz, seq, RW_WIDTH), F32),
        grid=(seq // RW_TB,),
        in_specs=[pl.BlockSpec((RW_TB * RW_VHI, LANES), lambda i: (i, 0)), tok, tok, vec, vec,
                  pl.BlockSpec((RW_WIDTH, RW_WIDTH), lambda i: (0, 0))],
        out_specs=tok,
        scratch_shapes=[pltpu.VMEM((RW_BH * RW_PITCH, RW_TB), F32)],
        compiler_params=_cp("parallel"),
        name="rwkv_post",
    )(y4.reshape(seq * RW_VHI, LANES), bonus, gate, lnx_g.reshape(1, -1), lnx_b.reshape(1, -1), _head_segments())


S5_TT = 128
S5_PITCH = S5_TT + SUBLANES


def _s5_kernel(u_ref, wb_ref, wc_ref, a_ref, d_ref, gw_ref, gb_ref, og_ref, o_ref, x_ref, carry_ref):
    @pl.when(pl.program_id(0) == 0)
    def _():
        carry_ref[...] = jnp.zeros_like(carry_ref)

    n = S5_CPLX
    nc = n // LANES
    bsz = u_ref.shape[0]

    def project(b, _):
        base = b * S5_PITCH
        bu = jnp.dot(u_ref[b].astype(BF16), wb_ref[...], preferred_element_type=F32)
        for c in range(2 * nc):
            x_ref[c, pl.ds(base, S5_TT), :] = bu[:, c * LANES:(c + 1) * LANES]
        return 0
    for b in range(bsz):
        project(b, 0)

    ar = [jnp.broadcast_to(a_ref[:, c * LANES:(c + 1) * LANES], (bsz, LANES)) for c in range(nc)]
    ai = [jnp.broadcast_to(a_ref[:, n + c * LANES:n + (c + 1) * LANES], (bsz, LANES)) for c in range(nc)]

    def steps(tb, carry):
        cr, ci = list(carry[0]), list(carry[1])
        for ts in range(SUBLANES):
            rows = pl.ds(tb * SUBLANES + ts, bsz, stride=S5_PITCH)
            for c in range(nc):
                cr[c], ci[c] = (ar[c] * cr[c] - ai[c] * ci[c] + x_ref[c, rows, :],
                                ar[c] * ci[c] + ai[c] * cr[c] + x_ref[nc + c, rows, :])
                x_ref[c, rows, :] = cr[c]
                x_ref[nc + c, rows, :] = ci[c]
        return tuple(cr), tuple(ci)

    init = (tuple(carry_ref[c] for c in range(nc)), tuple(carry_ref[nc + c] for c in range(nc)))
    cr, ci = lax.fori_loop(0, S5_TT // SUBLANES, steps, init)
    for c in range(nc):
        carry_ref[c] = cr[c]
        carry_ref[nc + c] = ci[c]

    def readout(b, _):
        base = b * S5_PITCH
        u = u_ref[b]
        x = jnp.concatenate([x_ref[c, pl.ds(base, S5_TT), :] for c in range(2 * nc)], axis=1)
        y = jnp.dot(x.astype(BF16), wc_ref[...], preferred_element_type=F32) + d_ref[...] * u
        z = jax.nn.gelu(y)
        z = z * jax.nn.sigmoid(jnp.dot(z.astype(BF16), gw_ref[...], preferred_element_type=F32) + gb_ref[...])
        o_ref[b] = z * lax.rsqrt(jnp.mean(z * z, axis=-1, keepdims=True) + RMS_EPS) * og_ref[...]
        return 0
    for b in range(bsz):
        readout(b, 0)


def _s5_tables(a_re, a_im, b_re, b_im, c_re, c_im, log_step):
    delta = jnp.exp(log_step.astype(F32))[:, None]
    mag = jnp.exp(delta * a_re)

    def a_pow(n):
        return (mag ** n) * jnp.cos(n * delta * a_im), (mag ** n) * jnp.sin(n * delta * a_im)

    abr, abi = a_pow(1)
    den = a_re * a_re + a_im * a_im
    qr = ((abr - 1.0) * a_re + abi * a_im) / den
    qi = (abi * a_re - (abr - 1.0) * a_im) / den
    bbr = qr[..., None] * b_re - qi[..., None] * b_im
    bbi = qr[..., None] * b_im + qi[..., None] * b_re
    eye = jnp.eye(S5_GROUPS, dtype=F32)
    blk_b = lambda m: jnp.einsum('gph,gk->ghkp', m, eye).reshape(S5_WIDTH, S5_CPLX)
    blk_c = lambda m: jnp.einsum('ghp,gk->gpkh', m, eye).reshape(S5_CPLX, S5_WIDTH)
    wb = jnp.concatenate([blk_b(bbr), blk_b(bbi)], axis=1).astype(BF16)
    wc = jnp.concatenate([blk_c(c_re), -blk_c(c_im)], axis=0).astype(BF16)
    a_row = jnp.concatenate([abr.reshape(1, -1), abi.reshape(1, -1)], axis=1)
    return wb, wc, a_row


def _s5(proj3, tables, d_skip, glu_w, glu_b, out_g):
    wb, wc, a_row = tables
    bsz, seq, _ = proj3.shape
    full = lambda a: pl.BlockSpec(a.shape, lambda i: (0,) * a.ndim)
    vec = pl.BlockSpec((1, S5_WIDTH), lambda i: (0, 0))
    gw = glu_w.astype(BF16)
    return pl.pallas_call(
        _s5_kernel,
        out_shape=jax.ShapeDtypeStruct((bsz, seq, S5_WIDTH), F32),
        grid=(seq // S5_TT,),
        in_specs=[pl.BlockSpec((bsz, S5_TT, S5_WIDTH), lambda i: (0, i, 0)),
                  full(wb), full(wc), full(a_row), vec, full(gw), vec, vec],
        out_specs=pl.BlockSpec((bsz, S5_TT, S5_WIDTH), lambda i: (0, i, 0)),
        scratch_shapes=[pltpu.VMEM((2 * S5_CPLX // LANES, bsz * S5_PITCH, LANES), F32),
                        pltpu.VMEM((2 * S5_CPLX // LANES, bsz, LANES), F32)],
        compiler_params=_cp("arbitrary"),
        name="s5",
    )(proj3, wb, wc, a_row, d_skip.reshape(1, -1), gw, glu_b.reshape(1, -1), out_g.reshape(1, -1))


def _xattn_kernel(hda_ref, hrw_ref, hs5_ref, x_ref, wda_ref, wrw_ref, ws5_ref, g1_ref, b1_ref,
                  kv_ref, wq_ref, wo_ref, g2_ref, b2_ref, o_ref):
    mix = ALPHA * x_ref[...]
    for h_ref, w_ref in ((hda_ref, wda_ref), (hrw_ref, wrw_ref), (hs5_ref, ws5_ref)):
        mix = mix + jnp.dot(h_ref[...].astype(BF16), w_ref[...], preferred_element_type=F32)
    x = _layer_norm_rows(mix, g1_ref[...], b1_ref[...])
    q = jnp.dot(x.astype(BF16), wq_ref[...], preferred_element_type=F32)
    outs = []
    for h in range(MEM_HEADS):
        sl = slice(h * MEM_HEAD_DIM, (h + 1) * MEM_HEAD_DIM)
        kh = kv_ref[:, sl]
        vh = kv_ref[:, D_MODEL + h * MEM_HEAD_DIM:D_MODEL + (h + 1) * MEM_HEAD_DIM]
        s = lax.dot_general(q[:, sl].astype(BF16), kh, (((1,), (1,)), ((), ())),
                            preferred_element_type=F32) * (MEM_HEAD_DIM ** -0.5)
        p = jnp.exp(s - jnp.max(s, axis=-1, keepdims=True))
        p = p / jnp.sum(p, axis=-1, keepdims=True)
        outs.append(jnp.dot(p.astype(BF16), vh, preferred_element_type=F32))
    o = jnp.concatenate(outs, axis=-1)
    h_out = jnp.dot(o.astype(BF16), wo_ref[...], preferred_element_type=F32)
    o_ref[...] = _layer_norm_rows(ALPHA * x + h_out, g2_ref[...], b2_ref[...])


def _mix_out_cross_attention(h_list, w_list, x, g1, b1, kv, wq, wo, g2, b2, seq, tm):
    t, d = x.shape
    n_mem = kv.shape[0] // (t // seq)
    tiles_per_seq = seq // tm
    const = lambda a: pl.BlockSpec(a.shape, lambda i: (0, 0))
    vec = pl.BlockSpec((1, d), lambda i: (0, 0))
    row = lambda n: pl.BlockSpec((tm, n), lambda i: (i, 0))
    return pl.pallas_call(
        _xattn_kernel,
        out_shape=jax.ShapeDtypeStruct((t, d), F32),
        grid=(t // tm,),
        in_specs=[row(h.shape[1]) for h in h_list] + [row(d)] + [const(w) for w in w_list] + [vec, vec]
        + [pl.BlockSpec((n_mem, 2 * d), lambda i: (i // tiles_per_seq, 0)), const(wq), const(wo), vec, vec],
        out_specs=row(d),
        compiler_params=_cp("parallel"),
        name="mix_out_cross_attention",
    )(*h_list, x, *w_list, g1.reshape(1, d), b1.reshape(1, d), kv, wq, wo, g2.reshape(1, d), b2.reshape(1, d))


FFN_TN = 1408


def _ffn_up_kernel(x_ref, halo_ref, wa_ref, wg_ref, cw_ref, cb_ref, o_ref, *, tiles_per_seq):
    i = pl.program_id(1)
    x = x_ref[...].astype(BF16)
    first = jnp.where(i % tiles_per_seq == 0, 0.0, 1.0)
    xh = (halo_ref[...] * first).astype(BF16)
    a8 = jnp.dot(jnp.concatenate([xh, x], axis=0), wa_ref[...], preferred_element_type=F32)
    a0 = a8[SUBLANES:]
    a1 = pltpu.roll(a8, 1, axis=0)[SUBLANES:]
    a2 = pltpu.roll(a8, 2, axis=0)[SUBLANES:]
    conv = cb_ref[...] + cw_ref[0:1, :] * a2 + cw_ref[1:2, :] * a1 + cw_ref[2:3, :] * a0
    g = jnp.dot(x, wg_ref[...], preferred_element_type=F32)
    o_ref[...] = (jax.nn.silu(conv) * g).astype(o_ref.dtype)


def _ffn_up(x, wa, wg, conv_w, conv_b, seq, tm):
    t, d = x.shape
    hb = tm // SUBLANES
    cw = jnp.zeros((SUBLANES, D_FF), F32).at[0:3].set(conv_w)
    wspec = pl.BlockSpec((d, FFN_TN), lambda j, i: (0, j))
    return pl.pallas_call(
        functools.partial(_ffn_up_kernel, tiles_per_seq=seq // tm),
        out_shape=jax.ShapeDtypeStruct((t, D_FF), BF16),
        grid=(D_FF // FFN_TN, t // tm),
        in_specs=[pl.BlockSpec((tm, d), lambda j, i: (i, 0)),
                  pl.BlockSpec((SUBLANES, d), lambda j, i: (jnp.maximum(i * hb - 1, 0), 0)),
                  wspec, wspec,
                  pl.BlockSpec((SUBLANES, FFN_TN), lambda j, i: (0, j)),
                  pl.BlockSpec((1, FFN_TN), lambda j, i: (0, j))],
        out_specs=pl.BlockSpec((tm, FFN_TN), lambda j, i: (i, j)),
        compiler_params=_cp("arbitrary", "parallel"),
        name="ffn_up",
    )(x, x, wa, wg, cw, conv_b.reshape(1, D_FF))


def kernel(x, mem, positions, w_in, da_lam_q1, da_lam_k1, da_lam_q2, da_lam_k2, da_subln_g, rw_mu, rw_w0, rw_w2, rw_a0, rw_a2, rw_g2, rw_k_k, rw_k_a, rw_r_k, rw_lnx_g, rw_lnx_b, s5_a_re, s5_a_im, s5_b_re, s5_b_im, s5_c_re, s5_c_im, s5_d, s5_log_step, s5_glu_w, s5_glu_b, s5_out_g, w_out, ln1_g, ln1_b, ca_wq, ca_wkv, ca_wo, ln2_g, ln2_b, ffn_w_up, ffn_conv_w, ffn_conv_b, ffn_w_down, ln3_g, ln3_b):
    bsz, seq, d = x.shape
    t = bsz * seq
    tm = min(512, seq)
    n_mem = mem.shape[1]
    xf = x.reshape(t, d)
    memf = mem.reshape(bsz * n_mem, d)
    cos_t, sin_t = _rope_tables(positions, tm)

    for l in range(DEPTH):
        lambda_init = 0.8 - 0.6 * math.exp(-0.3 * l)
        wi = w_in[l]
        w_perm = jnp.concatenate([wi[:, :3 * DA_WIDTH], wi[:, 3 * DA_WIDTH + RW_IN:],
                                  wi[:, 3 * DA_WIDTH:3 * DA_WIDTH + RW_IN]], axis=1).astype(BF16)
        qr, kr, vt, p_rw, p_s5 = _in_proj(xf, w_perm, cos_t, sin_t, seq, tm)
        lam = (jnp.exp(jnp.sum(da_lam_q1[l] * da_lam_k1[l])) - jnp.exp(jnp.sum(da_lam_q2[l] * da_lam_k2[l]))
               + lambda_init)
        kops, v4, gate, bonus = _rw_prep(p_rw.reshape(bsz, seq, RW_IN), rw_mu[l], rw_w0[l], rw_w2[l],
                                         rw_a0[l], rw_a2[l], rw_g2[l], rw_k_k[l], rw_k_a[l],
                                         rw_r_k[l].reshape(-1))
        h_da = _attention(qr, kr, vt, lam, da_subln_g[l], lambda_init, bsz, seq)
        y4 = _rw_scan(kops, v4, seq)
        h_rw = _rw_post(y4, bonus, gate, rw_lnx_g[l], rw_lnx_b[l]).reshape(t, RW_WIDTH)

        tables = _s5_tables(s5_a_re[l], s5_a_im[l], s5_b_re[l], s5_b_im[l], s5_c_re[l], s5_c_im[l],
                            s5_log_step[l])
        h_s5 = _s5(p_s5.reshape(bsz, seq, S5_WIDTH), tables, s5_d[l], s5_glu_w[l], s5_glu_b[l],
                   s5_out_g[l]).reshape(t, S5_WIDTH)

        wo = w_out[l].astype(BF16)
        kv = _matmul(memf, ca_wkv[l].astype(BF16), BF16, n_mem, 1024)
        xf = _mix_out_cross_attention(
            [h_da, h_rw, h_s5], [wo[:DA_WIDTH], wo[DA_WIDTH:DA_WIDTH + RW_WIDTH], wo[DA_WIDTH + RW_WIDTH:]],
            xf, ln1_g[l], ln1_b[l], kv, ca_wq[l].astype(BF16), ca_wo[l].astype(BF16), ln2_g[l], ln2_b[l], seq, tm)
        w_up = ffn_w_up[l].astype(BF16)
        hff = _ffn_up(xf, w_up[:, :D_FF], w_up[:, D_FF:], ffn_conv_w[l], ffn_conv_b[l], seq, tm)
        xf = _matmul_ln([hff], [ffn_w_down[l].astype(BF16)], xf, ln3_g[l], ln3_b[l], tm)
    return xf.reshape(bsz, seq, d)
```

```python
import functools
import math

import numpy as np
import jax
import jax.numpy as jnp
from jax import lax
from jax.experimental import pallas as pl
from jax.experimental.pallas import tpu as pltpu

F32 = jnp.float32
BF16 = jnp.bfloat16
HI = lax.Precision.HIGHEST

D_MODEL = 1024
DEPTH = 4
CHUNK = 64
LN_EPS = 1e-5
LOG2E = math.log2(math.e)
RMS_EPS = 1e-6

DA_HEAD_DIM = 64
DA_V_DIM = 128
DA_WIDTH = 512
DA_HEADS = 4
ROPE_THETA = 10000.0

RW_HEAD_DIM = 64
RW_WIDTH = 256
RW_HEADS = 4
RW_DECAY_RANK = 32
RW_AAA_RANK = 32
RW_GATE_RANK = 64
RW_IN = 3 * RW_WIDTH + RW_DECAY_RANK + RW_AAA_RANK + RW_GATE_RANK
RW_DECAY_SCALE = math.exp(-0.5)
RW_LNX_EPS = 64e-5

S5_WIDTH = 256
S5_GROUP_CH = 16
S5_GROUPS = 16
S5_STATE = 64
S5_CPLX = S5_GROUPS * S5_STATE

IN_WIDTH = 3 * DA_WIDTH + RW_IN + S5_WIDTH
MEM_HEADS = 4
MEM_HEAD_DIM = 256
D_FF = 2816
ALPHA = (2.0 * DEPTH) ** 0.25

COL_Q, COL_K, COL_V, COL_S5, COL_RW = 0, 512, 1024, 1536, 1792

VMEM_LIMIT = 48 * 1024 * 1024
LANES = 128
SUBLANES = 8


def _cp(*sem):
    return pltpu.CompilerParams(dimension_semantics=sem, vmem_limit_bytes=VMEM_LIMIT)


def _layer_norm_rows(v, g, b):
    mu = jnp.mean(v, axis=-1, keepdims=True)
    c = v - mu
    var = jnp.mean(c * c, axis=-1, keepdims=True)
    return c * lax.rsqrt(var + LN_EPS) * g + b


def _split3(x):
    p1 = x.astype(BF16)
    r1 = x - p1.astype(F32)
    p2 = r1.astype(BF16)
    p3 = (r1 - p2.astype(F32)).astype(BF16)
    return p1, p2, p3


def _segment_sum(x, seg):
    return sum(jnp.dot(p, seg, preferred_element_type=F32) for p in _split3(x))


def _dot_split(a, w_ref):
    a1, a2, _ = _split3(a)
    d = functools.partial(jnp.dot, preferred_element_type=F32)
    return d(a1, w_ref[0]) + (d(a1, w_ref[1]) + d(a2, w_ref[0]))


def _hi_lo(w):
    hi = w.astype(BF16)
    return jnp.stack([hi, (w - hi.astype(F32)).astype(BF16)])


def _mm_kernel(a_ref, w_ref, o_ref):
    o_ref[...] = jnp.dot(a_ref[...].astype(BF16), w_ref[...],
                         preferred_element_type=F32).astype(o_ref.dtype)


def _matmul(a, w, out_dtype, tm, tn):
    m, k = a.shape
    n = w.shape[1]
    return pl.pallas_call(
        _mm_kernel,
        out_shape=jax.ShapeDtypeStruct((m, n), out_dtype),
        grid=(m // tm, n // tn),
        in_specs=[pl.BlockSpec((tm, k), lambda i, j: (i, 0)),
                  pl.BlockSpec((k, tn), lambda i, j: (0, j))],
        out_specs=pl.BlockSpec((tm, tn), lambda i, j: (i, j)),
        compiler_params=_cp("parallel", "arbitrary"),
        name="matmul",
    )(a, w)


def _mm_ln_kernel(n_in, *refs):
    a_refs = refs[:n_in]
    w_refs = refs[n_in:2 * n_in]
    x_ref, g_ref, b_ref, o_ref = refs[2 * n_in:]
    acc = ALPHA * x_ref[...]
    for a_ref, w_ref in zip(a_refs, w_refs):
        acc = acc + jnp.dot(a_ref[...].astype(BF16), w_ref[...], preferred_element_type=F32)
    o_ref[...] = _layer_norm_rows(acc, g_ref[...], b_ref[...])


def _matmul_ln(a_list, w_list, x, g, b, tm):
    m, d = x.shape
    n_in = len(a_list)
    in_specs = [pl.BlockSpec((tm, a.shape[1]), lambda i: (i, 0)) for a in a_list]
    in_specs += [pl.BlockSpec(w.shape, lambda i: (0, 0)) for w in w_list]
    in_specs += [pl.BlockSpec((tm, d), lambda i: (i, 0)),
                 pl.BlockSpec((1, d), lambda i: (0, 0)),
                 pl.BlockSpec((1, d), lambda i: (0, 0))]
    return pl.pallas_call(
        functools.partial(_mm_ln_kernel, n_in),
        out_shape=jax.ShapeDtypeStruct((m, d), F32),
        grid=(m // tm,),
        in_specs=in_specs,
        out_specs=pl.BlockSpec((tm, d), lambda i: (i, 0)),
        compiler_params=_cp("parallel"),
        name="matmul_ln",
    )(*a_list, *w_list, x, g.reshape(1, d), b.reshape(1, d))


def _rope_table_kernel(pos_ref, freq_ref, sign_ref, cos_ref, sin_ref):
    ang = pos_ref[...] * freq_ref[...]
    cos_ref[...] = jnp.cos(ang)
    sin_ref[...] = jnp.sin(ang) * sign_ref[...]


def _rope_tables(positions, tm):
    t = positions.size
    inv_freq = ROPE_THETA ** (-jnp.arange(0, DA_HEAD_DIM, 2, dtype=F32) / DA_HEAD_DIM)
    freq_row = jnp.tile(inv_freq, 4).reshape(1, LANES)
    sign_row = jnp.tile(jnp.concatenate([-jnp.ones((32,), F32), jnp.ones((32,), F32)]), 2).reshape(1, LANES)
    pos_col = positions.astype(F32).reshape(t, 1)
    row = pl.BlockSpec((1, LANES), lambda i: (0, 0))
    return pl.pallas_call(
        _rope_table_kernel,
        out_shape=(jax.ShapeDtypeStruct((t, LANES), F32),) * 2,
        grid=(t // tm,),
        in_specs=[pl.BlockSpec((tm, 1), lambda i: (i, 0)), row, row],
        out_specs=(pl.BlockSpec((tm, LANES), lambda i: (i, 0)),) * 2,
        compiler_params=_cp("parallel"),
        name="rope_tables",
    )(pos_col, freq_row, sign_row)


def _in_proj_kernel(x_ref, w_ref, cos_ref, sin_ref, qo_ref, ko_ref, vt_ref, rw_ref, s5_ref):
    p = jnp.dot(x_ref[...].astype(BF16), w_ref[...], preferred_element_type=F32)
    cos = cos_ref[...]
    sin = sin_ref[...]
    lane = lax.broadcasted_iota(jnp.int32, cos.shape, 1)
    low = (lane % DA_HEAD_DIM) < (DA_HEAD_DIM // 2)

    def rope(t):
        swapped = jnp.where(low, pltpu.roll(t, LANES - 32, axis=1), pltpu.roll(t, 32, axis=1))
        return t * cos + swapped * sin

    scale = DA_HEAD_DIM ** -0.5 * LOG2E
    for h in range(DA_HEADS):
        sl = slice(h * LANES, (h + 1) * LANES)
        qo_ref[:, sl] = (rope(p[:, COL_Q + h * LANES:COL_Q + (h + 1) * LANES]) * scale).astype(BF16)
        ko_ref[:, sl] = rope(p[:, COL_K + h * LANES:COL_K + (h + 1) * LANES]).astype(BF16)
    vt_ref[...] = p[:, COL_V:COL_V + DA_WIDTH].T.astype(BF16)
    s5_ref[...] = p[:, COL_S5:COL_S5 + S5_WIDTH]
    rw_ref[...] = p[:, COL_RW:COL_RW + RW_IN]


def _in_proj(x, w, cos_t, sin_t, seq, tm):
    t, d = x.shape
    nt = seq // tm
    tab = pl.BlockSpec((tm, LANES), lambda i: (i, 0))
    row = lambda n: pl.BlockSpec((tm, n), lambda i: (i, 0))
    return pl.pallas_call(
        _in_proj_kernel,
        out_shape=(jax.ShapeDtypeStruct((t, DA_WIDTH), BF16), jax.ShapeDtypeStruct((t, DA_WIDTH), BF16),
                   jax.ShapeDtypeStruct((t // seq * DA_WIDTH, seq), BF16),
                   jax.ShapeDtypeStruct((t, RW_IN), F32), jax.ShapeDtypeStruct((t, S5_WIDTH), F32)),
        grid=(t // tm,),
        in_specs=[row(d), pl.BlockSpec(w.shape, lambda i: (0, 0)), tab, tab],
        out_specs=(row(DA_WIDTH), row(DA_WIDTH), pl.BlockSpec((DA_WIDTH, tm), lambda i: (i // nt, i % nt)),
                   row(RW_IN), row(S5_WIDTH)),
        compiler_params=_cp("parallel"),
        name="in_proj",
    )(x, w, cos_t, sin_t)


ATT_TQ = 1024
ATT_TK = 512
ATT_CW = 256
ATT_KB = ATT_TQ // ATT_TK


def _attn_kernel(lam_ref, g_ref, q_ref, k_ref, vt_ref, o_ref, qs_ref, p_ref, acc_ref, *, lambda_init):
    i = pl.program_id(2)
    nq = 2 * ATT_TQ
    q = q_ref[...]
    lane = lax.broadcasted_iota(jnp.int32, q.shape, 1)
    zero = jnp.zeros_like(q)
    qs_ref[:ATT_TQ, :] = jnp.where(lane < DA_HEAD_DIM, q, zero)
    qs_ref[ATT_TQ:, :] = jnp.where(lane >= DA_HEAD_DIM, q, zero)
    p_ref[...] = jnp.zeros_like(p_ref)
    acc_ref[...] = jnp.zeros_like(acc_ref)

    def visible_rows(diag, c):
        if diag is None:
            return ATT_TK
        first_query = (c * ATT_CW) % ATT_TQ
        return max(0, min(ATT_TK, first_query + ATT_CW - diag * ATT_TK))

    def pv_update(c, vt, alpha_prev, rows):
        if rows == 0:
            return
        cs = slice(c * ATT_CW, (c + 1) * ATT_CW)
        pv = jnp.dot(vt[:, :rows], p_ref[c, :rows, :], preferred_element_type=F32)
        acc_ref[c] = alpha_prev[:, cs] * acc_ref[c] + pv

    def block(j, carry, diag, prev_diag):
        m, l, alpha_prev = carry
        kb = k_ref[pl.ds(pl.multiple_of(j * ATT_TK, ATT_TK), ATT_TK), :]
        vt = vt_ref[:, pl.ds(pl.multiple_of(jnp.maximum(j - 1, 0) * ATT_TK, ATT_TK), ATT_TK)]
        ms, ls, alphas = [], [], []
        for c in range(nq // ATT_CW):
            cs = slice(c * ATT_CW, (c + 1) * ATT_CW)
            rows = visible_rows(diag, c)
            if rows:
                s = lax.dot_general(kb[:rows], qs_ref[cs, :], (((1,), (1,)), ((), ())),
                                    preferred_element_type=F32)
            if rows and diag is not None:
                krow = lax.broadcasted_iota(jnp.int32, s.shape, 0) + diag * ATT_TK
                qcol = (lax.broadcasted_iota(jnp.int32, s.shape, 1) + c * ATT_CW) % ATT_TQ
                s = jnp.where(krow // CHUNK <= qcol // CHUNK, s, -jnp.inf)
            pv_update(c, vt, alpha_prev, visible_rows(prev_diag, c))
            if not rows:
                ms.append(m[:, cs])
                ls.append(l[:, cs])
                alphas.append(jnp.ones_like(m[:, cs]))
                continue
            m_new = jnp.maximum(m[:, cs], jnp.max(s, axis=0, keepdims=True))
            alpha = jnp.exp2(m[:, cs] - m_new)
            p = jnp.exp2(s - m_new)
            ls.append(alpha * l[:, cs] + jnp.sum(p, axis=0, keepdims=True))
            p_ref[c, :rows, :] = p.astype(BF16)
            ms.append(m_new)
            alphas.append(alpha)
        return jnp.concatenate(ms, axis=1), jnp.concatenate(ls, axis=1), jnp.concatenate(alphas, axis=1)

    def trip(t, carry, masked):
        for d in range(ATT_KB):
            diag = d if masked else None
            prev_diag = d - 1 if masked and d > 0 else None
            carry = block(ATT_KB * t + d, carry, diag, prev_diag)
        return carry

    init = (jnp.full((1, nq), -jnp.inf, F32), jnp.zeros((1, nq), F32), jnp.ones((1, nq), F32))
    carry = lax.fori_loop(0, i, lambda t, c: trip(t, c, False), init)
    m, l, alpha = trip(i, carry, True)
    last = ATT_KB * i + ATT_KB - 1
    vt = vt_ref[:, pl.ds(pl.multiple_of(last * ATT_TK, ATT_TK), ATT_TK)]
    for c in range(nq // ATT_CW):
        pv_update(c, vt, alpha, visible_rows(ATT_KB - 1, c))
    o = jnp.concatenate([acc_ref[c] for c in range(nq // ATT_CW)], axis=1) / l
    o = o[:, :ATT_TQ] - lam_ref[...] * o[:, ATT_TQ:]
    o = o * lax.rsqrt(jnp.mean(o * o, axis=0, keepdims=True) + RMS_EPS) * g_ref[...]
    o_ref[...] = (o * (1.0 - lambda_init)).T


def _attention(qr, kr, vt, lam, subln_g, lambda_init, bsz, seq):
    nq = seq // ATT_TQ
    col = pl.BlockSpec((LANES, 1), lambda b, h, i: (0, 0))
    qo = pl.BlockSpec((ATT_TQ, LANES), lambda b, h, i: (b * nq + i, h))
    return pl.pallas_call(
        functools.partial(_attn_kernel, lambda_init=lambda_init),
        out_shape=jax.ShapeDtypeStruct((bsz * seq, DA_WIDTH), F32),
        grid=(bsz, DA_HEADS, nq),
        in_specs=[pl.BlockSpec((1, 1), lambda b, h, i: (0, 0)), col, qo,
                  pl.BlockSpec((seq, LANES), lambda b, h, i: (b, h)),
                  pl.BlockSpec((DA_V_DIM, seq), lambda b, h, i: (b * DA_HEADS + h, 0))],
        out_specs=qo,
        scratch_shapes=[pltpu.VMEM((2 * ATT_TQ, LANES), BF16),
                        pltpu.VMEM((2 * ATT_TQ // ATT_CW, ATT_TK, ATT_CW), BF16),
                        pltpu.VMEM((2 * ATT_TQ // ATT_CW, DA_V_DIM, ATT_CW), F32)],
        compiler_params=_cp("parallel", "parallel", "arbitrary"),
        name="diff_attention",
    )(lam.reshape(1, 1), subln_g.reshape(LANES, 1), qr, kr, vt)


RW_VLO = 4
RW_VHI = RW_HEAD_DIM // RW_VLO
RW_BH = LANES // RW_VLO
RW_TB = 128
RW_PITCH = RW_HEAD_DIM + SUBLANES
RW_TT = 64
RW_NACC = 4
N_KOPS = 5


def _rw_prep_kernel(p_ref, halo_ref, mu_ref, w0_ref, a0_ref, kk_ref, ka_ref, rk_ref,
                    w2_ref, a2_ref, g2_ref, seg_ref,
                    kop_ref, v_ref, g_ref, bonus_ref, tr_ref, *, n_blocks):
    i = pl.program_id(0)
    n_op = pl.program_id(1)
    blk = jnp.minimum(i, n_blocks - 1)
    cur = i % 2
    prv = 1 - cur
    first = jnp.where(blk == 0, 0.0, 1.0)
    seg = seg_ref[...]

    @pl.when((i == 0) & (n_op == 0))
    def _():
        tr_ref[1] = jnp.zeros(tr_ref.shape[1:], F32)

    def compute(b):
        p = p_ref[b]
        row = lax.broadcasted_iota(jnp.int32, p.shape, 0)
        last_prev = jnp.broadcast_to(halo_ref[b, SUBLANES - 1:SUBLANES, :], p.shape) * first
        prev = jnp.where(row == 0, last_prev, pltpu.roll(p, 1, axis=0))
        p = p + (prev - p) * mu_ref[...]
        r = p[:, 0:RW_WIDTH]
        k = p[:, RW_WIDTH:2 * RW_WIDTH]
        v = p[:, 2 * RW_WIDTH:3 * RW_WIDTH]
        low = p[:, 3 * RW_WIDTH:RW_IN]
        decay = jnp.exp(-RW_DECAY_SCALE * jax.nn.sigmoid(w0_ref[...] + _dot_split(jnp.tanh(low), w2_ref)))
        a = jax.nn.sigmoid(a0_ref[...] + _dot_split(low, a2_ref))
        g_ref[b] = _dot_split(jax.nn.sigmoid(low), g2_ref)
        kk = k * kk_ref[...]
        kk = kk * lax.rsqrt(jnp.maximum(_segment_sum(kk * kk, seg), 1e-24))
        k = k * (1.0 + (a - 1.0) * ka_ref[...])
        bonus_ref[b] = _segment_sum(r * k * rk_ref[...], seg) * v
        for n, val in enumerate((r, decay, k, -kk, kk * a, v)):
            vt = val.T
            for h in range(RW_HEADS):
                base = pl.multiple_of((b * RW_HEADS + h) * RW_PITCH, SUBLANES)
                tr_ref[cur, n, pl.ds(base, RW_HEAD_DIM), :] = vt[h * RW_HEAD_DIM:(h + 1) * RW_HEAD_DIM, :]

    def v_relayout():
        for vh in range(RW_VHI):
            rows = [tr_ref[cur, N_KOPS, pl.ds(vh * RW_VLO + vl, RW_BH, stride=RW_PITCH), :]
                    for vl in range(RW_VLO)]
            v_ref[pl.ds(vh, RW_TB, stride=RW_VHI), :] = jnp.concatenate(rows, axis=0).T

    def emit():
        for kk_i in range(RW_HEAD_DIM):
            rows = tr_ref[prv, n_op, pl.ds(kk_i, RW_BH, stride=RW_PITCH), :]
            tile = jnp.concatenate([rows] * RW_VLO, axis=0).T
            kop_ref[0, :, kk_i] = tile.reshape(RW_TB // SUBLANES, SUBLANES, LANES)

    per_step = p_ref.shape[0] // (N_KOPS - 1)

    @pl.when(n_op < N_KOPS - 1)
    def _():
        emit()
        for s in range(per_step):
            compute(n_op * per_step + s)

    @pl.when(n_op == N_KOPS - 1)
    def _():
        emit()
        v_relayout()


def _head_segments():
    idx = np.arange(RW_WIDTH) // RW_HEAD_DIM
    return jnp.asarray((idx[:, None] == idx[None, :]).astype(np.float32)).astype(BF16)


def _rw_prep(proj3, mu, w0, w2, a0, a2, g2, k_k, k_a, r_k):
    bsz, seq, _ = proj3.shape
    hb = RW_TB // SUBLANES
    w2p = _hi_lo(jnp.zeros((LANES, RW_WIDTH), F32).at[0:32].set(w2))
    a2p = _hi_lo(jnp.zeros((LANES, RW_WIDTH), F32).at[32:64].set(a2))
    g2p = _hi_lo(jnp.zeros((LANES, RW_WIDTH), F32).at[64:128].set(g2))
    lowrank = pl.BlockSpec((2, LANES, RW_WIDTH), lambda i, j: (0, 0, 0))
    vec = lambda n: pl.BlockSpec((1, n), lambda i, j: (0, 0))
    mat = lambda r, c: pl.BlockSpec((r, c), lambda i, j: (0, 0))
    n_blocks = seq // RW_TB
    assert bsz % (N_KOPS - 1) == 0
    blk = lambda i: jnp.minimum(i, n_blocks - 1)
    tok = pl.BlockSpec((bsz, RW_TB, RW_WIDTH), lambda i, j: (0, blk(i), 0))
    tshape = jax.ShapeDtypeStruct((bsz, seq, RW_WIDTH), F32)
    return pl.pallas_call(
        functools.partial(_rw_prep_kernel, n_blocks=n_blocks),
        out_shape=(jax.ShapeDtypeStruct((N_KOPS, seq // SUBLANES, RW_HEAD_DIM, SUBLANES, LANES), F32),
                   jax.ShapeDtypeStruct((seq * RW_VHI, LANES), F32), tshape, tshape),
        grid=(n_blocks + 1, N_KOPS),
        in_specs=[pl.BlockSpec((bsz, RW_TB, RW_IN), lambda i, j: (0, blk(i), 0)),
                  pl.BlockSpec((bsz, SUBLANES, RW_IN),
                               lambda i, j: (0, jnp.maximum(blk(i) * hb - 1, 0), 0)),
                  vec(RW_IN), vec(RW_WIDTH), vec(RW_WIDTH), vec(RW_WIDTH), vec(RW_WIDTH), vec(RW_WIDTH),
                  lowrank, lowrank, lowrank, mat(RW_WIDTH, RW_WIDTH)],
        out_specs=(pl.BlockSpec((1, RW_TB // SUBLANES, RW_HEAD_DIM, SUBLANES, LANES),
                                lambda i, j: (j, jnp.maximum(i - 1, 0), 0, 0, 0)),
                   pl.BlockSpec((RW_TB * RW_VHI, LANES), lambda i, j: (blk(i), 0)), tok, tok),
        scratch_shapes=[pltpu.VMEM((2, N_KOPS + 1, RW_BH * RW_PITCH, RW_TB), F32)],
        compiler_params=_cp("arbitrary", "arbitrary"),
        name="rwkv_prep",
    )(proj3, proj3, mu.reshape(1, RW_IN), w0.reshape(1, -1), a0.reshape(1, -1), k_k.reshape(1, -1),
      k_a.reshape(1, -1), r_k.reshape(1, -1), w2p, a2p, g2p, _head_segments())


def _rw_scan_kernel(r_ref, w_ref, k_ref, an_ref, b_ref, v_ref, y_ref, s_ref):
    @pl.when(pl.program_id(0) == 0)
    def _():
        s_ref[...] = jnp.zeros_like(s_ref)

    nvb = RW_VHI // SUBLANES
    vsl = [slice(vb * SUBLANES, (vb + 1) * SUBLANES) for vb in range(nvb)]
    zeros = lambda: [[jnp.zeros((SUBLANES, LANES), F32) for _ in range(RW_NACC)] for _ in range(nvb)]
    total = lambda acc: [sum(acc[vb][1:], acc[vb][0]) for vb in range(nvb)]

    def body(tb, _):
        def row(ref, kk, ts):
            return jnp.broadcast_to(ref[tb, kk, ts:ts + 1, :], (SUBLANES, LANES))

        zacc = zeros()
        for kk in range(RW_HEAD_DIM):
            an = row(an_ref, kk, 0)
            for vb in range(nvb):
                zacc[vb][kk % RW_NACC] += s_ref[kk, vsl[vb], :] * an
        z = total(zacc)
        for ts in range(SUBLANES):
            t = tb * SUBLANES + ts
            vt = [v_ref[t, vsl[vb], :] for vb in range(nvb)]
            yacc, zacc = zeros(), zeros()
            for kk in range(RW_HEAD_DIM):
                wr, br, kr, rr = row(w_ref, kk, ts), row(b_ref, kk, ts), row(k_ref, kk, ts), row(r_ref, kk, ts)
                an = row(an_ref, kk, ts + 1) if ts + 1 < SUBLANES else None
                for vb in range(nvb):
                    s_new = s_ref[kk, vsl[vb], :] * wr + z[vb] * br + vt[vb] * kr
                    s_ref[kk, vsl[vb], :] = s_new
                    yacc[vb][kk % RW_NACC] += s_new * rr
                    if an is not None:
                        zacc[vb][kk % RW_NACC] += s_new * an
            for vb, y in enumerate(total(yacc)):
                y_ref[t, vsl[vb], :] = y
            z = total(zacc)
        return 0

    lax.fori_loop(0, RW_TT // SUBLANES, body, 0)


def _rw_scan(kops, v4, seq):
    kspec = lambda n: pl.BlockSpec((None, RW_TT // SUBLANES, RW_HEAD_DIM, SUBLANES, LANES),
                                   lambda i: (n, i, 0, 0, 0))
    vspec = pl.BlockSpec((RW_TT, RW_VHI, LANES), lambda i: (i, 0, 0))
    return pl.pallas_call(
        _rw_scan_kernel,
        out_shape=jax.ShapeDtypeStruct((seq, RW_VHI, LANES), F32),
        grid=(seq // RW_TT,),
        in_specs=[kspec(n) for n in range(N_KOPS)] + [vspec],
        out_specs=vspec,
        scratch_shapes=[pltpu.VMEM((RW_HEAD_DIM, RW_VHI, LANES), F32)],
        compiler_params=_cp("arbitrary"),
        name="rwkv_scan",
    )(*([kops] * N_KOPS), v4.reshape(seq, RW_VHI, LANES))


def _rw_post_kernel(y_ref, bonus_ref, gate_ref, lg_ref, lb_ref, seg_ref, o_ref, tr_ref):
    bsz = bonus_ref.shape[0]

    for vh in range(RW_VHI):
        yt = y_ref[pl.ds(vh, RW_TB, stride=RW_VHI), :].T
        for vl in range(RW_VLO):
            tr_ref[pl.ds(vh * RW_VLO + vl, RW_BH, stride=RW_PITCH), :] = yt[vl * RW_BH:(vl + 1) * RW_BH, :]

    seg = seg_ref[...]
    head_mean = lambda a: _segment_sum(a, seg) * (1.0 / RW_HEAD_DIM)

    for b in range(bsz):
        slabs = [tr_ref[pl.ds((b * RW_HEADS + h) * RW_PITCH, RW_HEAD_DIM), :] for h in range(RW_HEADS)]
        y = jnp.concatenate(slabs, axis=0).T
        c = y - head_mean(y)
        var = head_mean(c * c)
        y = c * lax.rsqrt(var + RW_LNX_EPS) * lg_ref[...] + lb_ref[...]
        o_ref[b] = (y + bonus_ref[b]) * gate_ref[b]


def _rw_post(y4, bonus, gate, lnx_g, lnx_b):
    bsz, seq, _ = bonus.shape
    tok = pl.BlockSpec((bsz, RW_TB, RW_WIDTH), lambda i: (0, i, 0))
    vec = pl.BlockSpec((1, RW_WIDTH), lambda i: (0, 0))
    return pl.pallas_call(
        _rw_post_kernel,
        out_shape=jax.ShapeDtypeStruct((bsz, seq, RW_WIDTH), F32),
        grid=(seq // RW_TB,),
        in_specs=[pl.BlockSpec((RW_TB * RW_VHI, LANES), lambda i: (i, 0)), tok, tok, vec, vec,
                  pl.BlockSpec((RW_WIDTH, RW_WIDTH), lambda i: (0, 0))],
        out_specs=tok,
        scratch_shapes=[pltpu.VMEM((RW_BH * RW_PITCH, RW_TB), F32)],
        compiler_params=_cp("parallel"),
        name="rwkv_post",
    )(y4.reshape(seq * RW_VHI, LANES), bonus, gate, lnx_g.reshape(1, -1), lnx_b.reshape(1, -1), _head_segments())


S5_TT = 128
S5_PITCH = S5_TT + SUBLANES


def _s5_kernel(u_ref, wb_ref, wc_ref, a_ref, d_ref, gw_ref, gb_ref, og_ref, o_ref, x_ref, carry_ref):
    @pl.when(pl.program_id(0) == 0)
    def _():
        carry_ref[...] = jnp.zeros_like(carry_ref)

    n = S5_CPLX
    nc = n // LANES
    bsz = u_ref.shape[0]

    def project(b, _):
        base = b * S5_PITCH
        bu = jnp.dot(u_ref[b].astype(BF16), wb_ref[...], preferred_element_type=F32)
        for c in range(2 * nc):
            x_ref[c, pl.ds(base, S5_TT), :] = bu[:, c * LANES:(c + 1) * LANES]
        return 0
    for b in range(bsz):
        project(b, 0)

    ar = [jnp.broadcast_to(a_ref[:, c * LANES:(c + 1) * LANES], (bsz, LANES)) for c in range(nc)]
    ai = [jnp.broadcast_to(a_ref[:, n + c * LANES:n + (c + 1) * LANES], (bsz, LANES)) for c in range(nc)]

    def steps(tb, carry):
        cr, ci = list(carry[0]), list(carry[1])
        for ts in range(SUBLANES):
            rows = pl.ds(tb * SUBLANES + ts, bsz, stride=S5_PITCH)
            for c in range(nc):
                cr[c], ci[c] = (ar[c] * cr[c] - ai[c] * ci[c] + x_ref[c, rows, :],
                                ar[c] * ci[c] + ai[c] * cr[c] + x_ref[nc + c, rows, :])
                x_ref[c, rows, :] = cr[c]
                x_ref[nc + c, rows, :] = ci[c]
        return tuple(cr), tuple(ci)

    init = (tuple(carry_ref[c] for c in range(nc)), tuple(carry_ref[nc + c] for c in range(nc)))
    cr, ci = lax.fori_loop(0, S5_TT // SUBLANES, steps, init)
    for c in range(nc):
        carry_ref[c] = cr[c]
        carry_ref[nc + c] = ci[c]

    def readout(b, _):
        base = b * S5_PITCH
        u = u_ref[b]
        x = jnp.concatenate([x_ref[c, pl.ds(base, S5_TT), :] for c in range(2 * nc)], axis=1)
        y = jnp.dot(x.astype(BF16), wc_ref[...], preferred_element_type=F32) + d_ref[...] * u
        z = jax.nn.gelu(y)
        z = z * jax.nn.sigmoid(jnp.dot(z.astype(BF16), gw_ref[...], preferred_element_type=F32) + gb_ref[...])
        o_ref[b] = z * lax.rsqrt(jnp.mean(z * z, axis=-1, keepdims=True) + RMS_EPS) * og_ref[...]
        return 0
    for b in range(bsz):
        readout(b, 0)


def _s5_tables(a_re, a_im, b_re, b_im, c_re, c_im, log_step):
    delta = jnp.exp(log_step.astype(F32))[:, None]
    mag = jnp.exp(delta * a_re)

    def a_pow(n):
        return (mag ** n) * jnp.cos(n * delta * a_im), (mag ** n) * jnp.sin(n * delta * a_im)

    abr, abi = a_pow(1)
    den = a_re * a_re + a_im * a_im
    qr = ((abr - 1.0) * a_re + abi * a_im) / den
    qi = (abi * a_re - (abr - 1.0) * a_im) / den
    bbr = qr[..., None] * b_re - qi[..., None] * b_im
    bbi = qr[..., None] * b_im + qi[..., None] * b_re
    eye = jnp.eye(S5_GROUPS, dtype=F32)
    blk_b = lambda m: jnp.einsum('gph,gk->ghkp', m, eye).reshape(S5_WIDTH, S5_CPLX)
    blk_c = lambda m: jnp.einsum('ghp,gk->gpkh', m, eye).reshape(S5_CPLX, S5_WIDTH)
    wb = jnp.concatenate([blk_b(bbr), blk_b(bbi)], axis=1).astype(BF16)
    wc = jnp.concatenate([blk_c(c_re), -blk_c(c_im)], axis=0).astype(BF16)
    a_row = jnp.concatenate([abr.reshape(1, -1), abi.reshape(1, -1)], axis=1)
    return wb, wc, a_row


def _s5(proj3, tables, d_skip, glu_w, glu_b, out_g):
    wb, wc, a_row = tables
    bsz, seq, _ = proj3.shape
    full = lambda a: pl.BlockSpec(a.shape, lambda i: (0,) * a.ndim)
    vec = pl.BlockSpec((1, S5_WIDTH), lambda i: (0, 0))
    gw = glu_w.astype(BF16)
    return pl.pallas_call(
        _s5_kernel,
        out_shape=jax.ShapeDtypeStruct((bsz, seq, S5_WIDTH), F32),
        grid=(seq // S5_TT,),
        in_specs=[pl.BlockSpec((bsz, S5_TT, S5_WIDTH), lambda i: (0, i, 0)),
                  full(wb), full(wc), full(a_row), vec, full(gw), vec, vec],
        out_specs=pl.BlockSpec((bsz, S5_TT, S5_WIDTH), lambda i: (0, i, 0)),
        scratch_shapes=[pltpu.VMEM((2 * S5_CPLX // LANES, bsz * S5_PITCH, LANES), F32),
                        pltpu.VMEM((2 * S5_CPLX // LANES, bsz, LANES), F32)],
        compiler_params=_cp("arbitrary"),
        name="s5",
    )(proj3, wb, wc, a_row, d_skip.reshape(1, -1), gw, glu_b.reshape(1, -1), out_g.reshape(1, -1))


def _xattn_kernel(hda_ref, hrw_ref, hs5_ref, x_ref, wda_ref, wrw_ref, ws5_ref, g1_ref, b1_ref,
                  kv_ref, wq_ref, wo_ref, g2_ref, b2_ref, o_ref):
    mix = ALPHA * x_ref[...]
    for h_ref, w_ref in ((hda_ref, wda_ref), (hrw_ref, wrw_ref), (hs5_ref, ws5_ref)):
        mix = mix + jnp.dot(h_ref[...].astype(BF16), w_ref[...], preferred_element_type=F32)
    x = _layer_norm_rows(mix, g1_ref[...], b1_ref[...])
    q = jnp.dot(x.astype(BF16), wq_ref[...], preferred_element_type=F32)
    outs = []
    for h in range(MEM_HEADS):
        sl = slice(h * MEM_HEAD_DIM, (h + 1) * MEM_HEAD_DIM)
        kh = kv_ref[:, sl]
        vh = kv_ref[:, D_MODEL + h * MEM_HEAD_DIM:D_MODEL + (h + 1) * MEM_HEAD_DIM]
        s = lax.dot_general(q[:, sl].astype(BF16), kh, (((1,), (1,)), ((), ())),
                            preferred_element_type=F32) * (MEM_HEAD_DIM ** -0.5)
        p = jnp.exp(s - jnp.max(s, axis=-1, keepdims=True))
        p = p / jnp.sum(p, axis=-1, keepdims=True)
        outs.append(jnp.dot(p.astype(BF16), vh, preferred_element_type=F32))
    o = jnp.concatenate(outs, axis=-1)
    h_out = jnp.dot(o.astype(BF16), wo_ref[...], preferred_element_type=F32)
    o_ref[...] = _layer_norm_rows(ALPHA * x + h_out, g2_ref[...], b2_ref[...])


def _mix_out_cross_attention(h_list, w_list, x, g1, b1, kv, wq, wo, g2, b2, seq, tm):
    t, d = x.shape
    n_mem = kv.shape[0] // (t // seq)
    tiles_per_seq = seq // tm
    const = lambda a: pl.BlockSpec(a.shape, lambda i: (0, 0))
    vec = pl.BlockSpec((1, d), lambda i: (0, 0))
    row = lambda n: pl.BlockSpec((tm, n), lambda i: (i, 0))
    return pl.pallas_call(
        _xattn_kernel,
        out_shape=jax.ShapeDtypeStruct((t, d), F32),
        grid=(t // tm,),
        in_specs=[row(h.shape[1]) for h in h_list] + [row(d)] + [const(w) for w in w_list] + [vec, vec]
        + [pl.BlockSpec((n_mem, 2 * d), lambda i: (i // tiles_per_seq, 0)), const(wq), const(wo), vec, vec],
        out_specs=row(d),
        compiler_params=_cp("parallel"),
        name="mix_out_cross_attention",
    )(*h_list, x, *w_list, g1.reshape(1, d), b1.reshape(1, d), kv, wq, wo, g2.reshape(1, d), b2.reshape(1, d))


FFN_TN = 1408


def _ffn_up_kernel(x_ref, halo_ref, wa_ref, wg_ref, cw_ref, cb_ref, o_ref, *, tiles_per_seq):
    i = pl.program_id(1)
    x = x_ref[...].astype(BF16)
    first = jnp.where(i % tiles_per_seq == 0, 0.0, 1.0)
    xh = (halo_ref[...] * first).astype(BF16)
    a8 = jnp.dot(jnp.concatenate([xh, x], axis=0), wa_ref[...], preferred_element_type=F32)
    a0 = a8[SUBLANES:]
    a1 = pltpu.roll(a8, 1, axis=0)[SUBLANES:]
    a2 = pltpu.roll(a8, 2, axis=0)[SUBLANES:]
    conv = cb_ref[...] + cw_ref[0:1, :] * a2 + cw_ref[1:2, :] * a1 + cw_ref[2:3, :] * a0
    g = jnp.dot(x, wg_ref[...], preferred_element_type=F32)
    o_ref[...] = (jax.nn.silu(conv) * g).astype(o_ref.dtype)


def _ffn_up(x, wa, wg, conv_w, conv_b, seq, tm):
    t, d = x.shape
    hb = tm // SUBLANES
    cw = jnp.zeros((SUBLANES, D_FF), F32).at[0:3].set(conv_w)
    wspec = pl.BlockSpec((d, FFN_TN), lambda j, i: (0, j))
    return pl.pallas_call(
        functools.partial(_ffn_up_kernel, tiles_per_seq=seq // tm),
        out_shape=jax.ShapeDtypeStruct((t, D_FF), BF16),
        grid=(D_FF // FFN_TN, t // tm),
        in_specs=[pl.BlockSpec((tm, d), lambda j, i: (i, 0)),
                  pl.BlockSpec((SUBLANES, d), lambda j, i: (jnp.maximum(i * hb - 1, 0), 0)),
                  wspec, wspec,
                  pl.BlockSpec((SUBLANES, FFN_TN), lambda j, i: (0, j)),
                  pl.BlockSpec((1, FFN_TN), lambda j, i: (0, j))],
        out_specs=pl.BlockSpec((tm, FFN_TN), lambda j, i: (i, j)),
        compiler_params=_cp("arbitrary", "parallel"),
        name="ffn_up",
    )(x, x, wa, wg, cw, conv_b.reshape(1, D_FF))


def kernel(x, mem, positions, w_in, da_lam_q1, da_lam_k1, da_lam_q2, da_lam_k2, da_subln_g, rw_mu, rw_w0, rw_w2, rw_a0, rw_a2, rw_g2, rw_k_k, rw_k_a, rw_r_k, rw_lnx_g, rw_lnx_b, s5_a_re, s5_a_im, s5_b_re, s5_b_im, s5_c_re, s5_c_im, s5_d, s5_log_step, s5_glu_w, s5_glu_b, s5_out_g, w_out, ln1_g, ln1_b, ca_wq, ca_wkv, ca_wo, ln2_g, ln2_b, ffn_w_up, ffn_conv_w, ffn_conv_b, ffn_w_down, ln3_g, ln3_b):
    bsz, seq, d = x.shape
    t = bsz * seq
    tm = min(512, seq)
    n_mem = mem.shape[1]
    xf = x.reshape(t, d)
    memf = mem.reshape(bsz * n_mem, d)
    cos_t, sin_t = _rope_tables(positions, tm)

    for l in range(DEPTH):
        lambda_init = 0.8 - 0.6 * math.exp(-0.3 * l)
        wi = w_in[l]
        w_perm = jnp.concatenate([wi[:, :3 * DA_WIDTH], wi[:, 3 * DA_WIDTH + RW_IN:],
                                  wi[:, 3 * DA_WIDTH:3 * DA_WIDTH + RW_IN]], axis=1).astype(BF16)
        qr, kr, vt, p_rw, p_s5 = _in_proj(xf, w_perm, cos_t, sin_t, seq, tm)
        lam = (jnp.exp(jnp.sum(da_lam_q1[l] * da_lam_k1[l])) - jnp.exp(jnp.sum(da_lam_q2[l] * da_lam_k2[l]))
               + lambda_init)
        kops, v4, gate, bonus = _rw_prep(p_rw.reshape(bsz, seq, RW_IN), rw_mu[l], rw_w0[l], rw_w2[l],
                                         rw_a0[l], rw_a2[l], rw_g2[l], rw_k_k[l], rw_k_a[l],
                                         rw_r_k[l].reshape(-1))
        h_da = _attention(qr, kr, vt, lam, da_subln_g[l], lambda_init, bsz, seq)
        y4 = _rw_scan(kops, v4, seq)
        h_rw = _rw_post(y4, bonus, gate, rw_lnx_g[l], rw_lnx_b[l]).reshape(t, RW_WIDTH)

        tables = _s5_tables(s5_a_re[l], s5_a_im[l], s5_b_re[l], s5_b_im[l], s5_c_re[l], s5_c_im[l],
                            s5_log_step[l])
        h_s5 = _s5(p_s5.reshape(bsz, seq, S5_WIDTH), tables, s5_d[l], s5_glu_w[l], s5_glu_b[l],
                   s5_out_g[l]).reshape(t, S5_WIDTH)

        wo = w_out[l].astype(BF16)
        kv = _matmul(memf, ca_wkv[l].astype(BF16), BF16, n_mem, 1024)
        xf = _mix_out_cross_attention(
            [h_da, h_rw, h_s5], [wo[:DA_WIDTH], wo[DA_WIDTH:DA_WIDTH + RW_WIDTH], wo[DA_WIDTH + RW_WIDTH:]],
            xf, ln1_g[l], ln1_b[l], kv, ca_wq[l].astype(BF16), ca_wo[l].astype(BF16), ln2_g[l], ln2_b[l], seq, tm)
        w_up = ffn_w_up[l].astype(BF16)
        hff = _ffn_up(xf, w_up[:, :D_FF], w_up[:, D_FF:], ffn_conv_w[l], ffn_conv_b[l], seq, tm)
        xf = _matmul_ln([hff], [ffn_w_down[l].astype(BF16)], xf, ln3_g[l], ln3_b[l], tm)
    return xf.reshape(bsz, seq, d)
```

```python
import functools
import math

import numpy as np
import jax
import jax.numpy as jnp
from jax import lax
from jax.experimental import pallas as pl
from jax.experimental.pallas import tpu as pltpu

F32 = jnp.float32
BF16 = jnp.bfloat16
HI = lax.Precision.HIGHEST

D_MODEL = 1024
DEPTH = 4
CHUNK = 64
LN_EPS = 1e-5
LOG2E = math.log2(math.e)
RMS_EPS = 1e-6

DA_HEAD_DIM = 64
DA_V_DIM = 128
DA_WIDTH = 512
DA_HEADS = 4
ROPE_THETA = 10000.0

RW_HEAD_DIM = 64
RW_WIDTH = 256
RW_HEADS = 4
RW_DECAY_RANK = 32
RW_AAA_RANK = 32
RW_GATE_RANK = 64
RW_IN = 3 * RW_WIDTH + RW_DECAY_RANK + RW_AAA_RANK + RW_GATE_RANK
RW_DECAY_SCALE = math.exp(-0.5)
RW_LNX_EPS = 64e-5

S5_WIDTH = 256
S5_GROUP_CH = 16
S5_GROUPS = 16
S5_STATE = 64
S5_CPLX = S5_GROUPS * S5_STATE

IN_WIDTH = 3 * DA_WIDTH + RW_IN + S5_WIDTH
MEM_HEADS = 4
MEM_HEAD_DIM = 256
D_FF = 2816
ALPHA = (2.0 * DEPTH) ** 0.25

COL_Q, COL_K, COL_V, COL_S5, COL_RW = 0, 512, 1024, 1536, 1792

VMEM_LIMIT = 48 * 1024 * 1024
LANES = 128
SUBLANES = 8


def _cp(*sem):
    return pltpu.CompilerParams(dimension_semantics=sem, vmem_limit_bytes=VMEM_LIMIT)


def _layer_norm_rows(v, g, b):
    mu = jnp.mean(v, axis=-1, keepdims=True)
    c = v - mu
    var = jnp.mean(c * c, axis=-1, keepdims=True)
    return c * lax.rsqrt(var + LN_EPS) * g + b


def _split3(x):
    p1 = x.astype(BF16)
    r1 = x - p1.astype(F32)
    p2 = r1.astype(BF16)
    p3 = (r1 - p2.astype(F32)).astype(BF16)
    return p1, p2, p3


def _segment_sum(x, seg):
    return sum(jnp.dot(p, seg, preferred_element_type=F32) for p in _split3(x))


def _dot_split(a, w_ref):
    a1, a2, _ = _split3(a)
    d = functools.partial(jnp.dot, preferred_element_type=F32)
    return d(a1, w_ref[0]) + (d(a1, w_ref[1]) + d(a2, w_ref[0]))


def _hi_lo(w):
    hi = w.astype(BF16)
    return jnp.stack([hi, (w - hi.astype(F32)).astype(BF16)])


def _mm_kernel(a_ref, w_ref, o_ref):
    o_ref[...] = jnp.dot(a_ref[...].astype(BF16), w_ref[...],
                         preferred_element_type=F32).astype(o_ref.dtype)


def _matmul(a, w, out_dtype, tm, tn):
    m, k = a.shape
    n = w.shape[1]
    return pl.pallas_call(
        _mm_kernel,
        out_shape=jax.ShapeDtypeStruct((m, n), out_dtype),
        grid=(m // tm, n // tn),
        in_specs=[pl.BlockSpec((tm, k), lambda i, j: (i, 0)),
                  pl.BlockSpec((k, tn), lambda i, j: (0, j))],
        out_specs=pl.BlockSpec((tm, tn), lambda i, j: (i, j)),
        compiler_params=_cp("parallel", "arbitrary"),
        name="matmul",
    )(a, w)


def _mm_ln_kernel(n_in, *refs):
    a_refs = refs[:n_in]
    w_refs = refs[n_in:2 * n_in]
    x_ref, g_ref, b_ref, o_ref = refs[2 * n_in:]
    acc = ALPHA * x_ref[...]
    for a_ref, w_ref in zip(a_refs, w_refs):
        acc = acc + jnp.dot(a_ref[...].astype(BF16), w_ref[...], preferred_element_type=F32)
    o_ref[...] = _layer_norm_rows(acc, g_ref[...], b_ref[...])


def _matmul_ln(a_list, w_list, x, g, b, tm):
    m, d = x.shape
    n_in = len(a_list)
    in_specs = [pl.BlockSpec((tm, a.shape[1]), lambda i: (i, 0)) for a in a_list]
    in_specs += [pl.BlockSpec(w.shape, lambda i: (0, 0)) for w in w_list]
    in_specs += [pl.BlockSpec((tm, d), lambda i: (i, 0)),
                 pl.BlockSpec((1, d), lambda i: (0, 0)),
                 pl.BlockSpec((1, d), lambda i: (0, 0))]
    return pl.pallas_call(
        functools.partial(_mm_ln_kernel, n_in),
        out_shape=jax.ShapeDtypeStruct((m, d), F32),
        grid=(m // tm,),
        in_specs=in_specs,
        out_specs=pl.BlockSpec((tm, d), lambda i: (i, 0)),
        compiler_params=_cp("parallel"),
        name="matmul_ln",
    )(*a_list, *w_list, x, g.reshape(1, d), b.reshape(1, d))


def _rope_table_kernel(pos_ref, freq_ref, sign_ref, cos_ref, sin_ref):
    ang = pos_ref[...] * freq_ref[...]
    cos_ref[...] = jnp.cos(ang)
    sin_ref[...] = jnp.sin(ang) * sign_ref[...]


def _rope_tables(positions, tm):
    t = positions.size
    inv_freq = ROPE_THETA ** (-jnp.arange(0, DA_HEAD_DIM, 2, dtype=F32) / DA_HEAD_DIM)
    freq_row = jnp.tile(inv_freq, 4).reshape(1, LANES)
    sign_row = jnp.tile(jnp.concatenate([-jnp.ones((32,), F32), jnp.ones((32,), F32)]), 2).reshape(1, LANES)
    pos_col = positions.astype(F32).reshape(t, 1)
    row = pl.BlockSpec((1, LANES), lambda i: (0, 0))
    return pl.pallas_call(
        _rope_table_kernel,
        out_shape=(jax.ShapeDtypeStruct((t, LANES), F32),) * 2,
        grid=(t // tm,),
        in_specs=[pl.BlockSpec((tm, 1), lambda i: (i, 0)), row, row],
        out_specs=(pl.BlockSpec((tm, LANES), lambda i: (i, 0)),) * 2,
        compiler_params=_cp("parallel"),
        name="rope_tables",
    )(pos_col, freq_row, sign_row)


def _in_proj_kernel(x_ref, w_ref, cos_ref, sin_ref, qo_ref, ko_ref, vt_ref, rw_ref, s5_ref):
    p = jnp.dot(x_ref[...].astype(BF16), w_ref[...], preferred_element_type=F32)
    cos = cos_ref[...]
    sin = sin_ref[...]
    lane = lax.broadcasted_iota(jnp.int32, cos.shape, 1)
    low = (lane % DA_HEAD_DIM) < (DA_HEAD_DIM // 2)

    def rope(t):
        swapped = jnp.where(low, pltpu.roll(t, LANES - 32, axis=1), pltpu.roll(t, 32, axis=1))
        return t * cos + swapped * sin

    scale = DA_HEAD_DIM ** -0.5 * LOG2E
    for h in range(DA_HEADS):
        sl = slice(h * LANES, (h + 1) * LANES)
        qo_ref[:, sl] = (rope(p[:, COL_Q + h * LANES:COL_Q + (h + 1) * LANES]) * scale).astype(BF16)
        ko_ref[:, sl] = rope(p[:, COL_K + h * LANES:COL_K + (h + 1) * LANES]).astype(BF16)
    vt_ref[:, 0] = p[:, COL_V:COL_V + DA_WIDTH].T.astype(BF16).reshape(DA_HEADS, DA_V_DIM, p.shape[0])
    s5_ref[...] = p[:, COL_S5:COL_S5 + S5_WIDTH]
    rw_ref[...] = p[:, COL_RW:COL_RW + RW_IN]


def _in_proj(x, w, cos_t, sin_t, seq, tm):
    t, d = x.shape
    nt = seq // tm
    tab = pl.BlockSpec((tm, LANES), lambda i: (i, 0))
    row = lambda n: pl.BlockSpec((tm, n), lambda i: (i, 0))
    return pl.pallas_call(
        _in_proj_kernel,
        out_shape=(jax.ShapeDtypeStruct((t, DA_WIDTH), BF16), jax.ShapeDtypeStruct((t, DA_WIDTH), BF16),
                   jax.ShapeDtypeStruct((t // seq * DA_HEADS, nt, DA_V_DIM, tm), BF16),
                   jax.ShapeDtypeStruct((t, RW_IN), F32), jax.ShapeDtypeStruct((t, S5_WIDTH), F32)),
        grid=(t // tm,),
        in_specs=[row(d), pl.BlockSpec(w.shape, lambda i: (0, 0)), tab, tab],
        out_specs=(row(DA_WIDTH), row(DA_WIDTH),
                   pl.BlockSpec((DA_HEADS, 1, DA_V_DIM, tm), lambda i: (i // nt, i % nt, 0, 0)),
                   row(RW_IN), row(S5_WIDTH)),
        compiler_params=_cp("parallel"),
        name="in_proj",
    )(x, w, cos_t, sin_t)


ATT_TQ = 1024
ATT_TK = 512
ATT_CW = 256
ATT_KB = ATT_TQ // ATT_TK


def _attn_kernel(lam_ref, g_ref, q_ref, k_ref, vt_ref, o_ref, qs_ref, p_ref, acc_ref, *, lambda_init):
    i = pl.program_id(2)
    nq = 2 * ATT_TQ
    q = q_ref[...]
    lane = lax.broadcasted_iota(jnp.int32, q.shape, 1)
    zero = jnp.zeros_like(q)
    qs_ref[:ATT_TQ, :] = jnp.where(lane < DA_HEAD_DIM, q, zero)
    qs_ref[ATT_TQ:, :] = jnp.where(lane >= DA_HEAD_DIM, q, zero)
    p_ref[...] = jnp.zeros_like(p_ref)
    acc_ref[...] = jnp.zeros_like(acc_ref)

    def visible_rows(diag, c):
        if diag is None:
            return ATT_TK
        first_query = (c * ATT_CW) % ATT_TQ
        return max(0, min(ATT_TK, first_query + ATT_CW - diag * ATT_TK))

    def pv_update(c, vt, alpha_prev, rows):
        if rows == 0:
            return
        cs = slice(c * ATT_CW, (c + 1) * ATT_CW)
        pv = jnp.dot(vt[:, :rows], p_ref[c, :rows, :], preferred_element_type=F32)
        acc_ref[c] = alpha_prev[:, cs] * acc_ref[c] + pv

    def block(j, carry, diag, prev_diag):
        m, l, alpha_prev = carry
        kb = k_ref[pl.ds(pl.multiple_of(j * ATT_TK, ATT_TK), ATT_TK), :]
        vt = vt_ref[jnp.maximum(j - 1, 0)]
        ms, ls, alphas = [], [], []
        for c in range(nq // ATT_CW):
            cs = slice(c * ATT_CW, (c + 1) * ATT_CW)
            rows = visible_rows(diag, c)
            if rows:
                s = lax.dot_general(kb[:rows], qs_ref[cs, :], (((1,), (1,)), ((), ())),
                                    preferred_element_type=F32)
            if rows and diag is not None:
                krow = lax.broadcasted_iota(jnp.int32, s.shape, 0) + diag * ATT_TK
                qcol = (lax.broadcasted_iota(jnp.int32, s.shape, 1) + c * ATT_CW) % ATT_TQ
                s = jnp.where(krow // CHUNK <= qcol // CHUNK, s, -jnp.inf)
            pv_update(c, vt, alpha_prev, visible_rows(prev_diag, c))
            if not rows:
                ms.append(m[:, cs])
                ls.append(l[:, cs])
                alphas.append(jnp.ones_like(m[:, cs]))
                continue
            m_new = jnp.maximum(m[:, cs], jnp.max(s, axis=0, keepdims=True))
            alpha = jnp.exp2(m[:, cs] - m_new)
            p = jnp.exp2(s - m_new)
            ls.append(alpha * l[:, cs] + jnp.sum(p, axis=0, keepdims=True))
            p_ref[c, :rows, :] = p.astype(BF16)
            ms.append(m_new)
            alphas.append(alpha)
        return jnp.concatenate(ms, axis=1), jnp.concatenate(ls, axis=1), jnp.concatenate(alphas, axis=1)

    def trip(t, carry, masked):
        for d in range(ATT_KB):
            diag = d if masked else None
            prev_diag = d - 1 if masked and d > 0 else None
            carry = block(ATT_KB * t + d, carry, diag, prev_diag)
        return carry

    init = (jnp.full((1, nq), -jnp.inf, F32), jnp.zeros((1, nq), F32), jnp.ones((1, nq), F32))
    carry = lax.fori_loop(0, i, lambda t, c: trip(t, c, False), init)
    m, l, alpha = trip(i, carry, True)
    last = ATT_KB * i + ATT_KB - 1
    vt = vt_ref[last]
    for c in range(nq // ATT_CW):
        pv_update(c, vt, alpha, visible_rows(ATT_KB - 1, c))
    o = jnp.concatenate([acc_ref[c] for c in range(nq // ATT_CW)], axis=1) / l
    o = o[:, :ATT_TQ] - lam_ref[...] * o[:, ATT_TQ:]
    o = o * lax.rsqrt(jnp.mean(o * o, axis=0, keepdims=True) + RMS_EPS) * g_ref[...]
    o_ref[...] = (o * (1.0 - lambda_init)).T.astype(o_ref.dtype)


def _attention(qr, kr, vt, lam, subln_g, lambda_init, bsz, seq):
    nq = seq // ATT_TQ
    assert vt.shape == (bsz * DA_HEADS, seq // ATT_TK, DA_V_DIM, ATT_TK)
    col = pl.BlockSpec((LANES, 1), lambda b, h, i: (0, 0))
    qo = pl.BlockSpec((ATT_TQ, LANES), lambda b, h, i: (b * nq + i, h))
    return pl.pallas_call(
        functools.partial(_attn_kernel, lambda_init=lambda_init),
        out_shape=jax.ShapeDtypeStruct((bsz * seq, DA_WIDTH), BF16),
        grid=(bsz, DA_HEADS, nq),
        in_specs=[pl.BlockSpec((1, 1), lambda b, h, i: (0, 0)), col, qo,
                  pl.BlockSpec((seq, LANES), lambda b, h, i: (b, h)),
                  pl.BlockSpec((None, seq // ATT_TK, DA_V_DIM, ATT_TK), lambda b, h, i: (b * DA_HEADS + h, 0, 0, 0))],
        out_specs=qo,
        scratch_shapes=[pltpu.VMEM((2 * ATT_TQ, LANES), BF16),
                        pltpu.VMEM((2 * ATT_TQ // ATT_CW, ATT_TK, ATT_CW), BF16),
                        pltpu.VMEM((2 * ATT_TQ // ATT_CW, DA_V_DIM, ATT_CW), F32)],
        compiler_params=_cp("parallel", "parallel", "arbitrary"),
        name="diff_attention",
    )(lam.reshape(1, 1), subln_g.reshape(LANES, 1), qr, kr, vt)


RW_VLO = 4
RW_VHI = RW_HEAD_DIM // RW_VLO
RW_BH = LANES // RW_VLO
RW_TB = 128
RW_PITCH = RW_HEAD_DIM + SUBLANES
RW_TT = 64
RW_NACC = 4
N_KOPS = 4
RW_NORM = 64


def _rw_prep_kernel(p_ref, halo_ref, mu_ref, w0_ref, a0_ref, kk_ref, ka_ref, rk_ref,
                    w2_ref, a2_ref, g2_ref, seg_ref, tri_ref,
                    kop_ref, v_ref, dend_ref, g_ref, bonus_ref, tr_ref, de_ref, *, n_blocks):
    i = pl.program_id(0)
    n_op = pl.program_id(1)
    bsz = p_ref.shape[0]
    blk = jnp.minimum(i, n_blocks - 1)
    cur = i % 2
    prv = 1 - cur
    first = jnp.where(blk == 0, 0.0, 1.0)
    seg = seg_ref[...]
    tri = tri_ref[...]

    @pl.when((i == 0) & (n_op == 0))
    def _():
        tr_ref[1] = jnp.zeros(tr_ref.shape[1:], F32)
        de_ref[...] = jnp.zeros_like(de_ref)

    def compute(b):
        p = p_ref[b]
        row = lax.broadcasted_iota(jnp.int32, p.shape, 0)
        last_prev = jnp.broadcast_to(halo_ref[b, SUBLANES - 1:SUBLANES, :], p.shape) * first
        prev = jnp.where(row == 0, last_prev, pltpu.roll(p, 1, axis=0))
        p = p + (prev - p) * mu_ref[...]
        r = p[:, 0:RW_WIDTH]
        k = p[:, RW_WIDTH:2 * RW_WIDTH]
        v = p[:, 2 * RW_WIDTH:3 * RW_WIDTH]
        low = p[:, 3 * RW_WIDTH:RW_IN]
        log_w = -RW_DECAY_SCALE * jax.nn.sigmoid(w0_ref[...] + _dot_split(jnp.tanh(low), w2_ref))
        a = jax.nn.sigmoid(a0_ref[...] + _dot_split(low, a2_ref))
        g_ref[b] = _dot_split(jax.nn.sigmoid(low), g2_ref)
        kk = k * kk_ref[...]
        kk = kk * lax.rsqrt(jnp.maximum(_segment_sum(kk * kk, seg), 1e-24))
        k = k * (1.0 + (a - 1.0) * ka_ref[...])
        bonus_ref[b] = _segment_sum(r * k * rk_ref[...], seg) * v
        log_d = sum(jnp.dot(tri, piece, preferred_element_type=F32) for piece in _split3(log_w))
        d_inc = jnp.exp(log_d)
        d_inv = jnp.exp(-log_d)
        d_exc = jnp.exp(log_d - log_w)
        for n, val in enumerate((r * d_inc, k * d_inv, -kk * d_exc, kk * a * d_inv, v)):
            vt = val.T
            for h in range(RW_HEADS):
                base = pl.multiple_of((b * RW_HEADS + h) * RW_PITCH, SUBLANES)
                tr_ref[cur, n, pl.ds(base, RW_HEAD_DIM), :] = vt[h * RW_HEAD_DIM:(h + 1) * RW_HEAD_DIM, :]
        for e in range(RW_TB // RW_NORM):
            end = d_inc[(e + 1) * RW_NORM - 1:(e + 1) * RW_NORM, :]
            for h in range(RW_HEADS):
                de_ref[cur, e, pl.ds(b * RW_HEADS + h, 1), 0:RW_HEAD_DIM] = end[:, h * RW_HEAD_DIM:(h + 1) * RW_HEAD_DIM]

    def v_relayout():
        for vh in range(RW_VHI):
            rows = [tr_ref[cur, N_KOPS, pl.ds(vh * RW_VLO + vl, RW_BH, stride=RW_PITCH), :]
                    for vl in range(RW_VLO)]
            v_ref[pl.ds(vh, RW_TB, stride=RW_VHI), :] = jnp.concatenate(rows, axis=0).T
        for e in range(RW_TB // RW_NORM):
            tile = jnp.concatenate([de_ref[cur, e]] * RW_VLO, axis=0).T
            dend_ref[e] = tile[:RW_HEAD_DIM, :]

    def emit():
        for kk_i in range(RW_HEAD_DIM):
            rows = tr_ref[prv, n_op, pl.ds(kk_i, RW_BH, stride=RW_PITCH), :]
            tile = jnp.concatenate([rows] * RW_VLO, axis=0).T
            kop_ref[0, :, kk_i] = tile.reshape(RW_TB // SUBLANES, SUBLANES, LANES)

    per_step = bsz // N_KOPS

    @pl.when(n_op < N_KOPS - 1)
    def _():
        emit()
        for s in range(per_step):
            compute(n_op * per_step + s)

    @pl.when(n_op == N_KOPS - 1)
    def _():
        emit()
        for s in range(per_step):
            compute(n_op * per_step + s)
        v_relayout()


def _head_segments():
    idx = np.arange(RW_WIDTH) // RW_HEAD_DIM
    return jnp.asarray((idx[:, None] == idx[None, :]).astype(np.float32)).astype(BF16)


def _rw_prep(proj3, mu, w0, w2, a0, a2, g2, k_k, k_a, r_k):
    bsz, seq, _ = proj3.shape
    hb = RW_TB // SUBLANES
    w2p = _hi_lo(jnp.zeros((LANES, RW_WIDTH), F32).at[0:32].set(w2))
    a2p = _hi_lo(jnp.zeros((LANES, RW_WIDTH), F32).at[32:64].set(a2))
    g2p = _hi_lo(jnp.zeros((LANES, RW_WIDTH), F32).at[64:128].set(g2))
    lowrank = pl.BlockSpec((2, LANES, RW_WIDTH), lambda i, j: (0, 0, 0))
    vec = lambda n: pl.BlockSpec((1, n), lambda i, j: (0, 0))
    mat = lambda r, c: pl.BlockSpec((r, c), lambda i, j: (0, 0))
    n_blocks = seq // RW_TB
    assert bsz % N_KOPS == 0 and RW_TB % RW_NORM == 0
    t_idx = np.arange(RW_TB)
    tri = jnp.asarray(((t_idx[:, None] >= t_idx[None, :])
                       & (t_idx[:, None] // RW_NORM == t_idx[None, :] // RW_NORM)).astype(np.float32)).astype(BF16)
    blk = lambda i: jnp.minimum(i, n_blocks - 1)
    tok = pl.BlockSpec((bsz, RW_TB, RW_WIDTH), lambda i, j: (0, blk(i), 0))
    tshape = jax.ShapeDtypeStruct((bsz, seq, RW_WIDTH), F32)
    return pl.pallas_call(
        functools.partial(_rw_prep_kernel, n_blocks=n_blocks),
        out_shape=(jax.ShapeDtypeStruct((N_KOPS, seq // SUBLANES, RW_HEAD_DIM, SUBLANES, LANES), F32),
                   jax.ShapeDtypeStruct((seq * RW_VHI, LANES), F32),
                   jax.ShapeDtypeStruct((seq // RW_NORM, RW_HEAD_DIM, LANES), F32), tshape, tshape),
        grid=(n_blocks + 1, N_KOPS),
        in_specs=[pl.BlockSpec((bsz, RW_TB, RW_IN), lambda i, j: (0, blk(i), 0)),
                  pl.BlockSpec((bsz, SUBLANES, RW_IN),
                               lambda i, j: (0, jnp.maximum(blk(i) * hb - 1, 0), 0)),
                  vec(RW_IN), vec(RW_WIDTH), vec(RW_WIDTH), vec(RW_WIDTH), vec(RW_WIDTH), vec(RW_WIDTH),
                  lowrank, lowrank, lowrank, mat(RW_WIDTH, RW_WIDTH), mat(RW_TB, RW_TB)],
        out_specs=(pl.BlockSpec((1, RW_TB // SUBLANES, RW_HEAD_DIM, SUBLANES, LANES),
                                lambda i, j: (j, jnp.maximum(i - 1, 0), 0, 0, 0)),
                   pl.BlockSpec((RW_TB * RW_VHI, LANES), lambda i, j: (blk(i), 0)),
                   pl.BlockSpec((RW_TB // RW_NORM, RW_HEAD_DIM, LANES), lambda i, j: (blk(i), 0, 0)), tok, tok),
        scratch_shapes=[pltpu.VMEM((2, N_KOPS + 1, RW_BH * RW_PITCH, RW_TB), F32),
                        pltpu.VMEM((2, RW_TB // RW_NORM, RW_BH, LANES), F32)],
        compiler_params=_cp("arbitrary", "arbitrary"),
        name="rwkv_prep",
    )(proj3, proj3, mu.reshape(1, RW_IN), w0.reshape(1, -1), a0.reshape(1, -1), k_k.reshape(1, -1),
      k_a.reshape(1, -1), r_k.reshape(1, -1), w2p, a2p, g2p, _head_segments(), tri)


def _rw_scan_kernel(r_ref, k_ref, an_ref, b_ref, v_ref, dend_ref, y_ref, s_ref):
    @pl.when(pl.program_id(0) == 0)
    def _():
        s_ref[...] = jnp.zeros_like(s_ref)

    nvb = RW_VHI // SUBLANES
    vsl = [slice(vb * SUBLANES, (vb + 1) * SUBLANES) for vb in range(nvb)]
    zeros = lambda: [[jnp.zeros((SUBLANES, LANES), F32) for _ in range(RW_NACC)] for _ in range(nvb)]
    total = lambda acc: [sum(acc[vb][1:], acc[vb][0]) for vb in range(nvb)]

    def body(tb, _):
        def row(ref, kk, ts):
            return jnp.broadcast_to(ref[tb, kk, ts:ts + 1, :], (SUBLANES, LANES))

        zacc = zeros()
        for kk in range(RW_HEAD_DIM):
            an = row(an_ref, kk, 0)
            for vb in range(nvb):
                zacc[vb][kk % RW_NACC] += s_ref[kk, vsl[vb], :] * an
        z = total(zacc)
        for ts in range(SUBLANES):
            t = tb * SUBLANES + ts
            vt = [v_ref[t, vsl[vb], :] for vb in range(nvb)]
            yacc, zacc = zeros(), zeros()
            for kk in range(RW_HEAD_DIM):
                br, kr, rr = row(b_ref, kk, ts), row(k_ref, kk, ts), row(r_ref, kk, ts)
                an = row(an_ref, kk, ts + 1) if ts + 1 < SUBLANES else None
                for vb in range(nvb):
                    s_new = s_ref[kk, vsl[vb], :] + z[vb] * br + vt[vb] * kr
                    s_ref[kk, vsl[vb], :] = s_new
                    yacc[vb][kk % RW_NACC] += s_new * rr
                    if an is not None:
                        zacc[vb][kk % RW_NACC] += s_new * an
            for vb, y in enumerate(total(yacc)):
                y_ref[t, vsl[vb], :] = y
            z = total(zacc)
        return 0

    lax.fori_loop(0, RW_TT // SUBLANES, body, 0)
    for kk in range(RW_HEAD_DIM):
        d_end = jnp.broadcast_to(dend_ref[0, kk:kk + 1, :], (SUBLANES, LANES))
        for vb in range(nvb):
            s_ref[kk, vsl[vb], :] = s_ref[kk, vsl[vb], :] * d_end


def _rw_scan(kops, v4, dend, seq):
    assert RW_TT == RW_NORM
    kspec = lambda n: pl.BlockSpec((None, RW_TT // SUBLANES, RW_HEAD_DIM, SUBLANES, LANES),
                                   lambda i: (n, i, 0, 0, 0))
    vspec = pl.BlockSpec((RW_TT, RW_VHI, LANES), lambda i: (i, 0, 0))
    return pl.pallas_call(
        _rw_scan_kernel,
        out_shape=jax.ShapeDtypeStruct((seq, RW_VHI, LANES), F32),
        grid=(seq // RW_TT,),
        in_specs=[kspec(n) for n in range(N_KOPS)] + [vspec, pl.BlockSpec((1, RW_HEAD_DIM, LANES), lambda i: (i, 0, 0))],
        out_specs=vspec,
        scratch_shapes=[pltpu.VMEM((RW_HEAD_DIM, RW_VHI, LANES), F32)],
        compiler_params=_cp("arbitrary"),
        name="rwkv_scan",
    )(*([kops] * N_KOPS), v4.reshape(seq, RW_VHI, LANES), dend)


def _rw_post_kernel(y_ref, bonus_ref, gate_ref, lg_ref, lb_ref, seg_ref, o_ref, tr_ref):
    bsz = bonus_ref.shape[0]

    for vh in range(RW_VHI):
        yt = y_ref[pl.ds(vh, RW_TB, stride=RW_VHI), :].T
        for vl in range(RW_VLO):
            tr_ref[pl.ds(vh * RW_VLO + vl, RW_BH, stride=RW_PITCH), :] = yt[vl * RW_BH:(vl + 1) * RW_BH, :]

    seg = seg_ref[...]
    head_mean = lambda a: _segment_sum(a, seg) * (1.0 / RW_HEAD_DIM)

    for b in range(bsz):
        slabs = [tr_ref[pl.ds((b * RW_HEADS + h) * RW_PITCH, RW_HEAD_DIM), :] for h in range(RW_HEADS)]
        y = jnp.concatenate(slabs, axis=0).T
        c = y - head_mean(y)
        var = head_mean(c * c)
        y = c * lax.rsqrt(var + RW_LNX_EPS) * lg_ref[...] + lb_ref[...]
        o_ref[b] = ((y + bonus_ref[b]) * gate_ref[b]).astype(o_ref.dtype)


def _rw_post(y4, bonus, gate, lnx_g, lnx_b):
    bsz, seq, _ = bonus.shape
    tok = pl.BlockSpec((bsz, RW_TB, RW_WIDTH), lambda i: (0, i, 0))
    vec = pl.BlockSpec((1, RW_WIDTH), lambda i: (0, 0))
    return pl.pallas_call(
        _rw_post_kernel,
        out_shape=jax.ShapeDtypeStruct((bsz, seq, RW_WIDTH), BF16),
        grid=(seq // RW_TB,),
        in_specs=[pl.BlockSpec((RW_TB * RW_VHI, LANES), lambda i: (i, 0)), tok, tok, vec, vec,
                  pl.BlockSpec((RW_WIDTH, RW_WIDTH), lambda i: (0, 0))],
        out_specs=tok,
        scratch_shapes=[pltpu.VMEM((RW_BH * RW_PITCH, RW_TB), F32)],
        compiler_params=_cp("parallel"),
        name="rwkv_post",
    )(y4.reshape(seq * RW_VHI, LANES), bonus, gate, lnx_g.reshape(1, -1), lnx_b.reshape(1, -1), _head_segments())


S5_TT = 128
S5_PITCH = S5_TT + SUBLANES


def _s5_kernel(u_ref, wb_ref, wc_ref, a_ref, d_ref, gw_ref, gb_ref, og_ref, o_ref, x_ref, carry_ref):
    @pl.when(pl.program_id(0) == 0)
    def _():
        carry_ref[...] = jnp.zeros_like(carry_ref)

    n = S5_CPLX
    nc = n // LANES
    bsz = u_ref.shape[0]

    def project(b, _):
        base = b * S5_PITCH
        bu = jnp.dot(u_ref[b].astype(BF16), wb_ref[...], preferred_element_type=F32)
        for c in range(2 * nc):
            x_ref[c, pl.ds(base, S5_TT), :] = bu[:, c * LANES:(c + 1) * LANES]
        return 0
    for b in range(bsz):
        project(b, 0)

    ar = [jnp.broadcast_to(a_ref[:, c * LANES:(c + 1) * LANES], (bsz, LANES)) for c in range(nc)]
    ai = [jnp.broadcast_to(a_ref[:, n + c * LANES:n + (c + 1) * LANES], (bsz, LANES)) for c in range(nc)]

    def steps(tb, carry):
        cr, ci = list(carry[0]), list(carry[1])
        for ts in range(SUBLANES):
            rows = pl.ds(tb * SUBLANES + ts, bsz, stride=S5_PITCH)
            for c in range(nc):
                cr[c], ci[c] = (ar[c] * cr[c] - ai[c] * ci[c] + x_ref[c, rows, :],
                                ar[c] * ci[c] + ai[c] * cr[c] + x_ref[nc + c, rows, :])
                x_ref[c, rows, :] = cr[c]
                x_ref[nc + c, rows, :] = ci[c]
        return tuple(cr), tuple(ci)

    init = (tuple(carry_ref[c] for c in range(nc)), tuple(carry_ref[nc + c] for c in range(nc)))
    cr, ci = lax.fori_loop(0, S5_TT // SUBLANES, steps, init)
    for c in range(nc):
        carry_ref[c] = cr[c]
        carry_ref[nc + c] = ci[c]

    def readout(b, _):
        base = b * S5_PITCH
        u = u_ref[b]
        x = jnp.concatenate([x_ref[c, pl.ds(base, S5_TT), :] for c in range(2 * nc)], axis=1)
        y = jnp.dot(x.astype(BF16), wc_ref[...], preferred_element_type=F32) + d_ref[...] * u
        z = jax.nn.gelu(y)
        z = z * jax.nn.sigmoid(jnp.dot(z.astype(BF16), gw_ref[...], preferred_element_type=F32) + gb_ref[...])
        o_ref[b] = (z * lax.rsqrt(jnp.mean(z * z, axis=-1, keepdims=True) + RMS_EPS) * og_ref[...]).astype(o_ref.dtype)
        return 0
    for b in range(bsz):
        readout(b, 0)


def _s5_tables(a_re, a_im, b_re, b_im, c_re, c_im, log_step):
    delta = jnp.exp(log_step.astype(F32))[:, None]
    mag = jnp.exp(delta * a_re)

    def a_pow(n):
        return (mag ** n) * jnp.cos(n * delta * a_im), (mag ** n) * jnp.sin(n * delta * a_im)

    abr, abi = a_pow(1)
    den = a_re * a_re + a_im * a_im
    qr = ((abr - 1.0) * a_re + abi * a_im) / den
    qi = (abi * a_re - (abr - 1.0) * a_im) / den
    bbr = qr[..., None] * b_re - qi[..., None] * b_im
    bbi = qr[..., None] * b_im + qi[..., None] * b_re
    eye = jnp.eye(S5_GROUPS, dtype=F32)
    blk_b = lambda m: jnp.einsum('gph,gk->ghkp', m, eye).reshape(S5_WIDTH, S5_CPLX)
    blk_c = lambda m: jnp.einsum('ghp,gk->gpkh', m, eye).reshape(S5_CPLX, S5_WIDTH)
    wb = jnp.concatenate([blk_b(bbr), blk_b(bbi)], axis=1).astype(BF16)
    wc = jnp.concatenate([blk_c(c_re), -blk_c(c_im)], axis=0).astype(BF16)
    a_row = jnp.concatenate([abr.reshape(1, -1), abi.reshape(1, -1)], axis=1)
    return wb, wc, a_row


def _s5(proj3, tables, d_skip, glu_w, glu_b, out_g):
    wb, wc, a_row = tables
    bsz, seq, _ = proj3.shape
    full = lambda a: pl.BlockSpec(a.shape, lambda i: (0,) * a.ndim)
    vec = pl.BlockSpec((1, S5_WIDTH), lambda i: (0, 0))
    gw = glu_w.astype(BF16)
    return pl.pallas_call(
        _s5_kernel,
        out_shape=jax.ShapeDtypeStruct((bsz, seq, S5_WIDTH), BF16),
        grid=(seq // S5_TT,),
        in_specs=[pl.BlockSpec((bsz, S5_TT, S5_WIDTH), lambda i: (0, i, 0)),
                  full(wb), full(wc), full(a_row), vec, full(gw), vec, vec],
        out_specs=pl.BlockSpec((bsz, S5_TT, S5_WIDTH), lambda i: (0, i, 0)),
        scratch_shapes=[pltpu.VMEM((2 * S5_CPLX // LANES, bsz * S5_PITCH, LANES), F32),
                        pltpu.VMEM((2 * S5_CPLX // LANES, bsz, LANES), F32)],
        compiler_params=_cp("arbitrary"),
        name="s5",
    )(proj3, wb, wc, a_row, d_skip.reshape(1, -1), gw, glu_b.reshape(1, -1), out_g.reshape(1, -1))


def _xattn_kernel(hda_ref, hrw_ref, hs5_ref, x_ref, wda_ref, wrw_ref, ws5_ref, g1_ref, b1_ref,
                  kv_ref, wq_ref, wo_ref, g2_ref, b2_ref, o_ref):
    mix = ALPHA * x_ref[...]
    for h_ref, w_ref in ((hda_ref, wda_ref), (hrw_ref, wrw_ref), (hs5_ref, ws5_ref)):
        mix = mix + jnp.dot(h_ref[...].astype(BF16), w_ref[...], preferred_element_type=F32)
    x = _layer_norm_rows(mix, g1_ref[...], b1_ref[...])
    q = jnp.dot(x.astype(BF16), wq_ref[...], preferred_element_type=F32)
    outs = []
    for h in range(MEM_HEADS):
        sl = slice(h * MEM_HEAD_DIM, (h + 1) * MEM_HEAD_DIM)
        kh = kv_ref[:, sl]
        vh = kv_ref[:, D_MODEL + h * MEM_HEAD_DIM:D_MODEL + (h + 1) * MEM_HEAD_DIM]
        s = lax.dot_general(q[:, sl].astype(BF16), kh, (((1,), (1,)), ((), ())),
                            preferred_element_type=F32) * (MEM_HEAD_DIM ** -0.5)
        p = jnp.exp(s - jnp.max(s, axis=-1, keepdims=True))
        p = p / jnp.sum(p, axis=-1, keepdims=True)
        outs.append(jnp.dot(p.astype(BF16), vh, preferred_element_type=F32))
    o = jnp.concatenate(outs, axis=-1)
    h_out = jnp.dot(o.astype(BF16), wo_ref[...], preferred_element_type=F32)
    o_ref[...] = _layer_norm_rows(ALPHA * x + h_out, g2_ref[...], b2_ref[...])


def _mix_out_cross_attention(h_list, w_list, x, g1, b1, kv, wq, wo, g2, b2, seq, tm):
    t, d = x.shape
    n_mem = kv.shape[0] // (t // seq)
    tiles_per_seq = seq // tm
    const = lambda a: pl.BlockSpec(a.shape, lambda i: (0, 0))
    vec = pl.BlockSpec((1, d), lambda i: (0, 0))
    row = lambda n: pl.BlockSpec((tm, n), lambda i: (i, 0))
    return pl.pallas_call(
        _xattn_kernel,
        out_shape=jax.ShapeDtypeStruct((t, d), F32),
        grid=(t // tm,),
        in_specs=[row(h.shape[1]) for h in h_list] + [row(d)] + [const(w) for w in w_list] + [vec, vec]
        + [pl.BlockSpec((n_mem, 2 * d), lambda i: (i // tiles_per_seq, 0)), const(wq), const(wo), vec, vec],
        out_specs=row(d),
        compiler_params=_cp("parallel"),
        name="mix_out_cross_attention",
    )(*h_list, x, *w_list, g1.reshape(1, d), b1.reshape(1, d), kv, wq, wo, g2.reshape(1, d), b2.reshape(1, d))


FFN_TN = 1408


def _ffn_up_kernel(x_ref, halo_ref, wa_ref, wg_ref, cw_ref, cb_ref, o_ref, *, tiles_per_seq):
    i = pl.program_id(1)
    x = x_ref[...].astype(BF16)
    first = jnp.where(i % tiles_per_seq == 0, 0.0, 1.0)
    xh = (halo_ref[...] * first).astype(BF16)
    a8 = jnp.dot(jnp.concatenate([xh, x], axis=0), wa_ref[...], preferred_element_type=F32)
    a0 = a8[SUBLANES:]
    a1 = pltpu.roll(a8, 1, axis=0)[SUBLANES:]
    a2 = pltpu.roll(a8, 2, axis=0)[SUBLANES:]
    conv = cb_ref[...] + cw_ref[0:1, :] * a2 + cw_ref[1:2, :] * a1 + cw_ref[2:3, :] * a0
    g = jnp.dot(x, wg_ref[...], preferred_element_type=F32)
    o_ref[...] = (jax.nn.silu(conv) * g).astype(o_ref.dtype)


def _ffn_up(x, wa, wg, conv_w, conv_b, seq, tm):
    t, d = x.shape
    hb = tm // SUBLANES
    cw = jnp.zeros((SUBLANES, D_FF), F32).at[0:3].set(conv_w)
    wspec = pl.BlockSpec((d, FFN_TN), lambda j, i: (0, j))
    return pl.pallas_call(
        functools.partial(_ffn_up_kernel, tiles_per_seq=seq // tm),
        out_shape=jax.ShapeDtypeStruct((t, D_FF), BF16),
        grid=(D_FF // FFN_TN, t // tm),
        in_specs=[pl.BlockSpec((tm, d), lambda j, i: (i, 0)),
                  pl.BlockSpec((SUBLANES, d), lambda j, i: (jnp.maximum(i * hb - 1, 0), 0)),
                  wspec, wspec,
                  pl.BlockSpec((SUBLANES, FFN_TN), lambda j, i: (0, j)),
                  pl.BlockSpec((1, FFN_TN), lambda j, i: (0, j))],
        out_specs=pl.BlockSpec((tm, FFN_TN), lambda j, i: (i, j)),
        compiler_params=_cp("arbitrary", "parallel"),
        name="ffn_up",
    )(x, x, wa, wg, cw, conv_b.reshape(1, D_FF))


def kernel(x, mem, positions, w_in, da_lam_q1, da_lam_k1, da_lam_q2, da_lam_k2, da_subln_g, rw_mu, rw_w0, rw_w2, rw_a0, rw_a2, rw_g2, rw_k_k, rw_k_a, rw_r_k, rw_lnx_g, rw_lnx_b, s5_a_re, s5_a_im, s5_b_re, s5_b_im, s5_c_re, s5_c_im, s5_d, s5_log_step, s5_glu_w, s5_glu_b, s5_out_g, w_out, ln1_g, ln1_b, ca_wq, ca_wkv, ca_wo, ln2_g, ln2_b, ffn_w_up, ffn_conv_w, ffn_conv_b, ffn_w_down, ln3_g, ln3_b):
    bsz, seq, d = x.shape
    t = bsz * seq
    tm = min(512, seq)
    n_mem = mem.shape[1]
    xf = x.reshape(t, d)
    memf = mem.reshape(bsz * n_mem, d)
    cos_t, sin_t = _rope_tables(positions, tm)

    for l in range(DEPTH):
        lambda_init = 0.8 - 0.6 * math.exp(-0.3 * l)
        wi = w_in[l]
        w_perm = jnp.concatenate([wi[:, :3 * DA_WIDTH], wi[:, 3 * DA_WIDTH + RW_IN:],
                                  wi[:, 3 * DA_WIDTH:3 * DA_WIDTH + RW_IN]], axis=1).astype(BF16)
        qr, kr, vt, p_rw, p_s5 = _in_proj(xf, w_perm, cos_t, sin_t, seq, tm)
        lam = (jnp.exp(jnp.sum(da_lam_q1[l] * da_lam_k1[l])) - jnp.exp(jnp.sum(da_lam_q2[l] * da_lam_k2[l]))
               + lambda_init)
        kops, v4, dend, gate, bonus = _rw_prep(p_rw.reshape(bsz, seq, RW_IN), rw_mu[l], rw_w0[l], rw_w2[l],
                                               rw_a0[l], rw_a2[l], rw_g2[l], rw_k_k[l], rw_k_a[l],
                                               rw_r_k[l].reshape(-1))
        h_da = _attention(qr, kr, vt, lam, da_subln_g[l], lambda_init, bsz, seq)
        y4 = _rw_scan(kops, v4, dend, seq)
        h_rw = _rw_post(y4, bonus, gate, rw_lnx_g[l], rw_lnx_b[l]).reshape(t, RW_WIDTH)

        tables = _s5_tables(s5_a_re[l], s5_a_im[l], s5_b_re[l], s5_b_im[l], s5_c_re[l], s5_c_im[l],
                            s5_log_step[l])
        h_s5 = _s5(p_s5.reshape(bsz, seq, S5_WIDTH), tables, s5_d[l], s5_glu_w[l], s5_glu_b[l],
                   s5_out_g[l]).reshape(t, S5_WIDTH)

        wo = w_out[l].astype(BF16)
        kv = _matmul(memf, ca_wkv[l].astype(BF16), BF16, n_mem, 1024)
        xf = _mix_out_cross_attention(
            [h_da, h_rw, h_s5], [wo[:DA_WIDTH], wo[DA_WIDTH:DA_WIDTH + RW_WIDTH], wo[DA_WIDTH + RW_WIDTH:]],
            xf, ln1_g[l], ln1_b[l], kv, ca_wq[l].astype(BF16), ca_wo[l].astype(BF16), ln2_g[l], ln2_b[l], seq, tm)
        w_up = ffn_w_up[l].astype(BF16)
        hff = _ffn_up(xf, w_up[:, :D_FF], w_up[:, D_FF:], ffn_conv_w[l], ffn_conv_b[l], seq, tm)
        xf = _matmul_ln([hff], [ffn_w_down[l].astype(BF16)], xf, ln3_g[l], ln3_b[l], tm)
    return xf.reshape(bsz, seq, d)
```

```python
import functools
import math

import numpy as np
import jax
import jax.numpy as jnp
from jax import lax
from jax.experimental import pallas as pl
from jax.experimental.pallas import tpu as pltpu

F32 = jnp.float32
BF16 = jnp.bfloat16
HI = lax.Precision.HIGHEST

D_MODEL = 1024
DEPTH = 4
CHUNK = 64
LN_EPS = 1e-5
LOG2E = math.log2(math.e)
RMS_EPS = 1e-6

DA_HEAD_DIM = 64
DA_V_DIM = 128
DA_WIDTH = 512
DA_HEADS = 4
ROPE_THETA = 10000.0

RW_HEAD_DIM = 64
RW_WIDTH = 256
RW_HEADS = 4
RW_DECAY_RANK = 32
RW_AAA_RANK = 32
RW_GATE_RANK = 64
RW_IN = 3 * RW_WIDTH + RW_DECAY_RANK + RW_AAA_RANK + RW_GATE_RANK
RW_DECAY_SCALE = math.exp(-0.5)
RW_LNX_EPS = 64e-5

S5_WIDTH = 256
S5_GROUP_CH = 16
S5_GROUPS = 16
S5_STATE = 64
S5_CPLX = S5_GROUPS * S5_STATE

IN_WIDTH = 3 * DA_WIDTH + RW_IN + S5_WIDTH
MEM_HEADS = 4
MEM_HEAD_DIM = 256
D_FF = 2816
ALPHA = (2.0 * DEPTH) ** 0.25

COL_Q, COL_K, COL_V, COL_S5, COL_RW = 0, 512, 1024, 1536, 1792

VMEM_LIMIT = 48 * 1024 * 1024
LANES = 128
SUBLANES = 8


def _cp(*sem):
    return pltpu.CompilerParams(dimension_semantics=sem, vmem_limit_bytes=VMEM_LIMIT)


def _layer_norm_rows(v, g, b):
    mu = jnp.mean(v, axis=-1, keepdims=True)
    c = v - mu
    var = jnp.mean(c * c, axis=-1, keepdims=True)
    return c * lax.rsqrt(var + LN_EPS) * g + b


def _split3(x):
    p1 = x.astype(BF16)
    r1 = x - p1.astype(F32)
    p2 = r1.astype(BF16)
    p3 = (r1 - p2.astype(F32)).astype(BF16)
    return p1, p2, p3


def _segment_sum(x, seg):
    return sum(jnp.dot(p, seg, preferred_element_type=F32) for p in _split3(x))


def _dot_split(a, w_ref):
    a1, a2, _ = _split3(a)
    d = functools.partial(jnp.dot, preferred_element_type=F32)
    return d(a1, w_ref[0]) + (d(a1, w_ref[1]) + d(a2, w_ref[0]))


def _hi_lo(w):
    hi = w.astype(BF16)
    return jnp.stack([hi, (w - hi.astype(F32)).astype(BF16)])


def _mm_kernel(a_ref, w_ref, o_ref):
    o_ref[...] = jnp.dot(a_ref[...].astype(BF16), w_ref[...],
                         preferred_element_type=F32).astype(o_ref.dtype)


def _matmul(a, w, out_dtype, tm, tn):
    m, k = a.shape
    n = w.shape[1]
    return pl.pallas_call(
        _mm_kernel,
        out_shape=jax.ShapeDtypeStruct((m, n), out_dtype),
        grid=(m // tm, n // tn),
        in_specs=[pl.BlockSpec((tm, k), lambda i, j: (i, 0)),
                  pl.BlockSpec((k, tn), lambda i, j: (0, j))],
        out_specs=pl.BlockSpec((tm, tn), lambda i, j: (i, j)),
        compiler_params=_cp("parallel", "arbitrary"),
        name="matmul",
    )(a, w)


def _mm_ln_kernel(n_in, *refs):
    a_refs = refs[:n_in]
    w_refs = refs[n_in:2 * n_in]
    x_ref, g_ref, b_ref, o_ref = refs[2 * n_in:]
    acc = ALPHA * x_ref[...]
    for a_ref, w_ref in zip(a_refs, w_refs):
        acc = acc + jnp.dot(a_ref[...].astype(BF16), w_ref[...], preferred_element_type=F32)
    o_ref[...] = _layer_norm_rows(acc, g_ref[...], b_ref[...])


def _matmul_ln(a_list, w_list, x, g, b, tm):
    m, d = x.shape
    n_in = len(a_list)
    in_specs = [pl.BlockSpec((tm, a.shape[1]), lambda i: (i, 0)) for a in a_list]
    in_specs += [pl.BlockSpec(w.shape, lambda i: (0, 0)) for w in w_list]
    in_specs += [pl.BlockSpec((tm, d), lambda i: (i, 0)),
                 pl.BlockSpec((1, d), lambda i: (0, 0)),
                 pl.BlockSpec((1, d), lambda i: (0, 0))]
    return pl.pallas_call(
        functools.partial(_mm_ln_kernel, n_in),
        out_shape=jax.ShapeDtypeStruct((m, d), F32),
        grid=(m // tm,),
        in_specs=in_specs,
        out_specs=pl.BlockSpec((tm, d), lambda i: (i, 0)),
        compiler_params=_cp("parallel"),
        name="matmul_ln",
    )(*a_list, *w_list, x, g.reshape(1, d), b.reshape(1, d))


def _rope_table_kernel(pos_ref, freq_ref, sign_ref, cos_ref, sin_ref):
    ang = pos_ref[...] * freq_ref[...]
    cos_ref[...] = jnp.cos(ang)
    sin_ref[...] = jnp.sin(ang) * sign_ref[...]


def _rope_tables(positions, tm):
    t = positions.size
    inv_freq = ROPE_THETA ** (-jnp.arange(0, DA_HEAD_DIM, 2, dtype=F32) / DA_HEAD_DIM)
    freq_row = jnp.tile(inv_freq, 4).reshape(1, LANES)
    sign_row = jnp.tile(jnp.concatenate([-jnp.ones((32,), F32), jnp.ones((32,), F32)]), 2).reshape(1, LANES)
    pos_col = positions.astype(F32).reshape(t, 1)
    row = pl.BlockSpec((1, LANES), lambda i: (0, 0))
    return pl.pallas_call(
        _rope_table_kernel,
        out_shape=(jax.ShapeDtypeStruct((t, LANES), F32),) * 2,
        grid=(t // tm,),
        in_specs=[pl.BlockSpec((tm, 1), lambda i: (i, 0)), row, row],
        out_specs=(pl.BlockSpec((tm, LANES), lambda i: (i, 0)),) * 2,
        compiler_params=_cp("parallel"),
        name="rope_tables",
    )(pos_col, freq_row, sign_row)


def _in_proj_kernel(x_ref, w_ref, cos_ref, sin_ref, qo_ref, ko_ref, vt_ref, rw_ref, s5_ref):
    p = jnp.dot(x_ref[...].astype(BF16), w_ref[...], preferred_element_type=F32)
    cos = cos_ref[...]
    sin = sin_ref[...]
    lane = lax.broadcasted_iota(jnp.int32, cos.shape, 1)
    low = (lane % DA_HEAD_DIM) < (DA_HEAD_DIM // 2)

    def rope(t):
        swapped = jnp.where(low, pltpu.roll(t, LANES - 32, axis=1), pltpu.roll(t, 32, axis=1))
        return t * cos + swapped * sin

    scale = DA_HEAD_DIM ** -0.5 * LOG2E
    for h in range(DA_HEADS):
        sl = slice(h * LANES, (h + 1) * LANES)
        qo_ref[:, sl] = (rope(p[:, COL_Q + h * LANES:COL_Q + (h + 1) * LANES]) * scale).astype(BF16)
        ko_ref[:, sl] = rope(p[:, COL_K + h * LANES:COL_K + (h + 1) * LANES]).astype(BF16)
    vt_ref[:, 0] = p[:, COL_V:COL_V + DA_WIDTH].T.astype(BF16).reshape(DA_HEADS, DA_V_DIM, p.shape[0])
    s5_ref[...] = p[:, COL_S5:COL_S5 + S5_WIDTH]
    rw_ref[...] = p[:, COL_RW:COL_RW + RW_IN]


def _in_proj(x, w, cos_t, sin_t, seq, tm):
    t, d = x.shape
    nt = seq // tm
    tab = pl.BlockSpec((tm, LANES), lambda i: (i, 0))
    row = lambda n: pl.BlockSpec((tm, n), lambda i: (i, 0))
    return pl.pallas_call(
        _in_proj_kernel,
        out_shape=(jax.ShapeDtypeStruct((t, DA_WIDTH), BF16), jax.ShapeDtypeStruct((t, DA_WIDTH), BF16),
                   jax.ShapeDtypeStruct((t // seq * DA_HEADS, nt, DA_V_DIM, tm), BF16),
                   jax.ShapeDtypeStruct((t, RW_IN), F32), jax.ShapeDtypeStruct((t, S5_WIDTH), F32)),
        grid=(t // tm,),
        in_specs=[row(d), pl.BlockSpec(w.shape, lambda i: (0, 0)), tab, tab],
        out_specs=(row(DA_WIDTH), row(DA_WIDTH),
                   pl.BlockSpec((DA_HEADS, 1, DA_V_DIM, tm), lambda i: (i // nt, i % nt, 0, 0)),
                   row(RW_IN), row(S5_WIDTH)),
        compiler_params=_cp("parallel"),
        name="in_proj",
    )(x, w, cos_t, sin_t)


ATT_TQ = 1024
ATT_TK = 512
ATT_CW = 256
ATT_KB = ATT_TQ // ATT_TK


def _attn_kernel(lam_ref, g_ref, q_ref, k_ref, vt_ref, o_ref, qs_ref, p_ref, acc_ref, *, lambda_init):
    i = pl.program_id(2)
    nq = 2 * ATT_TQ
    q = q_ref[...]
    lane = lax.broadcasted_iota(jnp.int32, q.shape, 1)
    zero = jnp.zeros_like(q)
    qs_ref[:ATT_TQ, :] = jnp.where(lane < DA_HEAD_DIM, q, zero)
    qs_ref[ATT_TQ:, :] = jnp.where(lane >= DA_HEAD_DIM, q, zero)
    p_ref[...] = jnp.zeros_like(p_ref)
    acc_ref[...] = jnp.zeros_like(acc_ref)

    def visible_rows(diag, c):
        if diag is None:
            return ATT_TK
        first_query = (c * ATT_CW) % ATT_TQ
        return max(0, min(ATT_TK, first_query + ATT_CW - diag * ATT_TK))

    def pv_update(c, vt, alpha_prev, rows):
        if rows == 0:
            return
        cs = slice(c * ATT_CW, (c + 1) * ATT_CW)
        pv = jnp.dot(vt[:, :rows], p_ref[c, :rows, :], preferred_element_type=F32)
        acc_ref[c] = alpha_prev[:, cs] * acc_ref[c] + pv

    def block(j, carry, diag, prev_diag):
        m, l, alpha_prev = carry
        kb = k_ref[pl.ds(pl.multiple_of(j * ATT_TK, ATT_TK), ATT_TK), :]
        vt = vt_ref[jnp.maximum(j - 1, 0)]
        ms, ls, alphas = [], [], []
        for c in range(nq // ATT_CW):
            cs = slice(c * ATT_CW, (c + 1) * ATT_CW)
            rows = visible_rows(diag, c)
            if rows:
                s = lax.dot_general(kb[:rows], qs_ref[cs, :], (((1,), (1,)), ((), ())),
                                    preferred_element_type=F32)
            if rows and diag is not None:
                krow = lax.broadcasted_iota(jnp.int32, s.shape, 0) + diag * ATT_TK
                qcol = (lax.broadcasted_iota(jnp.int32, s.shape, 1) + c * ATT_CW) % ATT_TQ
                s = jnp.where(krow // CHUNK <= qcol // CHUNK, s, -jnp.inf)
            pv_update(c, vt, alpha_prev, visible_rows(prev_diag, c))
            if not rows:
                ms.append(m[:, cs])
                ls.append(l[:, cs])
                alphas.append(jnp.ones_like(m[:, cs]))
                continue
            m_new = jnp.maximum(m[:, cs], jnp.max(s, axis=0, keepdims=True))
            alpha = jnp.exp2(m[:, cs] - m_new)
            p = jnp.exp2(s - m_new)
            ls.append(alpha * l[:, cs] + jnp.sum(p, axis=0, keepdims=True))
            p_ref[c, :rows, :] = p.astype(BF16)
            ms.append(m_new)
            alphas.append(alpha)
        return jnp.concatenate(ms, axis=1), jnp.concatenate(ls, axis=1), jnp.concatenate(alphas, axis=1)

    def trip(t, carry, masked):
        for d in range(ATT_KB):
            diag = d if masked else None
            prev_diag = d - 1 if masked and d > 0 else None
            carry = block(ATT_KB * t + d, carry, diag, prev_diag)
        return carry

    init = (jnp.full((1, nq), -jnp.inf, F32), jnp.zeros((1, nq), F32), jnp.ones((1, nq), F32))
    carry = lax.fori_loop(0, i, lambda t, c: trip(t, c, False), init)
    m, l, alpha = trip(i, carry, True)
    last = ATT_KB * i + ATT_KB - 1
    vt = vt_ref[last]
    for c in range(nq // ATT_CW):
        pv_update(c, vt, alpha, visible_rows(ATT_KB - 1, c))
    o = jnp.concatenate([acc_ref[c] for c in range(nq // ATT_CW)], axis=1) / l
    o = o[:, :ATT_TQ] - lam_ref[...] * o[:, ATT_TQ:]
    o = o * lax.rsqrt(jnp.mean(o * o, axis=0, keepdims=True) + RMS_EPS) * g_ref[...]
    o_ref[...] = (o * (1.0 - lambda_init)).T


def _attention(qr, kr, vt, lam, subln_g, lambda_init, bsz, seq):
    nq = seq // ATT_TQ
    assert vt.shape == (bsz * DA_HEADS, seq // ATT_TK, DA_V_DIM, ATT_TK)
    col = pl.BlockSpec((LANES, 1), lambda b, h, i: (0, 0))
    qo = pl.BlockSpec((ATT_TQ, LANES), lambda b, h, i: (b * nq + i, h))
    return pl.pallas_call(
        functools.partial(_attn_kernel, lambda_init=lambda_init),
        out_shape=jax.ShapeDtypeStruct((bsz * seq, DA_WIDTH), F32),
        grid=(bsz, DA_HEADS, nq),
        in_specs=[pl.BlockSpec((1, 1), lambda b, h, i: (0, 0)), col, qo,
                  pl.BlockSpec((seq, LANES), lambda b, h, i: (b, h)),
                  pl.BlockSpec((None, seq // ATT_TK, DA_V_DIM, ATT_TK), lambda b, h, i: (b * DA_HEADS + h, 0, 0, 0))],
        out_specs=qo,
        scratch_shapes=[pltpu.VMEM((2 * ATT_TQ, LANES), BF16),
                        pltpu.VMEM((2 * ATT_TQ // ATT_CW, ATT_TK, ATT_CW), BF16),
                        pltpu.VMEM((2 * ATT_TQ // ATT_CW, DA_V_DIM, ATT_CW), F32)],
        compiler_params=_cp("parallel", "parallel", "arbitrary"),
        name="diff_attention",
    )(lam.reshape(1, 1), subln_g.reshape(LANES, 1), qr, kr, vt)


RW_VLO = 4
RW_VHI = RW_HEAD_DIM // RW_VLO
RW_BH = LANES // RW_VLO
RW_TB = 128
RW_PITCH = RW_HEAD_DIM + SUBLANES
RW_TT = 64
RW_NACC = 4
N_KOPS = 4
RW_NORM = 64


def _rw_prep_kernel(p_ref, halo_ref, mu_ref, w0_ref, a0_ref, kk_ref, ka_ref, rk_ref,
                    w2_ref, a2_ref, g2_ref, seg_ref, tri_ref,
                    kop_ref, v_ref, dend_ref, g_ref, bonus_ref, tr_ref, de_ref, *, n_blocks):
    i = pl.program_id(0)
    n_op = pl.program_id(1)
    bsz = p_ref.shape[0]
    blk = jnp.minimum(i, n_blocks - 1)
    cur = i % 2
    prv = 1 - cur
    first = jnp.where(blk == 0, 0.0, 1.0)
    seg = seg_ref[...]
    tri = tri_ref[...]

    @pl.when((i == 0) & (n_op == 0))
    def _():
        tr_ref[1] = jnp.zeros(tr_ref.shape[1:], F32)
        de_ref[...] = jnp.zeros_like(de_ref)

    def compute(b):
        p = p_ref[b]
        row = lax.broadcasted_iota(jnp.int32, p.shape, 0)
        last_prev = jnp.broadcast_to(halo_ref[b, SUBLANES - 1:SUBLANES, :], p.shape) * first
        prev = jnp.where(row == 0, last_prev, pltpu.roll(p, 1, axis=0))
        p = p + (prev - p) * mu_ref[...]
        r = p[:, 0:RW_WIDTH]
        k = p[:, RW_WIDTH:2 * RW_WIDTH]
        v = p[:, 2 * RW_WIDTH:3 * RW_WIDTH]
        low = p[:, 3 * RW_WIDTH:RW_IN]
        log_w = -RW_DECAY_SCALE * jax.nn.sigmoid(w0_ref[...] + _dot_split(jnp.tanh(low), w2_ref))
        a = jax.nn.sigmoid(a0_ref[...] + _dot_split(low, a2_ref))
        g_ref[b] = _dot_split(jax.nn.sigmoid(low), g2_ref)
        kk = k * kk_ref[...]
        kk = kk * lax.rsqrt(jnp.maximum(_segment_sum(kk * kk, seg), 1e-24))
        k = k * (1.0 + (a - 1.0) * ka_ref[...])
        bonus_ref[b] = _segment_sum(r * k * rk_ref[...], seg) * v
        log_d = sum(jnp.dot(tri, piece, preferred_element_type=F32) for piece in _split3(log_w))
        d_inc = jnp.exp(log_d)
        d_inv = jnp.exp(-log_d)
        d_exc = jnp.exp(log_d - log_w)
        for n, val in enumerate((r * d_inc, k * d_inv, -kk * d_exc, kk * a * d_inv, v)):
            vt = val.T
            for h in range(RW_HEADS):
                base = pl.multiple_of((b * RW_HEADS + h) * RW_PITCH, SUBLANES)
                tr_ref[cur, n, pl.ds(base, RW_HEAD_DIM), :] = vt[h * RW_HEAD_DIM:(h + 1) * RW_HEAD_DIM, :]
        for e in range(RW_TB // RW_NORM):
            end = d_inc[(e + 1) * RW_NORM - 1:(e + 1) * RW_NORM, :]
            for h in range(RW_HEADS):
                de_ref[cur, e, pl.ds(b * RW_HEADS + h, 1), 0:RW_HEAD_DIM] = end[:, h * RW_HEAD_DIM:(h + 1) * RW_HEAD_DIM]

    def v_relayout():
        for vh in range(RW_VHI):
            rows = [tr_ref[cur, N_KOPS, pl.ds(vh * RW_VLO + vl, RW_BH, stride=RW_PITCH), :]
                    for vl in range(RW_VLO)]
            v_ref[pl.ds(vh, RW_TB, stride=RW_VHI), :] = jnp.concatenate(rows, axis=0).T
        for e in range(RW_TB // RW_NORM):
            tile = jnp.concatenate([de_ref[cur, e]] * RW_VLO, axis=0).T
            dend_ref[e] = tile[:RW_HEAD_DIM, :]

    def emit():
        for kk_i in range(RW_HEAD_DIM):
            rows = tr_ref[prv, n_op, pl.ds(kk_i, RW_BH, stride=RW_PITCH), :]
            tile = jnp.concatenate([rows] * RW_VLO, axis=0).T
            kop_ref[0, :, kk_i] = tile.reshape(RW_TB // SUBLANES, SUBLANES, LANES)

    per_step = bsz // N_KOPS

    @pl.when(n_op < N_KOPS - 1)
    def _():
        emit()
        for s in range(per_step):
            compute(n_op * per_step + s)

    @pl.when(n_op == N_KOPS - 1)
    def _():
        emit()
        for s in range(per_step):
            compute(n_op * per_step + s)
        v_relayout()


def _head_segments():
    idx = np.arange(RW_WIDTH) // RW_HEAD_DIM
    return jnp.asarray((idx[:, None] == idx[None, :]).astype(np.float32)).astype(BF16)


def _rw_prep(proj3, mu, w0, w2, a0, a2, g2, k_k, k_a, r_k):
    bsz, seq, _ = proj3.shape
    hb = RW_TB // SUBLANES
    w2p = _hi_lo(jnp.zeros((LANES, RW_WIDTH), F32).at[0:32].set(w2))
    a2p = _hi_lo(jnp.zeros((LANES, RW_WIDTH), F32).at[32:64].set(a2))
    g2p = _hi_lo(jnp.zeros((LANES, RW_WIDTH), F32).at[64:128].set(g2))
    lowrank = pl.BlockSpec((2, LANES, RW_WIDTH), lambda i, j: (0, 0, 0))
    vec = lambda n: pl.BlockSpec((1, n), lambda i, j: (0, 0))
    mat = lambda r, c: pl.BlockSpec((r, c), lambda i, j: (0, 0))
    n_blocks = seq // RW_TB
    assert bsz % N_KOPS == 0 and RW_TB % RW_NORM == 0
    t_idx = np.arange(RW_TB)
    tri = jnp.asarray(((t_idx[:, None] >= t_idx[None, :])
                       & (t_idx[:, None] // RW_NORM == t_idx[None, :] // RW_NORM)).astype(np.float32)).astype(BF16)
    blk = lambda i: jnp.minimum(i, n_blocks - 1)
    tok = pl.BlockSpec((bsz, RW_TB, RW_WIDTH), lambda i, j: (0, blk(i), 0))
    tshape = jax.ShapeDtypeStruct((bsz, seq, RW_WIDTH), F32)
    return pl.pallas_call(
        functools.partial(_rw_prep_kernel, n_blocks=n_blocks),
        out_shape=(jax.ShapeDtypeStruct((N_KOPS, seq // SUBLANES, RW_HEAD_DIM, SUBLANES, LANES), F32),
                   jax.ShapeDtypeStruct((seq * RW_VHI, LANES), F32),
                   jax.ShapeDtypeStruct((seq // RW_NORM, RW_HEAD_DIM, LANES), F32), tshape, tshape),
        grid=(n_blocks + 1, N_KOPS),
        in_specs=[pl.BlockSpec((bsz, RW_TB, RW_IN), lambda i, j: (0, blk(i), 0)),
                  pl.BlockSpec((bsz, SUBLANES, RW_IN),
                               lambda i, j: (0, jnp.maximum(blk(i) * hb - 1, 0), 0)),
                  vec(RW_IN), vec(RW_WIDTH), vec(RW_WIDTH), vec(RW_WIDTH), vec(RW_WIDTH), vec(RW_WIDTH),
                  lowrank, lowrank, lowrank, mat(RW_WIDTH, RW_WIDTH), mat(RW_TB, RW_TB)],
        out_specs=(pl.BlockSpec((1, RW_TB // SUBLANES, RW_HEAD_DIM, SUBLANES, LANES),
                                lambda i, j: (j, jnp.maximum(i - 1, 0), 0, 0, 0)),
                   pl.BlockSpec((RW_TB * RW_VHI, LANES), lambda i, j: (blk(i), 0)),
                   pl.BlockSpec((RW_TB // RW_NORM, RW_HEAD_DIM, LANES), lambda i, j: (blk(i), 0, 0)), tok, tok),
        scratch_shapes=[pltpu.VMEM((2, N_KOPS + 1, RW_BH * RW_PITCH, RW_TB), F32),
                        pltpu.VMEM((2, RW_TB // RW_NORM, RW_BH, LANES), F32)],
        compiler_params=_cp("arbitrary", "arbitrary"),
        name="rwkv_prep",
    )(proj3, proj3, mu.reshape(1, RW_IN), w0.reshape(1, -1), a0.reshape(1, -1), k_k.reshape(1, -1),
      k_a.reshape(1, -1), r_k.reshape(1, -1), w2p, a2p, g2p, _head_segments(), tri)


def _rw_scan_kernel(r_ref, k_ref, an_ref, b_ref, v_ref, dend_ref, y_ref, s_ref):
    @pl.when(pl.program_id(0) == 0)
    def _():
        s_ref[...] = jnp.zeros_like(s_ref)

    nvb = RW_VHI // SUBLANES
    vsl = [slice(vb * SUBLANES, (vb + 1) * SUBLANES) for vb in range(nvb)]
    zeros = lambda: [[jnp.zeros((SUBLANES, LANES), F32) for _ in range(RW_NACC)] for _ in range(nvb)]
    total = lambda acc: [sum(acc[vb][1:], acc[vb][0]) for vb in range(nvb)]

    def body(tb, _):
        def row(ref, kk, ts):
            return jnp.broadcast_to(ref[tb, kk, ts:ts + 1, :], (SUBLANES, LANES))

        zacc = zeros()
        for kk in range(RW_HEAD_DIM):
            an = row(an_ref, kk, 0)
            for vb in range(nvb):
                zacc[vb][kk % RW_NACC] += s_ref[kk, vsl[vb], :] * an
        z = total(zacc)
        for ts in range(SUBLANES):
            t = tb * SUBLANES + ts
            vt = [v_ref[t, vsl[vb], :] for vb in range(nvb)]
            yacc, zacc = zeros(), zeros()
            for kk in range(RW_HEAD_DIM):
                br, kr, rr = row(b_ref, kk, ts), row(k_ref, kk, ts), row(r_ref, kk, ts)
                an = row(an_ref, kk, ts + 1) if ts + 1 < SUBLANES else None
                for vb in range(nvb):
                    s_new = s_ref[kk, vsl[vb], :] + z[vb] * br + vt[vb] * kr
                    s_ref[kk, vsl[vb], :] = s_new
                    yacc[vb][kk % RW_NACC] += s_new * rr
                    if an is not None:
                        zacc[vb][kk % RW_NACC] += s_new * an
            for vb, y in enumerate(total(yacc)):
                y_ref[t, vsl[vb], :] = y
            z = total(zacc)
        return 0

    lax.fori_loop(0, RW_TT // SUBLANES, body, 0)
    for kk in range(RW_HEAD_DIM):
        d_end = jnp.broadcast_to(dend_ref[0, kk:kk + 1, :], (SUBLANES, LANES))
        for vb in range(nvb):
            s_ref[kk, vsl[vb], :] = s_ref[kk, vsl[vb], :] * d_end


def _rw_scan(kops, v4, dend, seq):
    assert RW_TT == RW_NORM
    kspec = lambda n: pl.BlockSpec((None, RW_TT // SUBLANES, RW_HEAD_DIM, SUBLANES, LANES),
                                   lambda i: (n, i, 0, 0, 0))
    vspec = pl.BlockSpec((RW_TT, RW_VHI, LANES), lambda i: (i, 0, 0))
    return pl.pallas_call(
        _rw_scan_kernel,
        out_shape=jax.ShapeDtypeStruct((seq, RW_VHI, LANES), F32),
        grid=(seq // RW_TT,),
        in_specs=[kspec(n) for n in range(N_KOPS)] + [vspec, pl.BlockSpec((1, RW_HEAD_DIM, LANES), lambda i: (i, 0, 0))],
        out_specs=vspec,
        scratch_shapes=[pltpu.VMEM((RW_HEAD_DIM, RW_VHI, LANES), F32)],
        compiler_params=_cp("arbitrary"),
        name="rwkv_scan",
    )(*([kops] * N_KOPS), v4.reshape(seq, RW_VHI, LANES), dend)


def _rw_post_kernel(y_ref, bonus_ref, gate_ref, lg_ref, lb_ref, seg_ref, o_ref, tr_ref):
    bsz = bonus_ref.shape[0]

    for vh in range(RW_VHI):
        yt = y_ref[pl.ds(vh, RW_TB, stride=RW_VHI), :].T
        for vl in range(RW_VLO):
            tr_ref[pl.ds(vh * RW_VLO + vl, RW_BH, stride=RW_PITCH), :] = yt[vl * RW_BH:(vl + 1) * RW_BH, :]

    seg = seg_ref[...]
    head_mean = lambda a: _segment_sum(a, seg) * (1.0 / RW_HEAD_DIM)

    for b in range(bsz):
        slabs = [tr_ref[pl.ds((b * RW_HEADS + h) * RW_PITCH, RW_HEAD_DIM), :] for h in range(RW_HEADS)]
        y = jnp.concatenate(slabs, axis=0).T
        c = y - head_mean(y)
        var = head_mean(c * c)
        y = c * lax.rsqrt(var + RW_LNX_EPS) * lg_ref[...] + lb_ref[...]
        o_ref[b] = (y + bonus_ref[b]) * gate_ref[b]


def _rw_post(y4, bonus, gate, lnx_g, lnx_b):
    bsz, seq, _ = bonus.shape
    tok = pl.BlockSpec((bsz, RW_TB, RW_WIDTH), lambda i: (0, i, 0))
    vec = pl.BlockSpec((1, RW_WIDTH), lambda i: (0, 0))
    return pl.pallas_call(
        _rw_post_kernel,
        out_shape=jax.ShapeDtypeStruct((bsz, seq, RW_WIDTH), F32),
        grid=(seq // RW_TB,),
        in_specs=[pl.BlockSpec((RW_TB * RW_VHI, LANES), lambda i: (i, 0)), tok, tok, vec, vec,
                  pl.BlockSpec((RW_WIDTH, RW_WIDTH), lambda i: (0, 0))],
        out_specs=tok,
        scratch_shapes=[pltpu.VMEM((RW_BH * RW_PITCH, RW_TB), F32)],
        compiler_params=_cp("parallel"),
        name="rwkv_post",
    )(y4.reshape(seq * RW_VHI, LANES), bonus, gate, lnx_g.reshape(1, -1), lnx_b.reshape(1, -1), _head_segments())


S5_TT = 256
S5_PITCH = S5_TT + SUBLANES


def _s5_kernel(u_ref, wb_ref, wc_ref, a_ref, d_ref, gw_ref, gb_ref, og_ref, o_ref, x_ref, carry_ref):
    @pl.when(pl.program_id(0) == 0)
    def _():
        carry_ref[...] = jnp.zeros_like(carry_ref)

    n = S5_CPLX
    nc = n // LANES
    bsz = u_ref.shape[0]

    def project(b, _):
        base = b * S5_PITCH
        bu = jnp.dot(u_ref[b].astype(BF16), wb_ref[...], preferred_element_type=F32)
        for c in range(2 * nc):
            x_ref[c, pl.ds(base, S5_TT), :] = bu[:, c * LANES:(c + 1) * LANES]
        return 0
    for b in range(bsz):
        project(b, 0)

    ar = [jnp.broadcast_to(a_ref[:, c * LANES:(c + 1) * LANES], (bsz, LANES)) for c in range(nc)]
    ai = [jnp.broadcast_to(a_ref[:, n + c * LANES:n + (c + 1) * LANES], (bsz, LANES)) for c in range(nc)]

    def steps(tb, carry):
        cr, ci = list(carry[0]), list(carry[1])
        for ts in range(SUBLANES):
            rows = pl.ds(tb * SUBLANES + ts, bsz, stride=S5_PITCH)
            for c in range(nc):
                cr[c], ci[c] = (ar[c] * cr[c] - ai[c] * ci[c] + x_ref[c, rows, :],
                                ar[c] * ci[c] + ai[c] * cr[c] + x_ref[nc + c, rows, :])
                x_ref[c, rows, :] = cr[c]
                x_ref[nc + c, rows, :] = ci[c]
        return tuple(cr), tuple(ci)

    init = (tuple(carry_ref[c] for c in range(nc)), tuple(carry_ref[nc + c] for c in range(nc)))
    cr, ci = lax.fori_loop(0, S5_TT // SUBLANES, steps, init)
    for c in range(nc):
        carry_ref[c] = cr[c]
        carry_ref[nc + c] = ci[c]

    def readout(b, _):
        base = b * S5_PITCH
        u = u_ref[b]
        x = jnp.concatenate([x_ref[c, pl.ds(base, S5_TT), :] for c in range(2 * nc)], axis=1)
        y = jnp.dot(x.astype(BF16), wc_ref[...], preferred_element_type=F32) + d_ref[...] * u
        z = jax.nn.gelu(y)
        z = z * jax.nn.sigmoid(jnp.dot(z.astype(BF16), gw_ref[...], preferred_element_type=F32) + gb_ref[...])
        o_ref[b] = z * lax.rsqrt(jnp.mean(z * z, axis=-1, keepdims=True) + RMS_EPS) * og_ref[...]
        return 0
    for b in range(bsz):
        readout(b, 0)


def _s5_tables(a_re, a_im, b_re, b_im, c_re, c_im, log_step):
    delta = jnp.exp(log_step.astype(F32))[:, None]
    mag = jnp.exp(delta * a_re)

    def a_pow(n):
        return (mag ** n) * jnp.cos(n * delta * a_im), (mag ** n) * jnp.sin(n * delta * a_im)

    abr, abi = a_pow(1)
    den = a_re * a_re + a_im * a_im
    qr = ((abr - 1.0) * a_re + abi * a_im) / den
    qi = (abi * a_re - (abr - 1.0) * a_im) / den
    bbr = qr[..., None] * b_re - qi[..., None] * b_im
    bbi = qr[..., None] * b_im + qi[..., None] * b_re
    eye = jnp.eye(S5_GROUPS, dtype=F32)
    blk_b = lambda m: jnp.einsum('gph,gk->ghkp', m, eye).reshape(S5_WIDTH, S5_CPLX)
    blk_c = lambda m: jnp.einsum('ghp,gk->gpkh', m, eye).reshape(S5_CPLX, S5_WIDTH)
    wb = jnp.concatenate([blk_b(bbr), blk_b(bbi)], axis=1).astype(BF16)
    wc = jnp.concatenate([blk_c(c_re), -blk_c(c_im)], axis=0).astype(BF16)
    a_row = jnp.concatenate([abr.reshape(1, -1), abi.reshape(1, -1)], axis=1)
    return wb, wc, a_row


def _s5(proj3, tables, d_skip, glu_w, glu_b, out_g):
    wb, wc, a_row = tables
    bsz, seq, _ = proj3.shape
    full = lambda a: pl.BlockSpec(a.shape, lambda i: (0,) * a.ndim)
    vec = pl.BlockSpec((1, S5_WIDTH), lambda i: (0, 0))
    gw = glu_w.astype(BF16)
    return pl.pallas_call(
        _s5_kernel,
        out_shape=jax.ShapeDtypeStruct((bsz, seq, S5_WIDTH), F32),
        grid=(seq // S5_TT,),
        in_specs=[pl.BlockSpec((bsz, S5_TT, S5_WIDTH), lambda i: (0, i, 0)),
                  full(wb), full(wc), full(a_row), vec, full(gw), vec, vec],
        out_specs=pl.BlockSpec((bsz, S5_TT, S5_WIDTH), lambda i: (0, i, 0)),
        scratch_shapes=[pltpu.VMEM((2 * S5_CPLX // LANES, bsz * S5_PITCH, LANES), F32),
                        pltpu.VMEM((2 * S5_CPLX // LANES, bsz, LANES), F32)],
        compiler_params=_cp("arbitrary"),
        name="s5",
    )(proj3, wb, wc, a_row, d_skip.reshape(1, -1), gw, glu_b.reshape(1, -1), out_g.reshape(1, -1))


def _xattn_kernel(hda_ref, hrw_ref, hs5_ref, x_ref, wda_ref, wrw_ref, ws5_ref, g1_ref, b1_ref,
                  kv_ref, wq_ref, wo_ref, g2_ref, b2_ref, o_ref):
    mix = ALPHA * x_ref[...]
    for h_ref, w_ref in ((hda_ref, wda_ref), (hrw_ref, wrw_ref), (hs5_ref, ws5_ref)):
        mix = mix + jnp.dot(h_ref[...].astype(BF16), w_ref[...], preferred_element_type=F32)
    x = _layer_norm_rows(mix, g1_ref[...], b1_ref[...])
    q = jnp.dot(x.astype(BF16), wq_ref[...], preferred_element_type=F32)
    outs = []
    for h in range(MEM_HEADS):
        sl = slice(h * MEM_HEAD_DIM, (h + 1) * MEM_HEAD_DIM)
        kh = kv_ref[:, sl]
        vh = kv_ref[:, D_MODEL + h * MEM_HEAD_DIM:D_MODEL + (h + 1) * MEM_HEAD_DIM]
        s = lax.dot_general(q[:, sl].astype(BF16), kh, (((1,), (1,)), ((), ())),
                            preferred_element_type=F32) * (MEM_HEAD_DIM ** -0.5)
        p = jnp.exp(s - jnp.max(s, axis=-1, keepdims=True))
        p = p / jnp.sum(p, axis=-1, keepdims=True)
        outs.append(jnp.dot(p.astype(BF16), vh, preferred_element_type=F32))
    o = jnp.concatenate(outs, axis=-1)
    h_out = jnp.dot(o.astype(BF16), wo_ref[...], preferred_element_type=F32)
    o_ref[...] = _layer_norm_rows(ALPHA * x + h_out, g2_ref[...], b2_ref[...])


def _mix_out_cross_attention(h_list, w_list, x, g1, b1, kv, wq, wo, g2, b2, seq, tm):
    t, d = x.shape
    n_mem = kv.shape[0] // (t // seq)
    tiles_per_seq = seq // tm
    const = lambda a: pl.BlockSpec(a.shape, lambda i: (0, 0))
    vec = pl.BlockSpec((1, d), lambda i: (0, 0))
    row = lambda n: pl.BlockSpec((tm, n), lambda i: (i, 0))
    return pl.pallas_call(
        _xattn_kernel,
        out_shape=jax.ShapeDtypeStruct((t, d), F32),
        grid=(t // tm,),
        in_specs=[row(h.shape[1]) for h in h_list] + [row(d)] + [const(w) for w in w_list] + [vec, vec]
        + [pl.BlockSpec((n_mem, 2 * d), lambda i: (i // tiles_per_seq, 0)), const(wq), const(wo), vec, vec],
        out_specs=row(d),
        compiler_params=_cp("parallel"),
        name="mix_out_cross_attention",
    )(*h_list, x, *w_list, g1.reshape(1, d), b1.reshape(1, d), kv, wq, wo, g2.reshape(1, d), b2.reshape(1, d))


FFN_TN = 1408
FFN_TM = 1024


def _ffn_up_kernel(x_ref, halo_ref, wa_ref, wg_ref, cw_ref, cb_ref, o_ref, *, tiles_per_seq):
    i = pl.program_id(1)
    x = x_ref[...].astype(BF16)
    first = jnp.where(i % tiles_per_seq == 0, 0.0, 1.0)
    xh = (halo_ref[...] * first).astype(BF16)
    a8 = jnp.dot(jnp.concatenate([xh, x], axis=0), wa_ref[...], preferred_element_type=F32)
    a0 = a8[SUBLANES:]
    a1 = pltpu.roll(a8, 1, axis=0)[SUBLANES:]
    a2 = pltpu.roll(a8, 2, axis=0)[SUBLANES:]
    conv = cb_ref[...] + cw_ref[0:1, :] * a2 + cw_ref[1:2, :] * a1 + cw_ref[2:3, :] * a0
    g = jnp.dot(x, wg_ref[...], preferred_element_type=F32)
    o_ref[...] = (jax.nn.silu(conv) * g).astype(o_ref.dtype)


def _ffn_up(x, wa, wg, conv_w, conv_b, seq, tm):
    t, d = x.shape
    hb = tm // SUBLANES
    cw = jnp.zeros((SUBLANES, D_FF), F32).at[0:3].set(conv_w)
    wspec = pl.BlockSpec((d, FFN_TN), lambda j, i: (0, j))
    return pl.pallas_call(
        functools.partial(_ffn_up_kernel, tiles_per_seq=seq // tm),
        out_shape=jax.ShapeDtypeStruct((t, D_FF), BF16),
        grid=(D_FF // FFN_TN, t // tm),
        in_specs=[pl.BlockSpec((tm, d), lambda j, i: (i, 0)),
                  pl.BlockSpec((SUBLANES, d), lambda j, i: (jnp.maximum(i * hb - 1, 0), 0)),
                  wspec, wspec,
                  pl.BlockSpec((SUBLANES, FFN_TN), lambda j, i: (0, j)),
                  pl.BlockSpec((1, FFN_TN), lambda j, i: (0, j))],
        out_specs=pl.BlockSpec((tm, FFN_TN), lambda j, i: (i, j)),
        compiler_params=_cp("arbitrary", "parallel"),
        name="ffn_up",
    )(x, x, wa, wg, cw, conv_b.reshape(1, D_FF))


def kernel(x, mem, positions, w_in, da_lam_q1, da_lam_k1, da_lam_q2, da_lam_k2, da_subln_g, rw_mu, rw_w0, rw_w2, rw_a0, rw_a2, rw_g2, rw_k_k, rw_k_a, rw_r_k, rw_lnx_g, rw_lnx_b, s5_a_re, s5_a_im, s5_b_re, s5_b_im, s5_c_re, s5_c_im, s5_d, s5_log_step, s5_glu_w, s5_glu_b, s5_out_g, w_out, ln1_g, ln1_b, ca_wq, ca_wkv, ca_wo, ln2_g, ln2_b, ffn_w_up, ffn_conv_w, ffn_conv_b, ffn_w_down, ln3_g, ln3_b):
    bsz, seq, d = x.shape
    t = bsz * seq
    tm = min(512, seq)
    n_mem = mem.shape[1]
    xf = x.reshape(t, d)
    memf = mem.reshape(bsz * n_mem, d)
    cos_t, sin_t = _rope_tables(positions, tm)

    for l in range(DEPTH):
        lambda_init = 0.8 - 0.6 * math.exp(-0.3 * l)
        wi = w_in[l]
        w_perm = jnp.concatenate([wi[:, :3 * DA_WIDTH], wi[:, 3 * DA_WIDTH + RW_IN:],
                                  wi[:, 3 * DA_WIDTH:3 * DA_WIDTH + RW_IN]], axis=1).astype(BF16)
        qr, kr, vt, p_rw, p_s5 = _in_proj(xf, w_perm, cos_t, sin_t, seq, tm)
        lam = (jnp.exp(jnp.sum(da_lam_q1[l] * da_lam_k1[l])) - jnp.exp(jnp.sum(da_lam_q2[l] * da_lam_k2[l]))
               + lambda_init)
        kops, v4, dend, gate, bonus = _rw_prep(p_rw.reshape(bsz, seq, RW_IN), rw_mu[l], rw_w0[l], rw_w2[l],
                                               rw_a0[l], rw_a2[l], rw_g2[l], rw_k_k[l], rw_k_a[l],
                                               rw_r_k[l].reshape(-1))
        h_da = _attention(qr, kr, vt, lam, da_subln_g[l], lambda_init, bsz, seq)
        y4 = _rw_scan(kops, v4, dend, seq)
        h_rw = _rw_post(y4, bonus, gate, rw_lnx_g[l], rw_lnx_b[l]).reshape(t, RW_WIDTH)

        tables = _s5_tables(s5_a_re[l], s5_a_im[l], s5_b_re[l], s5_b_im[l], s5_c_re[l], s5_c_im[l],
                            s5_log_step[l])
        h_s5 = _s5(p_s5.reshape(bsz, seq, S5_WIDTH), tables, s5_d[l], s5_glu_w[l], s5_glu_b[l],
                   s5_out_g[l]).reshape(t, S5_WIDTH)

        wo = w_out[l].astype(BF16)
        kv = _matmul(memf, ca_wkv[l].astype(BF16), BF16, n_mem, 1024)
        xf = _mix_out_cross_attention(
            [h_da, h_rw, h_s5], [wo[:DA_WIDTH], wo[DA_WIDTH:DA_WIDTH + RW_WIDTH], wo[DA_WIDTH + RW_WIDTH:]],
            xf, ln1_g[l], ln1_b[l], kv, ca_wq[l].astype(BF16), ca_wo[l].astype(BF16), ln2_g[l], ln2_b[l], seq, tm)
        w_up = ffn_w_up[l].astype(BF16)
        hff = _ffn_up(xf, w_up[:, :D_FF], w_up[:, D_FF:], ffn_conv_w[l], ffn_conv_b[l], seq, min(FFN_TM, seq))
        xf = _matmul_ln([hff], [ffn_w_down[l].astype(BF16)], xf, ln3_g[l], ln3_b[l], tm)
    return xf.reshape(bsz, seq, d)
```

```python
import functools
import math

import numpy as np
import jax
import jax.numpy as jnp
from jax import lax
from jax.experimental import pallas as pl
from jax.experimental.pallas import tpu as pltpu

F32 = jnp.float32
BF16 = jnp.bfloat16
HI = lax.Precision.HIGHEST

D_MODEL = 1024
DEPTH = 4
CHUNK = 64
LN_EPS = 1e-5
LOG2E = math.log2(math.e)
RMS_EPS = 1e-6

DA_HEAD_DIM = 64
DA_V_DIM = 128
DA_WIDTH = 512
DA_HEADS = 4
ROPE_THETA = 10000.0

RW_HEAD_DIM = 64
RW_WIDTH = 256
RW_HEADS = 4
RW_DECAY_RANK = 32
RW_AAA_RANK = 32
RW_GATE_RANK = 64
RW_IN = 3 * RW_WIDTH + RW_DECAY_RANK + RW_AAA_RANK + RW_GATE_RANK
RW_DECAY_SCALE = math.exp(-0.5)
RW_LNX_EPS = 64e-5

S5_WIDTH = 256
S5_GROUP_CH = 16
S5_GROUPS = 16
S5_STATE = 64
S5_CPLX = S5_GROUPS * S5_STATE

IN_WIDTH = 3 * DA_WIDTH + RW_IN + S5_WIDTH
MEM_HEADS = 4
MEM_HEAD_DIM = 256
D_FF = 2816
ALPHA = (2.0 * DEPTH) ** 0.25

COL_Q, COL_K, COL_V, COL_S5, COL_RW = 0, 512, 1024, 1536, 1792

VMEM_LIMIT = 48 * 1024 * 1024
LANES = 128
SUBLANES = 8


def _cp(*sem):
    return pltpu.CompilerParams(dimension_semantics=sem, vmem_limit_bytes=VMEM_LIMIT)


def _layer_norm_rows(v, g, b):
    mu = jnp.mean(v, axis=-1, keepdims=True)
    c = v - mu
    var = jnp.mean(c * c, axis=-1, keepdims=True)
    return c * lax.rsqrt(var + LN_EPS) * g + b


def _split3(x):
    p1 = x.astype(BF16)
    r1 = x - p1.astype(F32)
    p2 = r1.astype(BF16)
    p3 = (r1 - p2.astype(F32)).astype(BF16)
    return p1, p2, p3


def _segment_sum(x, seg):
    return sum(jnp.dot(p, seg, preferred_element_type=F32) for p in _split3(x))


def _dot_split(a, w_ref):
    a1, a2, _ = _split3(a)
    d = functools.partial(jnp.dot, preferred_element_type=F32)
    return d(a1, w_ref[0]) + (d(a1, w_ref[1]) + d(a2, w_ref[0]))


def _hi_lo(w):
    hi = w.astype(BF16)
    return jnp.stack([hi, (w - hi.astype(F32)).astype(BF16)])


def _mm_kernel(a_ref, w_ref, o_ref):
    o_ref[...] = jnp.dot(a_ref[...].astype(BF16), w_ref[...],
                         preferred_element_type=F32).astype(o_ref.dtype)


def _matmul(a, w, out_dtype, tm, tn):
    m, k = a.shape
    n = w.shape[1]
    return pl.pallas_call(
        _mm_kernel,
        out_shape=jax.ShapeDtypeStruct((m, n), out_dtype),
        grid=(m // tm, n // tn),
        in_specs=[pl.BlockSpec((tm, k), lambda i, j: (i, 0)),
                  pl.BlockSpec((k, tn), lambda i, j: (0, j))],
        out_specs=pl.BlockSpec((tm, tn), lambda i, j: (i, j)),
        compiler_params=_cp("parallel", "arbitrary"),
        name="matmul",
    )(a, w)


def _mm_ln_kernel(n_in, *refs):
    a_refs = refs[:n_in]
    w_refs = refs[n_in:2 * n_in]
    x_ref, g_ref, b_ref, o_ref = refs[2 * n_in:]
    acc = ALPHA * x_ref[...]
    for a_ref, w_ref in zip(a_refs, w_refs):
        acc = acc + jnp.dot(a_ref[...].astype(BF16), w_ref[...], preferred_element_type=F32)
    o_ref[...] = _layer_norm_rows(acc, g_ref[...], b_ref[...])


def _matmul_ln(a_list, w_list, x, g, b, tm):
    m, d = x.shape
    n_in = len(a_list)
    in_specs = [pl.BlockSpec((tm, a.shape[1]), lambda i: (i, 0)) for a in a_list]
    in_specs += [pl.BlockSpec(w.shape, lambda i: (0, 0)) for w in w_list]
    in_specs += [pl.BlockSpec((tm, d), lambda i: (i, 0)),
                 pl.BlockSpec((1, d), lambda i: (0, 0)),
                 pl.BlockSpec((1, d), lambda i: (0, 0))]
    return pl.pallas_call(
        functools.partial(_mm_ln_kernel, n_in),
        out_shape=jax.ShapeDtypeStruct((m, d), F32),
        grid=(m // tm,),
        in_specs=in_specs,
        out_specs=pl.BlockSpec((tm, d), lambda i: (i, 0)),
        compiler_params=_cp("parallel"),
        name="matmul_ln",
    )(*a_list, *w_list, x, g.reshape(1, d), b.reshape(1, d))


def _rope_table_kernel(pos_ref, freq_ref, sign_ref, cos_ref, sin_ref):
    ang = pos_ref[...] * freq_ref[...]
    cos_ref[...] = jnp.cos(ang)
    sin_ref[...] = jnp.sin(ang) * sign_ref[...]


def _rope_tables(positions, tm):
    t = positions.size
    inv_freq = ROPE_THETA ** (-jnp.arange(0, DA_HEAD_DIM, 2, dtype=F32) / DA_HEAD_DIM)
    freq_row = jnp.tile(inv_freq, 4).reshape(1, LANES)
    sign_row = jnp.tile(jnp.concatenate([-jnp.ones((32,), F32), jnp.ones((32,), F32)]), 2).reshape(1, LANES)
    pos_col = positions.astype(F32).reshape(t, 1)
    row = pl.BlockSpec((1, LANES), lambda i: (0, 0))
    return pl.pallas_call(
        _rope_table_kernel,
        out_shape=(jax.ShapeDtypeStruct((t, LANES), F32),) * 2,
        grid=(t // tm,),
        in_specs=[pl.BlockSpec((tm, 1), lambda i: (i, 0)), row, row],
        out_specs=(pl.BlockSpec((tm, LANES), lambda i: (i, 0)),) * 2,
        compiler_params=_cp("parallel"),
        name="rope_tables",
    )(pos_col, freq_row, sign_row)


def _in_proj_kernel(x_ref, w_ref, cos_ref, sin_ref, qo_ref, ko_ref, vt_ref, rw_ref, s5_ref):
    p = jnp.dot(x_ref[...].astype(BF16), w_ref[...], preferred_element_type=F32)
    cos = cos_ref[...]
    sin = sin_ref[...]
    lane = lax.broadcasted_iota(jnp.int32, cos.shape, 1)
    low = (lane % DA_HEAD_DIM) < (DA_HEAD_DIM // 2)

    def rope(t):
        swapped = jnp.where(low, pltpu.roll(t, LANES - 32, axis=1), pltpu.roll(t, 32, axis=1))
        return t * cos + swapped * sin

    scale = DA_HEAD_DIM ** -0.5 * LOG2E
    for h in range(DA_HEADS):
        sl = slice(h * LANES, (h + 1) * LANES)
        qo_ref[:, sl] = (rope(p[:, COL_Q + h * LANES:COL_Q + (h + 1) * LANES]) * scale).astype(BF16)
        ko_ref[:, sl] = rope(p[:, COL_K + h * LANES:COL_K + (h + 1) * LANES]).astype(BF16)
    vt_ref[:, 0] = p[:, COL_V:COL_V + DA_WIDTH].T.astype(BF16).reshape(DA_HEADS, DA_V_DIM, p.shape[0])
    s5_ref[...] = p[:, COL_S5:COL_S5 + S5_WIDTH]
    rw_ref[...] = p[:, COL_RW:COL_RW + RW_IN]


def _in_proj(x, w, cos_t, sin_t, seq, tm):
    t, d = x.shape
    nt = seq // tm
    tab = pl.BlockSpec((tm, LANES), lambda i: (i, 0))
    row = lambda n: pl.BlockSpec((tm, n), lambda i: (i, 0))
    return pl.pallas_call(
        _in_proj_kernel,
        out_shape=(jax.ShapeDtypeStruct((t, DA_WIDTH), BF16), jax.ShapeDtypeStruct((t, DA_WIDTH), BF16),
                   jax.ShapeDtypeStruct((t // seq * DA_HEADS, nt, DA_V_DIM, tm), BF16),
                   jax.ShapeDtypeStruct((t, RW_IN), F32), jax.ShapeDtypeStruct((t, S5_WIDTH), F32)),
        grid=(t // tm,),
        in_specs=[row(d), pl.BlockSpec(w.shape, lambda i: (0, 0)), tab, tab],
        out_specs=(row(DA_WIDTH), row(DA_WIDTH),
                   pl.BlockSpec((DA_HEADS, 1, DA_V_DIM, tm), lambda i: (i // nt, i % nt, 0, 0)),
                   row(RW_IN), row(S5_WIDTH)),
        compiler_params=_cp("parallel"),
        name="in_proj",
    )(x, w, cos_t, sin_t)


ATT_TQ = 1024
ATT_TK = 512
ATT_CW = 256
ATT_KB = ATT_TQ // ATT_TK


def _attn_kernel(lam_ref, g_ref, q_ref, k_ref, vt_ref, o_ref, qs_ref, p_ref, acc_ref, *, lambda_init):
    i = pl.program_id(2)
    nq = 2 * ATT_TQ
    q = q_ref[...]
    lane = lax.broadcasted_iota(jnp.int32, q.shape, 1)
    zero = jnp.zeros_like(q)
    qs_ref[:ATT_TQ, :] = jnp.where(lane < DA_HEAD_DIM, q, zero)
    qs_ref[ATT_TQ:, :] = jnp.where(lane >= DA_HEAD_DIM, q, zero)
    p_ref[...] = jnp.zeros_like(p_ref)
    acc_ref[...] = jnp.zeros_like(acc_ref)

    def visible_rows(diag, c):
        if diag is None:
            return ATT_TK
        first_query = (c * ATT_CW) % ATT_TQ
        return max(0, min(ATT_TK, first_query + ATT_CW - diag * ATT_TK))

    def pv_update(c, vt, alpha_prev, rows):
        if rows == 0:
            return
        cs = slice(c * ATT_CW, (c + 1) * ATT_CW)
        pv = jnp.dot(vt[:, :rows], p_ref[c, :rows, :], preferred_element_type=F32)
        acc_ref[c] = alpha_prev[:, cs] * acc_ref[c] + pv

    def block(j, carry, diag, prev_diag):
        m, l, alpha_prev = carry
        kb = k_ref[pl.ds(pl.multiple_of(j * ATT_TK, ATT_TK), ATT_TK), :]
        vt = vt_ref[jnp.maximum(j - 1, 0)]
        ms, ls, alphas = [], [], []
        for c in range(nq // ATT_CW):
            cs = slice(c * ATT_CW, (c + 1) * ATT_CW)
            rows = visible_rows(diag, c)
            if rows:
                s = lax.dot_general(kb[:rows], qs_ref[cs, :], (((1,), (1,)), ((), ())),
                                    preferred_element_type=F32)
            if rows and diag is not None:
                krow = lax.broadcasted_iota(jnp.int32, s.shape, 0) + diag * ATT_TK
                qcol = (lax.broadcasted_iota(jnp.int32, s.shape, 1) + c * ATT_CW) % ATT_TQ
                s = jnp.where(krow // CHUNK <= qcol // CHUNK, s, -jnp.inf)
            pv_update(c, vt, alpha_prev, visible_rows(prev_diag, c))
            if not rows:
                ms.append(m[:, cs])
                ls.append(l[:, cs])
                alphas.append(jnp.ones_like(m[:, cs]))
                continue
            m_new = jnp.maximum(m[:, cs], jnp.max(s, axis=0, keepdims=True))
            alpha = jnp.exp2(m[:, cs] - m_new)
            p = jnp.exp2(s - m_new)
            ls.append(alpha * l[:, cs] + jnp.sum(p, axis=0, keepdims=True))
            p_ref[c, :rows, :] = p.astype(BF16)
            ms.append(m_new)
            alphas.append(alpha)
        return jnp.concatenate(ms, axis=1), jnp.concatenate(ls, axis=1), jnp.concatenate(alphas, axis=1)

    def trip(t, carry, masked):
        for d in range(ATT_KB):
            diag = d if masked else None
            prev_diag = d - 1 if masked and d > 0 else None
            carry = block(ATT_KB * t + d, carry, diag, prev_diag)
        return carry

    init = (jnp.full((1, nq), -jnp.inf, F32), jnp.zeros((1, nq), F32), jnp.ones((1, nq), F32))
    carry = lax.fori_loop(0, i, lambda t, c: trip(t, c, False), init)
    m, l, alpha = trip(i, carry, True)
    last = ATT_KB * i + ATT_KB - 1
    vt = vt_ref[last]
    for c in range(nq // ATT_CW):
        pv_update(c, vt, alpha, visible_rows(ATT_KB - 1, c))
    o = jnp.concatenate([acc_ref[c] for c in range(nq // ATT_CW)], axis=1) / l
    o = o[:, :ATT_TQ] - lam_ref[...] * o[:, ATT_TQ:]
    o = o * lax.rsqrt(jnp.mean(o * o, axis=0, keepdims=True) + RMS_EPS) * g_ref[...]
    o_ref[...] = (o * (1.0 - lambda_init)).T


def _attention(qr, kr, vt, lam, subln_g, lambda_init, bsz, seq):
    nq = seq // ATT_TQ
    assert vt.shape == (bsz * DA_HEADS, seq // ATT_TK, DA_V_DIM, ATT_TK)
    col = pl.BlockSpec((LANES, 1), lambda b, h, i: (0, 0))
    qo = pl.BlockSpec((ATT_TQ, LANES), lambda b, h, i: (b * nq + i, h))
    return pl.pallas_call(
        functools.partial(_attn_kernel, lambda_init=lambda_init),
        out_shape=jax.ShapeDtypeStruct((bsz * seq, DA_WIDTH), F32),
        grid=(bsz, DA_HEADS, nq),
        in_specs=[pl.BlockSpec((1, 1), lambda b, h, i: (0, 0)), col, qo,
                  pl.BlockSpec((seq, LANES), lambda b, h, i: (b, h)),
                  pl.BlockSpec((None, seq // ATT_TK, DA_V_DIM, ATT_TK), lambda b, h, i: (b * DA_HEADS + h, 0, 0, 0))],
        out_specs=qo,
        scratch_shapes=[pltpu.VMEM((2 * ATT_TQ, LANES), BF16),
                        pltpu.VMEM((2 * ATT_TQ // ATT_CW, ATT_TK, ATT_CW), BF16),
                        pltpu.VMEM((2 * ATT_TQ // ATT_CW, DA_V_DIM, ATT_CW), F32)],
        compiler_params=_cp("parallel", "parallel", "arbitrary"),
        name="diff_attention",
    )(lam.reshape(1, 1), subln_g.reshape(LANES, 1), qr, kr, vt)


RW_VLO = 4
RW_VHI = RW_HEAD_DIM // RW_VLO
RW_BH = LANES // RW_VLO
RW_TB = 128
RW_PITCH = RW_HEAD_DIM + SUBLANES
RW_TT = 64
RW_NACC = 4
N_KOPS = 4
RW_NORM = 64


def _rw_prep_kernel(p_ref, halo_ref, mu_ref, w0_ref, a0_ref, kk_ref, ka_ref, rk_ref,
                    w2_ref, a2_ref, g2_ref, seg_ref, tri_ref,
                    kop_ref, v_ref, dend_ref, g_ref, bonus_ref, tr_ref, de_ref, *, n_blocks):
    i = pl.program_id(0)
    n_op = pl.program_id(1)
    bsz = p_ref.shape[0]
    blk = jnp.minimum(i, n_blocks - 1)
    cur = i % 2
    prv = 1 - cur
    first = jnp.where(blk == 0, 0.0, 1.0)
    seg = seg_ref[...]
    tri = tri_ref[...]

    @pl.when((i == 0) & (n_op == 0))
    def _():
        tr_ref[1] = jnp.zeros(tr_ref.shape[1:], F32)
        de_ref[...] = jnp.zeros_like(de_ref)

    def compute(b):
        p = p_ref[b]
        row = lax.broadcasted_iota(jnp.int32, p.shape, 0)
        last_prev = jnp.broadcast_to(halo_ref[b, SUBLANES - 1:SUBLANES, :], p.shape) * first
        prev = jnp.where(row == 0, last_prev, pltpu.roll(p, 1, axis=0))
        p = p + (prev - p) * mu_ref[...]
        r = p[:, 0:RW_WIDTH]
        k = p[:, RW_WIDTH:2 * RW_WIDTH]
        v = p[:, 2 * RW_WIDTH:3 * RW_WIDTH]
        low = p[:, 3 * RW_WIDTH:RW_IN]
        log_w = -RW_DECAY_SCALE * jax.nn.sigmoid(w0_ref[...] + _dot_split(jnp.tanh(low), w2_ref))
        a = jax.nn.sigmoid(a0_ref[...] + _dot_split(low, a2_ref))
        g_ref[b] = _dot_split(jax.nn.sigmoid(low), g2_ref)
        kk = k * kk_ref[...]
        kk = kk * lax.rsqrt(jnp.maximum(_segment_sum(kk * kk, seg), 1e-24))
        k = k * (1.0 + (a - 1.0) * ka_ref[...])
        bonus_ref[b] = _segment_sum(r * k * rk_ref[...], seg) * v
        log_d = sum(jnp.dot(tri, piece, preferred_element_type=F32) for piece in _split3(log_w))
        d_inc = jnp.exp(log_d)
        d_inv = jnp.exp(-log_d)
        d_exc = jnp.exp(log_d - log_w)
        for n, val in enumerate((r * d_inc, k * d_inv, -kk * d_exc, kk * a * d_inv, v)):
            vt = val.T
            for h in range(RW_HEADS):
                base = pl.multiple_of((b * RW_HEADS + h) * RW_PITCH, SUBLANES)
                tr_ref[cur, n, pl.ds(base, RW_HEAD_DIM), :] = vt[h * RW_HEAD_DIM:(h + 1) * RW_HEAD_DIM, :]
        for e in range(RW_TB // RW_NORM):
            end = d_inc[(e + 1) * RW_NORM - 1:(e + 1) * RW_NORM, :]
            for h in range(RW_HEADS):
                de_ref[cur, e, pl.ds(b * RW_HEADS + h, 1), 0:RW_HEAD_DIM] = end[:, h * RW_HEAD_DIM:(h + 1) * RW_HEAD_DIM]

    def v_relayout():
        for vh in range(RW_VHI):
            rows = [tr_ref[cur, N_KOPS, pl.ds(vh * RW_VLO + vl, RW_BH, stride=RW_PITCH), :]
                    for vl in range(RW_VLO)]
            v_ref[pl.ds(vh, RW_TB, stride=RW_VHI), :] = jnp.concatenate(rows, axis=0).T
        for e in range(RW_TB // RW_NORM):
            tile = jnp.concatenate([de_ref[cur, e]] * RW_VLO, axis=0).T
            dend_ref[e] = tile[:RW_HEAD_DIM, :]

    def emit():
        for kk_i in range(RW_HEAD_DIM):
            rows = tr_ref[prv, n_op, pl.ds(kk_i, RW_BH, stride=RW_PITCH), :]
            tile = jnp.concatenate([rows] * RW_VLO, axis=0).T
            kop_ref[0, :, kk_i] = tile.reshape(RW_TB // SUBLANES, SUBLANES, LANES)

    per_step = bsz // N_KOPS

    @pl.when(n_op < N_KOPS - 1)
    def _():
        emit()
        for s in range(per_step):
            compute(n_op * per_step + s)

    @pl.when(n_op == N_KOPS - 1)
    def _():
        emit()
        for s in range(per_step):
            compute(n_op * per_step + s)
        v_relayout()


def _head_segments():
    idx = np.arange(RW_WIDTH) // RW_HEAD_DIM
    return jnp.asarray((idx[:, None] == idx[None, :]).astype(np.float32)).astype(BF16)


def _rw_prep(proj3, mu, w0, w2, a0, a2, g2, k_k, k_a, r_k):
    bsz, seq, _ = proj3.shape
    hb = RW_TB // SUBLANES
    w2p = _hi_lo(jnp.zeros((LANES, RW_WIDTH), F32).at[0:32].set(w2))
    a2p = _hi_lo(jnp.zeros((LANES, RW_WIDTH), F32).at[32:64].set(a2))
    g2p = _hi_lo(jnp.zeros((LANES, RW_WIDTH), F32).at[64:128].set(g2))
    lowrank = pl.BlockSpec((2, LANES, RW_WIDTH), lambda i, j: (0, 0, 0))
    vec = lambda n: pl.BlockSpec((1, n), lambda i, j: (0, 0))
    mat = lambda r, c: pl.BlockSpec((r, c), lambda i, j: (0, 0))
    n_blocks = seq // RW_TB
    assert bsz % N_KOPS == 0 and RW_TB % RW_NORM == 0
    t_idx = np.arange(RW_TB)
    tri = jnp.asarray(((t_idx[:, None] >= t_idx[None, :])
                       & (t_idx[:, None] // RW_NORM == t_idx[None, :] // RW_NORM)).astype(np.float32)).astype(BF16)
    blk = lambda i: jnp.minimum(i, n_blocks - 1)
    tok = pl.BlockSpec((bsz, RW_TB, RW_WIDTH), lambda i, j: (0, blk(i), 0))
    tshape = jax.ShapeDtypeStruct((bsz, seq, RW_WIDTH), F32)
    return pl.pallas_call(
        functools.partial(_rw_prep_kernel, n_blocks=n_blocks),
        out_shape=(jax.ShapeDtypeStruct((N_KOPS, (seq + RW_TB) // SUBLANES, RW_HEAD_DIM, SUBLANES, LANES), F32),
                   jax.ShapeDtypeStruct((seq * RW_VHI, LANES), F32),
                   jax.ShapeDtypeStruct((seq // RW_NORM, RW_HEAD_DIM, LANES), F32), tshape, tshape),
        grid=(n_blocks + 1, N_KOPS),
        in_specs=[pl.BlockSpec((bsz, RW_TB, RW_IN), lambda i, j: (0, blk(i), 0)),
                  pl.BlockSpec((bsz, SUBLANES, RW_IN),
                               lambda i, j: (0, jnp.maximum(blk(i) * hb - 1, 0), 0)),
                  vec(RW_IN), vec(RW_WIDTH), vec(RW_WIDTH), vec(RW_WIDTH), vec(RW_WIDTH), vec(RW_WIDTH),
                  lowrank, lowrank, lowrank, mat(RW_WIDTH, RW_WIDTH), mat(RW_TB, RW_TB)],
        out_specs=(pl.BlockSpec((1, RW_TB // SUBLANES, RW_HEAD_DIM, SUBLANES, LANES),
                                lambda i, j: (j, jnp.where(i == 0, n_blocks, i - 1), 0, 0, 0)),
                   pl.BlockSpec((RW_TB * RW_VHI, LANES), lambda i, j: (blk(i), 0)),
                   pl.BlockSpec((RW_TB // RW_NORM, RW_HEAD_DIM, LANES), lambda i, j: (blk(i), 0, 0)), tok, tok),
        scratch_shapes=[pltpu.VMEM((2, N_KOPS + 1, RW_BH * RW_PITCH, RW_TB), F32),
                        pltpu.VMEM((2, RW_TB // RW_NORM, RW_BH, LANES), F32)],
        compiler_params=_cp("arbitrary", "arbitrary"),
        name="rwkv_prep",
    )(proj3, proj3, mu.reshape(1, RW_IN), w0.reshape(1, -1), a0.reshape(1, -1), k_k.reshape(1, -1),
      k_a.reshape(1, -1), r_k.reshape(1, -1), w2p, a2p, g2p, _head_segments(), tri)


def _rw_scan_kernel(r_ref, k_ref, an_ref, b_ref, v_ref, dend_ref, y_ref, s_ref):
    @pl.when(pl.program_id(0) == 0)
    def _():
        s_ref[...] = jnp.zeros_like(s_ref)

    nvb = RW_VHI // SUBLANES
    vsl = [slice(vb * SUBLANES, (vb + 1) * SUBLANES) for vb in range(nvb)]
    zeros = lambda: [[jnp.zeros((SUBLANES, LANES), F32) for _ in range(RW_NACC)] for _ in range(nvb)]
    total = lambda acc: [sum(acc[vb][1:], acc[vb][0]) for vb in range(nvb)]

    def body(tb, _):
        def row(ref, kk, ts):
            return jnp.broadcast_to(ref[tb, kk, ts:ts + 1, :], (SUBLANES, LANES))

        zacc = zeros()
        for kk in range(RW_HEAD_DIM):
            an = row(an_ref, kk, 0)
            for vb in range(nvb):
                zacc[vb][kk % RW_NACC] += s_ref[kk, vsl[vb], :] * an
        z = total(zacc)
        for ts in range(SUBLANES):
            t = tb * SUBLANES + ts
            vt = [v_ref[t, vsl[vb], :] for vb in range(nvb)]
            yacc, zacc = zeros(), zeros()
            for kk in range(RW_HEAD_DIM):
                br, kr, rr = row(b_ref, kk, ts), row(k_ref, kk, ts), row(r_ref, kk, ts)
                an = row(an_ref, kk, ts + 1) if ts + 1 < SUBLANES else None
                for vb in range(nvb):
                    s_new = s_ref[kk, vsl[vb], :] + z[vb] * br + vt[vb] * kr
                    s_ref[kk, vsl[vb], :] = s_new
                    yacc[vb][kk % RW_NACC] += s_new * rr
                    if an is not None:
                        zacc[vb][kk % RW_NACC] += s_new * an
            for vb, y in enumerate(total(yacc)):
                y_ref[t, vsl[vb], :] = y
            z = total(zacc)
        return 0

    lax.fori_loop(0, RW_TT // SUBLANES, body, 0)
    for kk in range(RW_HEAD_DIM):
        d_end = jnp.broadcast_to(dend_ref[0, kk:kk + 1, :], (SUBLANES, LANES))
        for vb in range(nvb):
            s_ref[kk, vsl[vb], :] = s_ref[kk, vsl[vb], :] * d_end


def _rw_scan(kops, v4, dend, seq):
    assert RW_TT == RW_NORM
    kspec = lambda n: pl.BlockSpec((None, RW_TT // SUBLANES, RW_HEAD_DIM, SUBLANES, LANES),
                                   lambda i: (n, i, 0, 0, 0))
    vspec = pl.BlockSpec((RW_TT, RW_VHI, LANES), lambda i: (i, 0, 0))
    return pl.pallas_call(
        _rw_scan_kernel,
        out_shape=jax.ShapeDtypeStruct((seq, RW_VHI, LANES), F32),
        grid=(seq // RW_TT,),
        in_specs=[kspec(n) for n in range(N_KOPS)] + [vspec, pl.BlockSpec((1, RW_HEAD_DIM, LANES), lambda i: (i, 0, 0))],
        out_specs=vspec,
        scratch_shapes=[pltpu.VMEM((RW_HEAD_DIM, RW_VHI, LANES), F32)],
        compiler_params=_cp("arbitrary"),
        name="rwkv_scan",
    )(*([kops] * N_KOPS), v4.reshape(seq, RW_VHI, LANES), dend)


def _rw_post_kernel(y_ref, bonus_ref, gate_ref, lg_ref, lb_ref, seg_ref, o_ref, tr_ref):
    bsz = bonus_ref.shape[0]

    for vh in range(RW_VHI):
        yt = y_ref[pl.ds(vh, RW_TB, stride=RW_VHI), :].T
        for vl in range(RW_VLO):
            tr_ref[pl.ds(vh * RW_VLO + vl, RW_BH, stride=RW_PITCH), :] = yt[vl * RW_BH:(vl + 1) * RW_BH, :]

    seg = seg_ref[...]
    head_mean = lambda a: _segment_sum(a, seg) * (1.0 / RW_HEAD_DIM)

    for b in range(bsz):
        slabs = [tr_ref[pl.ds((b * RW_HEADS + h) * RW_PITCH, RW_HEAD_DIM), :] for h in range(RW_HEADS)]
        y = jnp.concatenate(slabs, axis=0).T
        c = y - head_mean(y)
        var = head_mean(c * c)
        y = c * lax.rsqrt(var + RW_LNX_EPS) * lg_ref[...] + lb_ref[...]
        o_ref[b] = (y + bonus_ref[b]) * gate_ref[b]


def _rw_post(y4, bonus, gate, lnx_g, lnx_b):
    bsz, seq, _ = bonus.shape
    tok = pl.BlockSpec((bsz, RW_TB, RW_WIDTH), lambda i: (0, i, 0))
    vec = pl.BlockSpec((1, RW_WIDTH), lambda i: (0, 0))
    return pl.pallas_call(
        _rw_post_kernel,
        out_shape=jax.ShapeDtypeStruct((bsz, seq, RW_WIDTH), F32),
        grid=(seq // RW_TB,),
        in_specs=[pl.BlockSpec((RW_TB * RW_VHI, LANES), lambda i: (i, 0)), tok, tok, vec, vec,
                  pl.BlockSpec((RW_WIDTH, RW_WIDTH), lambda i: (0, 0))],
        out_specs=tok,
        scratch_shapes=[pltpu.VMEM((RW_BH * RW_PITCH, RW_TB), F32)],
        compiler_params=_cp("parallel"),
        name="rwkv_post",
    )(y4.reshape(seq * RW_VHI, LANES), bonus, gate, lnx_g.reshape(1, -1), lnx_b.reshape(1, -1), _head_segments())


S5_TT = 256
S5_PITCH = S5_TT + SUBLANES


def _s5_kernel(u_ref, wb_ref, wc_ref, a_ref, d_ref, gw_ref, gb_ref, og_ref, o_ref, x_ref, carry_ref):
    @pl.when(pl.program_id(0) == 0)
    def _():
        carry_ref[...] = jnp.zeros_like(carry_ref)

    n = S5_CPLX
    nc = n // LANES
    bsz = u_ref.shape[0]

    def project(b, _):
        base = b * S5_PITCH
        bu = jnp.dot(u_ref[b].astype(BF16), wb_ref[...], preferred_element_type=F32)
        for c in range(2 * nc):
            x_ref[c, pl.ds(base, S5_TT), :] = bu[:, c * LANES:(c + 1) * LANES]
        return 0
    for b in range(bsz):
        project(b, 0)

    ar = [jnp.broadcast_to(a_ref[:, c * LANES:(c + 1) * LANES], (bsz, LANES)) for c in range(nc)]
    ai = [jnp.broadcast_to(a_ref[:, n + c * LANES:n + (c + 1) * LANES], (bsz, LANES)) for c in range(nc)]

    def steps(tb, carry):
        cr, ci = list(carry[0]), list(carry[1])
        for ts in range(SUBLANES):
            rows = pl.ds(tb * SUBLANES + ts, bsz, stride=S5_PITCH)
            for c in range(nc):
                cr[c], ci[c] = (ar[c] * cr[c] - ai[c] * ci[c] + x_ref[c, rows, :],
                                ar[c] * ci[c] + ai[c] * cr[c] + x_ref[nc + c, rows, :])
                x_ref[c, rows, :] = cr[c]
                x_ref[nc + c, rows, :] = ci[c]
        return tuple(cr), tuple(ci)

    init = (tuple(carry_ref[c] for c in range(nc)), tuple(carry_ref[nc + c] for c in range(nc)))
    cr, ci = lax.fori_loop(0, S5_TT // SUBLANES, steps, init)
    for c in range(nc):
        carry_ref[c] = cr[c]
        carry_ref[nc + c] = ci[c]

    def readout(b, _):
        base = b * S5_PITCH
        u = u_ref[b]
        x = jnp.concatenate([x_ref[c, pl.ds(base, S5_TT), :] for c in range(2 * nc)], axis=1)
        y = jnp.dot(x.astype(BF16), wc_ref[...], preferred_element_type=F32) + d_ref[...] * u
        z = jax.nn.gelu(y)
        z = z * jax.nn.sigmoid(jnp.dot(z.astype(BF16), gw_ref[...], preferred_element_type=F32) + gb_ref[...])
        o_ref[b] = z * lax.rsqrt(jnp.mean(z * z, axis=-1, keepdims=True) + RMS_EPS) * og_ref[...]
        return 0
    for b in range(bsz):
        readout(b, 0)


def _s5_tables(a_re, a_im, b_re, b_im, c_re, c_im, log_step):
    delta = jnp.exp(log_step.astype(F32))[:, None]
    mag = jnp.exp(delta * a_re)

    def a_pow(n):
        return (mag ** n) * jnp.cos(n * delta * a_im), (mag ** n) * jnp.sin(n * delta * a_im)

    abr, abi = a_pow(1)
    den = a_re * a_re + a_im * a_im
    qr = ((abr - 1.0) * a_re + abi * a_im) / den
    qi = (abi * a_re - (abr - 1.0) * a_im) / den
    bbr = qr[..., None] * b_re - qi[..., None] * b_im
    bbi = qr[..., None] * b_im + qi[..., None] * b_re
    eye = jnp.eye(S5_GROUPS, dtype=F32)
    blk_b = lambda m: jnp.einsum('gph,gk->ghkp', m, eye).reshape(S5_WIDTH, S5_CPLX)
    blk_c = lambda m: jnp.einsum('ghp,gk->gpkh', m, eye).reshape(S5_CPLX, S5_WIDTH)
    wb = jnp.concatenate([blk_b(bbr), blk_b(bbi)], axis=1).astype(BF16)
    wc = jnp.concatenate([blk_c(c_re), -blk_c(c_im)], axis=0).astype(BF16)
    a_row = jnp.concatenate([abr.reshape(1, -1), abi.reshape(1, -1)], axis=1)
    return wb, wc, a_row


def _s5(proj3, tables, d_skip, glu_w, glu_b, out_g):
    wb, wc, a_row = tables
    bsz, seq, _ = proj3.shape
    full = lambda a: pl.BlockSpec(a.shape, lambda i: (0,) * a.ndim)
    vec = pl.BlockSpec((1, S5_WIDTH), lambda i: (0, 0))
    gw = glu_w.astype(BF16)
    return pl.pallas_call(
        _s5_kernel,
        out_shape=jax.ShapeDtypeStruct((bsz, seq, S5_WIDTH), F32),
        grid=(seq // S5_TT,),
        in_specs=[pl.BlockSpec((bsz, S5_TT, S5_WIDTH), lambda i: (0, i, 0)),
                  full(wb), full(wc), full(a_row), vec, full(gw), vec, vec],
        out_specs=pl.BlockSpec((bsz, S5_TT, S5_WIDTH), lambda i: (0, i, 0)),
        scratch_shapes=[pltpu.VMEM((2 * S5_CPLX // LANES, bsz * S5_PITCH, LANES), F32),
                        pltpu.VMEM((2 * S5_CPLX // LANES, bsz, LANES), F32)],
        compiler_params=_cp("arbitrary"),
        name="s5",
    )(proj3, wb, wc, a_row, d_skip.reshape(1, -1), gw, glu_b.reshape(1, -1), out_g.reshape(1, -1))


def _xattn_kernel(hda_ref, hrw_ref, hs5_ref, x_ref, wda_ref, wrw_ref, ws5_ref, g1_ref, b1_ref,
                  kv_ref, wq_ref, wo_ref, g2_ref, b2_ref, o_ref):
    mix = ALPHA * x_ref[...]
    for h_ref, w_ref in ((hda_ref, wda_ref), (hrw_ref, wrw_ref), (hs5_ref, ws5_ref)):
        mix = mix + jnp.dot(h_ref[...].astype(BF16), w_ref[...], preferred_element_type=F32)
    x = _layer_norm_rows(mix, g1_ref[...], b1_ref[...])
    q = jnp.dot(x.astype(BF16), wq_ref[...], preferred_element_type=F32)
    outs = []
    for h in range(MEM_HEADS):
        sl = slice(h * MEM_HEAD_DIM, (h + 1) * MEM_HEAD_DIM)
        kh = kv_ref[:, sl]
        vh = kv_ref[:, D_MODEL + h * MEM_HEAD_DIM:D_MODEL + (h + 1) * MEM_HEAD_DIM]
        s = lax.dot_general(q[:, sl].astype(BF16), kh, (((1,), (1,)), ((), ())),
                            preferred_element_type=F32) * (MEM_HEAD_DIM ** -0.5)
        p = jnp.exp(s - jnp.max(s, axis=-1, keepdims=True))
        p = p / jnp.sum(p, axis=-1, keepdims=True)
        outs.append(jnp.dot(p.astype(BF16), vh, preferred_element_type=F32))
    o = jnp.concatenate(outs, axis=-1)
    h_out = jnp.dot(o.astype(BF16), wo_ref[...], preferred_element_type=F32)
    o_ref[...] = _layer_norm_rows(ALPHA * x + h_out, g2_ref[...], b2_ref[...])


def _mix_out_cross_attention(h_list, w_list, x, g1, b1, kv, wq, wo, g2, b2, seq, tm):
    t, d = x.shape
    n_mem = kv.shape[0] // (t // seq)
    tiles_per_seq = seq // tm
    const = lambda a: pl.BlockSpec(a.shape, lambda i: (0, 0))
    vec = pl.BlockSpec((1, d), lambda i: (0, 0))
    row = lambda n: pl.BlockSpec((tm, n), lambda i: (i, 0))
    return pl.pallas_call(
        _xattn_kernel,
        out_shape=jax.ShapeDtypeStruct((t, d), F32),
        grid=(t // tm,),
        in_specs=[row(h.shape[1]) for h in h_list] + [row(d)] + [const(w) for w in w_list] + [vec, vec]
        + [pl.BlockSpec((n_mem, 2 * d), lambda i: (i // tiles_per_seq, 0)), const(wq), const(wo), vec, vec],
        out_specs=row(d),
        compiler_params=_cp("parallel"),
        name="mix_out_cross_attention",
    )(*h_list, x, *w_list, g1.reshape(1, d), b1.reshape(1, d), kv, wq, wo, g2.reshape(1, d), b2.reshape(1, d))


FFN_TN = 1408
FFN_TM = 1024


def _ffn_up_kernel(x_ref, halo_ref, wa_ref, wg_ref, cw_ref, cb_ref, o_ref, *, tiles_per_seq):
    i = pl.program_id(1)
    x = x_ref[...].astype(BF16)
    first = jnp.where(i % tiles_per_seq == 0, 0.0, 1.0)
    xh = (halo_ref[...] * first).astype(BF16)
    a8 = jnp.dot(jnp.concatenate([xh, x], axis=0), wa_ref[...], preferred_element_type=F32)
    a0 = a8[SUBLANES:]
    a1 = pltpu.roll(a8, 1, axis=0)[SUBLANES:]
    a2 = pltpu.roll(a8, 2, axis=0)[SUBLANES:]
    conv = cb_ref[...] + cw_ref[0:1, :] * a2 + cw_ref[1:2, :] * a1 + cw_ref[2:3, :] * a0
    g = jnp.dot(x, wg_ref[...], preferred_element_type=F32)
    o_ref[...] = (jax.nn.silu(conv) * g).astype(o_ref.dtype)


def _ffn_up(x, wa, wg, conv_w, conv_b, seq, tm):
    t, d = x.shape
    hb = tm // SUBLANES
    cw = jnp.zeros((SUBLANES, D_FF), F32).at[0:3].set(conv_w)
    wspec = pl.BlockSpec((d, FFN_TN), lambda j, i: (0, j))
    return pl.pallas_call(
        functools.partial(_ffn_up_kernel, tiles_per_seq=seq // tm),
        out_shape=jax.ShapeDtypeStruct((t, D_FF), BF16),
        grid=(D_FF // FFN_TN, t // tm),
        in_specs=[pl.BlockSpec((tm, d), lambda j, i: (i, 0)),
                  pl.BlockSpec((SUBLANES, d), lambda j, i: (jnp.maximum(i * hb - 1, 0), 0)),
                  wspec, wspec,
                  pl.BlockSpec((SUBLANES, FFN_TN), lambda j, i: (0, j)),
                  pl.BlockSpec((1, FFN_TN), lambda j, i: (0, j))],
        out_specs=pl.BlockSpec((tm, FFN_TN), lambda j, i: (i, j)),
        compiler_params=_cp("arbitrary", "parallel"),
        name="ffn_up",
    )(x, x, wa, wg, cw, conv_b.reshape(1, D_FF))


def kernel(x, mem, positions, w_in, da_lam_q1, da_lam_k1, da_lam_q2, da_lam_k2, da_subln_g, rw_mu, rw_w0, rw_w2, rw_a0, rw_a2, rw_g2, rw_k_k, rw_k_a, rw_r_k, rw_lnx_g, rw_lnx_b, s5_a_re, s5_a_im, s5_b_re, s5_b_im, s5_c_re, s5_c_im, s5_d, s5_log_step, s5_glu_w, s5_glu_b, s5_out_g, w_out, ln1_g, ln1_b, ca_wq, ca_wkv, ca_wo, ln2_g, ln2_b, ffn_w_up, ffn_conv_w, ffn_conv_b, ffn_w_down, ln3_g, ln3_b):
    bsz, seq, d = x.shape
    t = bsz * seq
    tm = min(512, seq)
    n_mem = mem.shape[1]
    xf = x.reshape(t, d)
    memf = mem.reshape(bsz * n_mem, d)
    cos_t, sin_t = _rope_tables(positions, tm)

    for l in range(DEPTH):
        lambda_init = 0.8 - 0.6 * math.exp(-0.3 * l)
        wi = w_in[l]
        w_perm = jnp.concatenate([wi[:, :3 * DA_WIDTH], wi[:, 3 * DA_WIDTH + RW_IN:],
                                  wi[:, 3 * DA_WIDTH:3 * DA_WIDTH + RW_IN]], axis=1).astype(BF16)
        qr, kr, vt, p_rw, p_s5 = _in_proj(xf, w_perm, cos_t, sin_t, seq, tm)
        lam = (jnp.exp(jnp.sum(da_lam_q1[l] * da_lam_k1[l])) - jnp.exp(jnp.sum(da_lam_q2[l] * da_lam_k2[l]))
               + lambda_init)
        kops, v4, dend, gate, bonus = _rw_prep(p_rw.reshape(bsz, seq, RW_IN), rw_mu[l], rw_w0[l], rw_w2[l],
                                               rw_a0[l], rw_a2[l], rw_g2[l], rw_k_k[l], rw_k_a[l],
                                               rw_r_k[l].reshape(-1))
        h_da = _attention(qr, kr, vt, lam, da_subln_g[l], lambda_init, bsz, seq)
        y4 = _rw_scan(kops, v4, dend, seq)
        h_rw = _rw_post(y4, bonus, gate, rw_lnx_g[l], rw_lnx_b[l]).reshape(t, RW_WIDTH)

        tables = _s5_tables(s5_a_re[l], s5_a_im[l], s5_b_re[l], s5_b_im[l], s5_c_re[l], s5_c_im[l],
                            s5_log_step[l])
        h_s5 = _s5(p_s5.reshape(bsz, seq, S5_WIDTH), tables, s5_d[l], s5_glu_w[l], s5_glu_b[l],
                   s5_out_g[l]).reshape(t, S5_WIDTH)

        wo = w_out[l].astype(BF16)
        kv = _matmul(memf, ca_wkv[l].astype(BF16), BF16, n_mem, 1024)
        xf = _mix_out_cross_attention(
            [h_da, h_rw, h_s5], [wo[:DA_WIDTH], wo[DA_WIDTH:DA_WIDTH + RW_WIDTH], wo[DA_WIDTH + RW_WIDTH:]],
            xf, ln1_g[l], ln1_b[l], kv, ca_wq[l].astype(BF16), ca_wo[l].astype(BF16), ln2_g[l], ln2_b[l], seq, tm)
        w_up = ffn_w_up[l].astype(BF16)
        hff = _ffn_up(xf, w_up[:, :D_FF], w_up[:, D_FF:], ffn_conv_w[l], ffn_conv_b[l], seq, min(FFN_TM, seq))
        xf = _matmul_ln([hff], [ffn_w_down[l].astype(BF16)], xf, ln3_g[l], ln3_b[l], tm)
    return xf.reshape(bsz, seq, d)
```

```python
import functools
import math

import numpy as np
import jax
import jax.numpy as jnp
from jax import lax
from jax.experimental import pallas as pl
from jax.experimental.pallas import tpu as pltpu

F32 = jnp.float32
BF16 = jnp.bfloat16
HI = lax.Precision.HIGHEST

D_MODEL = 1024
DEPTH = 4
CHUNK = 64
LN_EPS = 1e-5
LOG2E = math.log2(math.e)
RMS_EPS = 1e-6

DA_HEAD_DIM = 64
DA_V_DIM = 128
DA_WIDTH = 512
DA_HEADS = 4
ROPE_THETA = 10000.0

RW_HEAD_DIM = 64
RW_WIDTH = 256
RW_HEADS = 4
RW_DECAY_RANK = 32
RW_AAA_RANK = 32
RW_GATE_RANK = 64
RW_IN = 3 * RW_WIDTH + RW_DECAY_RANK + RW_AAA_RANK + RW_GATE_RANK
RW_DECAY_SCALE = math.exp(-0.5)
RW_LNX_EPS = 64e-5

S5_WIDTH = 256
S5_GROUP_CH = 16
S5_GROUPS = 16
S5_STATE = 64
S5_CPLX = S5_GROUPS * S5_STATE

IN_WIDTH = 3 * DA_WIDTH + RW_IN + S5_WIDTH
MEM_HEADS = 4
MEM_HEAD_DIM = 256
D_FF = 2816
ALPHA = (2.0 * DEPTH) ** 0.25

COL_Q, COL_K, COL_V, COL_S5, COL_RW = 0, 512, 1024, 1536, 1792

VMEM_LIMIT = 48 * 1024 * 1024
XATTN_TM = 1024
XATTN_VMEM_LIMIT = 58 * 1024 * 1024
LANES = 128
SUBLANES = 8


def _cp(*sem):
    return pltpu.CompilerParams(dimension_semantics=sem, vmem_limit_bytes=VMEM_LIMIT)


def _layer_norm_rows(v, g, b):
    mu = jnp.mean(v, axis=-1, keepdims=True)
    c = v - mu
    var = jnp.mean(c * c, axis=-1, keepdims=True)
    return c * lax.rsqrt(var + LN_EPS) * g + b


def _split3(x):
    p1 = x.astype(BF16)
    r1 = x - p1.astype(F32)
    p2 = r1.astype(BF16)
    p3 = (r1 - p2.astype(F32)).astype(BF16)
    return p1, p2, p3


def _segment_sum(x, seg):
    return sum(jnp.dot(p, seg, preferred_element_type=F32) for p in _split3(x))


def _dot_split(a, w_ref):
    a1, a2, _ = _split3(a)
    d = functools.partial(jnp.dot, preferred_element_type=F32)
    return d(a1, w_ref[0]) + (d(a1, w_ref[1]) + d(a2, w_ref[0]))


def _hi_lo(w):
    hi = w.astype(BF16)
    return jnp.stack([hi, (w - hi.astype(F32)).astype(BF16)])


def _mm_kernel(a_ref, w_ref, o_ref):
    o_ref[...] = jnp.dot(a_ref[...].astype(BF16), w_ref[...],
                         preferred_element_type=F32).astype(o_ref.dtype)


def _matmul(a, w, out_dtype, tm, tn):
    m, k = a.shape
    n = w.shape[1]
    return pl.pallas_call(
        _mm_kernel,
        out_shape=jax.ShapeDtypeStruct((m, n), out_dtype),
        grid=(m // tm, n // tn),
        in_specs=[pl.BlockSpec((tm, k), lambda i, j: (i, 0)),
                  pl.BlockSpec((k, tn), lambda i, j: (0, j))],
        out_specs=pl.BlockSpec((tm, tn), lambda i, j: (i, j)),
        compiler_params=_cp("parallel", "arbitrary"),
        name="matmul",
    )(a, w)


def _mm_ln_kernel(n_in, *refs):
    a_refs = refs[:n_in]
    w_refs = refs[n_in:2 * n_in]
    x_ref, g_ref, b_ref, o_ref = refs[2 * n_in:]
    acc = ALPHA * x_ref[...]
    for a_ref, w_ref in zip(a_refs, w_refs):
        acc = acc + jnp.dot(a_ref[...].astype(BF16), w_ref[...], preferred_element_type=F32)
    o_ref[...] = _layer_norm_rows(acc, g_ref[...], b_ref[...])


def _matmul_ln(a_list, w_list, x, g, b, tm):
    m, d = x.shape
    n_in = len(a_list)
    in_specs = [pl.BlockSpec((tm, a.shape[1]), lambda i: (i, 0)) for a in a_list]
    in_specs += [pl.BlockSpec(w.shape, lambda i: (0, 0)) for w in w_list]
    in_specs += [pl.BlockSpec((tm, d), lambda i: (i, 0)),
                 pl.BlockSpec((1, d), lambda i: (0, 0)),
                 pl.BlockSpec((1, d), lambda i: (0, 0))]
    return pl.pallas_call(
        functools.partial(_mm_ln_kernel, n_in),
        out_shape=jax.ShapeDtypeStruct((m, d), F32),
        grid=(m // tm,),
        in_specs=in_specs,
        out_specs=pl.BlockSpec((tm, d), lambda i: (i, 0)),
        compiler_params=_cp("parallel"),
        name="matmul_ln",
    )(*a_list, *w_list, x, g.reshape(1, d), b.reshape(1, d))


def _rope_table_kernel(pos_ref, freq_ref, sign_ref, cos_ref, sin_ref):
    ang = pos_ref[...] * freq_ref[...]
    cos_ref[...] = jnp.cos(ang)
    sin_ref[...] = jnp.sin(ang) * sign_ref[...]


def _rope_tables(positions, tm):
    t = positions.size
    inv_freq = ROPE_THETA ** (-jnp.arange(0, DA_HEAD_DIM, 2, dtype=F32) / DA_HEAD_DIM)
    freq_row = jnp.tile(inv_freq, 4).reshape(1, LANES)
    sign_row = jnp.tile(jnp.concatenate([-jnp.ones((32,), F32), jnp.ones((32,), F32)]), 2).reshape(1, LANES)
    pos_col = positions.astype(F32).reshape(t, 1)
    row = pl.BlockSpec((1, LANES), lambda i: (0, 0))
    return pl.pallas_call(
        _rope_table_kernel,
        out_shape=(jax.ShapeDtypeStruct((t, LANES), F32),) * 2,
        grid=(t // tm,),
        in_specs=[pl.BlockSpec((tm, 1), lambda i: (i, 0)), row, row],
        out_specs=(pl.BlockSpec((tm, LANES), lambda i: (i, 0)),) * 2,
        compiler_params=_cp("parallel"),
        name="rope_tables",
    )(pos_col, freq_row, sign_row)


def _in_proj_kernel(x_ref, w_ref, cos_ref, sin_ref, qo_ref, ko_ref, vt_ref, rw_ref, s5_ref):
    p = jnp.dot(x_ref[...].astype(BF16), w_ref[...], preferred_element_type=F32)
    cos = cos_ref[...]
    sin = sin_ref[...]
    lane = lax.broadcasted_iota(jnp.int32, cos.shape, 1)
    low = (lane % DA_HEAD_DIM) < (DA_HEAD_DIM // 2)

    def rope(t):
        swapped = jnp.where(low, pltpu.roll(t, LANES - 32, axis=1), pltpu.roll(t, 32, axis=1))
        return t * cos + swapped * sin

    scale = DA_HEAD_DIM ** -0.5 * LOG2E
    for h in range(DA_HEADS):
        sl = slice(h * LANES, (h + 1) * LANES)
        qo_ref[:, sl] = (rope(p[:, COL_Q + h * LANES:COL_Q + (h + 1) * LANES]) * scale).astype(BF16)
        ko_ref[:, sl] = rope(p[:, COL_K + h * LANES:COL_K + (h + 1) * LANES]).astype(BF16)
    vt_ref[:, 0] = p[:, COL_V:COL_V + DA_WIDTH].T.astype(BF16).reshape(DA_HEADS, DA_V_DIM, p.shape[0])
    s5_ref[...] = p[:, COL_S5:COL_S5 + S5_WIDTH]
    rw_ref[...] = p[:, COL_RW:COL_RW + RW_IN]


def _in_proj(x, w, cos_t, sin_t, seq, tm):
    t, d = x.shape
    nt = seq // tm
    tab = pl.BlockSpec((tm, LANES), lambda i: (i, 0))
    row = lambda n: pl.BlockSpec((tm, n), lambda i: (i, 0))
    return pl.pallas_call(
        _in_proj_kernel,
        out_shape=(jax.ShapeDtypeStruct((t, DA_WIDTH), BF16), jax.ShapeDtypeStruct((t, DA_WIDTH), BF16),
                   jax.ShapeDtypeStruct((t // seq * DA_HEADS, nt, DA_V_DIM, tm), BF16),
                   jax.ShapeDtypeStruct((t, RW_IN), F32), jax.ShapeDtypeStruct((t, S5_WIDTH), F32)),
        grid=(t // tm,),
        in_specs=[row(d), pl.BlockSpec(w.shape, lambda i: (0, 0)), tab, tab],
        out_specs=(row(DA_WIDTH), row(DA_WIDTH),
                   pl.BlockSpec((DA_HEADS, 1, DA_V_DIM, tm), lambda i: (i // nt, i % nt, 0, 0)),
                   row(RW_IN), row(S5_WIDTH)),
        compiler_params=_cp("parallel"),
        name="in_proj",
    )(x, w, cos_t, sin_t)


ATT_TQ = 1024
ATT_TK = 512
ATT_CW = 256
ATT_KB = ATT_TQ // ATT_TK


def _attn_kernel(lam_ref, g_ref, q_ref, k_ref, vt_ref, o_ref, qs_ref, p_ref, acc_ref, *, lambda_init):
    i = pl.program_id(2)
    nq = 2 * ATT_TQ
    q = q_ref[...]
    lane = lax.broadcasted_iota(jnp.int32, q.shape, 1)
    zero = jnp.zeros_like(q)
    qs_ref[:ATT_TQ, :] = jnp.where(lane < DA_HEAD_DIM, q, zero)
    qs_ref[ATT_TQ:, :] = jnp.where(lane >= DA_HEAD_DIM, q, zero)
    p_ref[...] = jnp.zeros_like(p_ref)
    acc_ref[...] = jnp.zeros_like(acc_ref)

    def visible_rows(diag, c):
        if diag is None:
            return ATT_TK
        first_query = (c * ATT_CW) % ATT_TQ
        return max(0, min(ATT_TK, first_query + ATT_CW - diag * ATT_TK))

    def pv_update(c, vt, alpha_prev, rows):
        if rows == 0:
            return
        cs = slice(c * ATT_CW, (c + 1) * ATT_CW)
        pv = jnp.dot(vt[:, :rows], p_ref[c, :rows, :], preferred_element_type=F32)
        acc_ref[c] = alpha_prev[:, cs] * acc_ref[c] + pv

    def block(j, carry, diag, prev_diag):
        m, l, alpha_prev = carry
        kb = k_ref[pl.ds(pl.multiple_of(j * ATT_TK, ATT_TK), ATT_TK), :]
        vt = vt_ref[jnp.maximum(j - 1, 0)]
        ms, ls, alphas = [], [], []
        for c in range(nq // ATT_CW):
            cs = slice(c * ATT_CW, (c + 1) * ATT_CW)
            rows = visible_rows(diag, c)
            if rows:
                s = lax.dot_general(kb[:rows], qs_ref[cs, :], (((1,), (1,)), ((), ())),
                                    preferred_element_type=F32)
            if rows and diag is not None:
                krow = lax.broadcasted_iota(jnp.int32, s.shape, 0) + diag * ATT_TK
                qcol = (lax.broadcasted_iota(jnp.int32, s.shape, 1) + c * ATT_CW) % ATT_TQ
                s = jnp.where(krow // CHUNK <= qcol // CHUNK, s, -jnp.inf)
            pv_update(c, vt, alpha_prev, visible_rows(prev_diag, c))
            if not rows:
                ms.append(m[:, cs])
                ls.append(l[:, cs])
                alphas.append(jnp.ones_like(m[:, cs]))
                continue
            m_new = jnp.maximum(m[:, cs], jnp.max(s, axis=0, keepdims=True))
            alpha = jnp.exp2(m[:, cs] - m_new)
            p = jnp.exp2(s - m_new)
            ls.append(alpha * l[:, cs] + jnp.sum(p, axis=0, keepdims=True))
            p_ref[c, :rows, :] = p.astype(BF16)
            ms.append(m_new)
            alphas.append(alpha)
        return jnp.concatenate(ms, axis=1), jnp.concatenate(ls, axis=1), jnp.concatenate(alphas, axis=1)

    def trip(t, carry, masked):
        for d in range(ATT_KB):
            diag = d if masked else None
            prev_diag = d - 1 if masked and d > 0 else None
            carry = block(ATT_KB * t + d, carry, diag, prev_diag)
        return carry

    init = (jnp.full((1, nq), -jnp.inf, F32), jnp.zeros((1, nq), F32), jnp.ones((1, nq), F32))
    carry = lax.fori_loop(0, i, lambda t, c: trip(t, c, False), init)
    m, l, alpha = trip(i, carry, True)
    last = ATT_KB * i + ATT_KB - 1
    vt = vt_ref[last]
    for c in range(nq // ATT_CW):
        pv_update(c, vt, alpha, visible_rows(ATT_KB - 1, c))
    o = jnp.concatenate([acc_ref[c] for c in range(nq // ATT_CW)], axis=1) / l
    o = o[:, :ATT_TQ] - lam_ref[...] * o[:, ATT_TQ:]
    o = o * lax.rsqrt(jnp.mean(o * o, axis=0, keepdims=True) + RMS_EPS) * g_ref[...]
    o_ref[...] = (o * (1.0 - lambda_init)).T


def _attention(qr, kr, vt, lam, subln_g, lambda_init, bsz, seq):
    nq = seq // ATT_TQ
    assert vt.shape == (bsz * DA_HEADS, seq // ATT_TK, DA_V_DIM, ATT_TK)
    col = pl.BlockSpec((LANES, 1), lambda b, h, i: (0, 0))
    qo = pl.BlockSpec((ATT_TQ, LANES), lambda b, h, i: (b * nq + i, h))
    return pl.pallas_call(
        functools.partial(_attn_kernel, lambda_init=lambda_init),
        out_shape=jax.ShapeDtypeStruct((bsz * seq, DA_WIDTH), F32),
        grid=(bsz, DA_HEADS, nq),
        in_specs=[pl.BlockSpec((1, 1), lambda b, h, i: (0, 0)), col, qo,
                  pl.BlockSpec((seq, LANES), lambda b, h, i: (b, h)),
                  pl.BlockSpec((None, seq // ATT_TK, DA_V_DIM, ATT_TK), lambda b, h, i: (b * DA_HEADS + h, 0, 0, 0))],
        out_specs=qo,
        scratch_shapes=[pltpu.VMEM((2 * ATT_TQ, LANES), BF16),
                        pltpu.VMEM((2 * ATT_TQ // ATT_CW, ATT_TK, ATT_CW), BF16),
                        pltpu.VMEM((2 * ATT_TQ // ATT_CW, DA_V_DIM, ATT_CW), F32)],
        compiler_params=_cp("parallel", "parallel", "arbitrary"),
        name="diff_attention",
    )(lam.reshape(1, 1), subln_g.reshape(LANES, 1), qr, kr, vt)


RW_VLO = 4
RW_VHI = RW_HEAD_DIM // RW_VLO
RW_BH = LANES // RW_VLO
RW_TB = 128
RW_PITCH = RW_HEAD_DIM + SUBLANES
RW_TT = 64
RW_NACC = 4
N_KOPS = 4
RW_NORM = 64


def _rw_prep_kernel(p_ref, halo_ref, mu_ref, w0_ref, a0_ref, kk_ref, ka_ref, rk_ref,
                    w2_ref, a2_ref, g2_ref, seg_ref, tri_ref,
                    kop_ref, v_ref, dend_ref, g_ref, bonus_ref, tr_ref, de_ref, *, n_blocks):
    i = pl.program_id(0)
    n_op = pl.program_id(1)
    bsz = p_ref.shape[0]
    blk = jnp.minimum(i, n_blocks - 1)
    cur = i % 2
    prv = 1 - cur
    first = jnp.where(blk == 0, 0.0, 1.0)
    seg = seg_ref[...]
    tri = tri_ref[...]

    @pl.when((i == 0) & (n_op == 0))
    def _():
        tr_ref[1] = jnp.zeros(tr_ref.shape[1:], F32)
        de_ref[...] = jnp.zeros_like(de_ref)

    def compute(b):
        p = p_ref[b]
        row = lax.broadcasted_iota(jnp.int32, p.shape, 0)
        last_prev = jnp.broadcast_to(halo_ref[b, SUBLANES - 1:SUBLANES, :], p.shape) * first
        prev = jnp.where(row == 0, last_prev, pltpu.roll(p, 1, axis=0))
        p = p + (prev - p) * mu_ref[...]
        r = p[:, 0:RW_WIDTH]
        k = p[:, RW_WIDTH:2 * RW_WIDTH]
        v = p[:, 2 * RW_WIDTH:3 * RW_WIDTH]
        low = p[:, 3 * RW_WIDTH:RW_IN]
        log_w = -RW_DECAY_SCALE * jax.nn.sigmoid(w0_ref[...] + _dot_split(jnp.tanh(low), w2_ref))
        a = jax.nn.sigmoid(a0_ref[...] + _dot_split(low, a2_ref))
        g_ref[b] = _dot_split(jax.nn.sigmoid(low), g2_ref)
        kk = k * kk_ref[...]
        kk = kk * lax.rsqrt(jnp.maximum(_segment_sum(kk * kk, seg), 1e-24))
        k = k * (1.0 + (a - 1.0) * ka_ref[...])
        bonus_ref[b] = _segment_sum(r * k * rk_ref[...], seg) * v
        log_d = sum(jnp.dot(tri, piece, preferred_element_type=F32) for piece in _split3(log_w))
        d_inc = jnp.exp(log_d)
        d_inv = jnp.exp(-log_d)
        d_exc = jnp.exp(log_d - log_w)
        for n, val in enumerate((r * d_inc, k * d_inv, -kk * d_exc, kk * a * d_inv, v)):
            vt = val.T
            for h in range(RW_HEADS):
                base = pl.multiple_of((b * RW_HEADS + h) * RW_PITCH, SUBLANES)
                tr_ref[cur, n, pl.ds(base, RW_HEAD_DIM), :] = vt[h * RW_HEAD_DIM:(h + 1) * RW_HEAD_DIM, :]
        for e in range(RW_TB // RW_NORM):
            end = d_inc[(e + 1) * RW_NORM - 1:(e + 1) * RW_NORM, :]
            for h in range(RW_HEADS):
                de_ref[cur, e, pl.ds(b * RW_HEADS + h, 1), 0:RW_HEAD_DIM] = end[:, h * RW_HEAD_DIM:(h + 1) * RW_HEAD_DIM]

    def v_relayout():
        for vh in range(RW_VHI):
            rows = [tr_ref[cur, N_KOPS, pl.ds(vh * RW_VLO + vl, RW_BH, stride=RW_PITCH), :]
                    for vl in range(RW_VLO)]
            v_ref[pl.ds(vh, RW_TB, stride=RW_VHI), :] = jnp.concatenate(rows, axis=0).T
        for e in range(RW_TB // RW_NORM):
            tile = jnp.concatenate([de_ref[cur, e]] * RW_VLO, axis=0).T
            dend_ref[e] = tile[:RW_HEAD_DIM, :]

    def emit():
        for kk_i in range(RW_HEAD_DIM):
            rows = tr_ref[prv, n_op, pl.ds(kk_i, RW_BH, stride=RW_PITCH), :]
            tile = jnp.concatenate([rows] * RW_VLO, axis=0).T
            kop_ref[0, :, kk_i] = tile.reshape(RW_TB // SUBLANES, SUBLANES, LANES)

    per_step = bsz // N_KOPS

    @pl.when(n_op < N_KOPS - 1)
    def _():
        emit()
        for s in range(per_step):
            compute(n_op * per_step + s)

    @pl.when(n_op == N_KOPS - 1)
    def _():
        emit()
        for s in range(per_step):
            compute(n_op * per_step + s)
        v_relayout()


def _head_segments():
    idx = np.arange(RW_WIDTH) // RW_HEAD_DIM
    return jnp.asarray((idx[:, None] == idx[None, :]).astype(np.float32)).astype(BF16)


def _rw_prep(proj3, mu, w0, w2, a0, a2, g2, k_k, k_a, r_k):
    bsz, seq, _ = proj3.shape
    hb = RW_TB // SUBLANES
    w2p = _hi_lo(jnp.zeros((LANES, RW_WIDTH), F32).at[0:32].set(w2))
    a2p = _hi_lo(jnp.zeros((LANES, RW_WIDTH), F32).at[32:64].set(a2))
    g2p = _hi_lo(jnp.zeros((LANES, RW_WIDTH), F32).at[64:128].set(g2))
    lowrank = pl.BlockSpec((2, LANES, RW_WIDTH), lambda i, j: (0, 0, 0))
    vec = lambda n: pl.BlockSpec((1, n), lambda i, j: (0, 0))
    mat = lambda r, c: pl.BlockSpec((r, c), lambda i, j: (0, 0))
    n_blocks = seq // RW_TB
    assert bsz % N_KOPS == 0 and RW_TB % RW_NORM == 0
    t_idx = np.arange(RW_TB)
    tri = jnp.asarray(((t_idx[:, None] >= t_idx[None, :])
                       & (t_idx[:, None] // RW_NORM == t_idx[None, :] // RW_NORM)).astype(np.float32)).astype(BF16)
    blk = lambda i: jnp.minimum(i, n_blocks - 1)
    tok = pl.BlockSpec((bsz, RW_TB, RW_WIDTH), lambda i, j: (0, blk(i), 0))
    tshape = jax.ShapeDtypeStruct((bsz, seq, RW_WIDTH), F32)
    return pl.pallas_call(
        functools.partial(_rw_prep_kernel, n_blocks=n_blocks),
        out_shape=(jax.ShapeDtypeStruct((N_KOPS, (seq + RW_TB) // SUBLANES, RW_HEAD_DIM, SUBLANES, LANES), F32),
                   jax.ShapeDtypeStruct((seq * RW_VHI, LANES), F32),
                   jax.ShapeDtypeStruct((seq // RW_NORM, RW_HEAD_DIM, LANES), F32), tshape, tshape),
        grid=(n_blocks + 1, N_KOPS),
        in_specs=[pl.BlockSpec((bsz, RW_TB, RW_IN), lambda i, j: (0, blk(i), 0)),
                  pl.BlockSpec((bsz, SUBLANES, RW_IN),
                               lambda i, j: (0, jnp.maximum(blk(i) * hb - 1, 0), 0)),
                  vec(RW_IN), vec(RW_WIDTH), vec(RW_WIDTH), vec(RW_WIDTH), vec(RW_WIDTH), vec(RW_WIDTH),
                  lowrank, lowrank, lowrank, mat(RW_WIDTH, RW_WIDTH), mat(RW_TB, RW_TB)],
        out_specs=(pl.BlockSpec((1, RW_TB // SUBLANES, RW_HEAD_DIM, SUBLANES, LANES),
                                lambda i, j: (j, jnp.where(i == 0, n_blocks, i - 1), 0, 0, 0)),
                   pl.BlockSpec((RW_TB * RW_VHI, LANES), lambda i, j: (blk(i), 0)),
                   pl.BlockSpec((RW_TB // RW_NORM, RW_HEAD_DIM, LANES), lambda i, j: (blk(i), 0, 0)), tok, tok),
        scratch_shapes=[pltpu.VMEM((2, N_KOPS + 1, RW_BH * RW_PITCH, RW_TB), F32),
                        pltpu.VMEM((2, RW_TB // RW_NORM, RW_BH, LANES), F32)],
        compiler_params=_cp("arbitrary", "arbitrary"),
        name="rwkv_prep",
    )(proj3, proj3, mu.reshape(1, RW_IN), w0.reshape(1, -1), a0.reshape(1, -1), k_k.reshape(1, -1),
      k_a.reshape(1, -1), r_k.reshape(1, -1), w2p, a2p, g2p, _head_segments(), tri)


def _rw_scan_kernel(r_ref, k_ref, an_ref, b_ref, v_ref, dend_ref, y_ref, s_ref):
    @pl.when(pl.program_id(0) == 0)
    def _():
        s_ref[...] = jnp.zeros_like(s_ref)

    nvb = RW_VHI // SUBLANES
    vsl = [slice(vb * SUBLANES, (vb + 1) * SUBLANES) for vb in range(nvb)]
    zeros = lambda: [[jnp.zeros((SUBLANES, LANES), F32) for _ in range(RW_NACC)] for _ in range(nvb)]
    total = lambda acc: [sum(acc[vb][1:], acc[vb][0]) for vb in range(nvb)]

    def body(tb, _):
        def row(ref, kk, ts):
            return jnp.broadcast_to(ref[tb, kk, ts:ts + 1, :], (SUBLANES, LANES))

        zacc = zeros()
        for kk in range(RW_HEAD_DIM):
            an = row(an_ref, kk, 0)
            for vb in range(nvb):
                zacc[vb][kk % RW_NACC] += s_ref[kk, vsl[vb], :] * an
        z = total(zacc)
        for ts in range(SUBLANES):
            t = tb * SUBLANES + ts
            vt = [v_ref[t, vsl[vb], :] for vb in range(nvb)]
            yacc, zacc = zeros(), zeros()
            for kk in range(RW_HEAD_DIM):
                br, kr, rr = row(b_ref, kk, ts), row(k_ref, kk, ts), row(r_ref, kk, ts)
                an = row(an_ref, kk, ts + 1) if ts + 1 < SUBLANES else None
                for vb in range(nvb):
                    s_new = s_ref[kk, vsl[vb], :] + z[vb] * br + vt[vb] * kr
                    s_ref[kk, vsl[vb], :] = s_new
                    yacc[vb][kk % RW_NACC] += s_new * rr
                    if an is not None:
                        zacc[vb][kk % RW_NACC] += s_new * an
            for vb, y in enumerate(total(yacc)):
                y_ref[t, vsl[vb], :] = y
            z = total(zacc)
        return 0

    lax.fori_loop(0, RW_TT // SUBLANES, body, 0)
    for kk in range(RW_HEAD_DIM):
        d_end = jnp.broadcast_to(dend_ref[0, kk:kk + 1, :], (SUBLANES, LANES))
        for vb in range(nvb):
            s_ref[kk, vsl[vb], :] = s_ref[kk, vsl[vb], :] * d_end


def _rw_scan(kops, v4, dend, seq):
    assert RW_TT == RW_NORM
    kspec = lambda n: pl.BlockSpec((None, RW_TT // SUBLANES, RW_HEAD_DIM, SUBLANES, LANES),
                                   lambda i: (n, i, 0, 0, 0))
    vspec = pl.BlockSpec((RW_TT, RW_VHI, LANES), lambda i: (i, 0, 0))
    return pl.pallas_call(
        _rw_scan_kernel,
        out_shape=jax.ShapeDtypeStruct((seq, RW_VHI, LANES), F32),
        grid=(seq // RW_TT,),
        in_specs=[kspec(n) for n in range(N_KOPS)] + [vspec, pl.BlockSpec((1, RW_HEAD_DIM, LANES), lambda i: (i, 0, 0))],
        out_specs=vspec,
        scratch_shapes=[pltpu.VMEM((RW_HEAD_DIM, RW_VHI, LANES), F32)],
        compiler_params=_cp("arbitrary"),
        name="rwkv_scan",
    )(*([kops] * N_KOPS), v4.reshape(seq, RW_VHI, LANES), dend)


def _rw_post_kernel(y_ref, bonus_ref, gate_ref, lg_ref, lb_ref, seg_ref, o_ref, tr_ref):
    bsz = bonus_ref.shape[0]

    for vh in range(RW_VHI):
        yt = y_ref[pl.ds(vh, RW_TB, stride=RW_VHI), :].T
        for vl in range(RW_VLO):
            tr_ref[pl.ds(vh * RW_VLO + vl, RW_BH, stride=RW_PITCH), :] = yt[vl * RW_BH:(vl + 1) * RW_BH, :]

    seg = seg_ref[...]
    head_mean = lambda a: _segment_sum(a, seg) * (1.0 / RW_HEAD_DIM)

    for b in range(bsz):
        slabs = [tr_ref[pl.ds((b * RW_HEADS + h) * RW_PITCH, RW_HEAD_DIM), :] for h in range(RW_HEADS)]
        y = jnp.concatenate(slabs, axis=0).T
        c = y - head_mean(y)
        var = head_mean(c * c)
        y = c * lax.rsqrt(var + RW_LNX_EPS) * lg_ref[...] + lb_ref[...]
        o_ref[b] = (y + bonus_ref[b]) * gate_ref[b]


def _rw_post(y4, bonus, gate, lnx_g, lnx_b):
    bsz, seq, _ = bonus.shape
    tok = pl.BlockSpec((bsz, RW_TB, RW_WIDTH), lambda i: (0, i, 0))
    vec = pl.BlockSpec((1, RW_WIDTH), lambda i: (0, 0))
    return pl.pallas_call(
        _rw_post_kernel,
        out_shape=jax.ShapeDtypeStruct((bsz, seq, RW_WIDTH), F32),
        grid=(seq // RW_TB,),
        in_specs=[pl.BlockSpec((RW_TB * RW_VHI, LANES), lambda i: (i, 0)), tok, tok, vec, vec,
                  pl.BlockSpec((RW_WIDTH, RW_WIDTH), lambda i: (0, 0))],
        out_specs=tok,
        scratch_shapes=[pltpu.VMEM((RW_BH * RW_PITCH, RW_TB), F32)],
        compiler_params=_cp("parallel"),
        name="rwkv_post",
    )(y4.reshape(seq * RW_VHI, LANES), bonus, gate, lnx_g.reshape(1, -1), lnx_b.reshape(1, -1), _head_segments())


S5_TT = 256
S5_PITCH = S5_TT + SUBLANES


def _s5_kernel(u_ref, wb_ref, wc_ref, a_ref, d_ref, gw_ref, gb_ref, og_ref, o_ref, x_ref, carry_ref):
    @pl.when(pl.program_id(0) == 0)
    def _():
        carry_ref[...] = jnp.zeros_like(carry_ref)

    n = S5_CPLX
    nc = n // LANES
    bsz = u_ref.shape[0]

    def project(b, _):
        base = b * S5_PITCH
        bu = jnp.dot(u_ref[b].astype(BF16), wb_ref[...], preferred_element_type=F32)
        for c in range(2 * nc):
            x_ref[c, pl.ds(base, S5_TT), :] = bu[:, c * LANES:(c + 1) * LANES]
        return 0
    for b in range(bsz):
        project(b, 0)

    ar = [jnp.broadcast_to(a_ref[:, c * LANES:(c + 1) * LANES], (bsz, LANES)) for c in range(nc)]
    ai = [jnp.broadcast_to(a_ref[:, n + c * LANES:n + (c + 1) * LANES], (bsz, LANES)) for c in range(nc)]

    def steps(tb, carry):
        cr, ci = list(carry[0]), list(carry[1])
        for ts in range(SUBLANES):
            rows = pl.ds(tb * SUBLANES + ts, bsz, stride=S5_PITCH)
            for c in range(nc):
                cr[c], ci[c] = (ar[c] * cr[c] - ai[c] * ci[c] + x_ref[c, rows, :],
                                ar[c] * ci[c] + ai[c] * cr[c] + x_ref[nc + c, rows, :])
                x_ref[c, rows, :] = cr[c]
                x_ref[nc + c, rows, :] = ci[c]
        return tuple(cr), tuple(ci)

    init = (tuple(carry_ref[c] for c in range(nc)), tuple(carry_ref[nc + c] for c in range(nc)))
    cr, ci = lax.fori_loop(0, S5_TT // SUBLANES, steps, init)
    for c in range(nc):
        carry_ref[c] = cr[c]
        carry_ref[nc + c] = ci[c]

    def readout(b, _):
        base = b * S5_PITCH
        u = u_ref[b]
        x = jnp.concatenate([x_ref[c, pl.ds(base, S5_TT), :] for c in range(2 * nc)], axis=1)
        y = jnp.dot(x.astype(BF16), wc_ref[...], preferred_element_type=F32) + d_ref[...] * u
        z = jax.nn.gelu(y)
        z = z * jax.nn.sigmoid(jnp.dot(z.astype(BF16), gw_ref[...], preferred_element_type=F32) + gb_ref[...])
        o_ref[b] = z * lax.rsqrt(jnp.mean(z * z, axis=-1, keepdims=True) + RMS_EPS) * og_ref[...]
        return 0
    for b in range(bsz):
        readout(b, 0)


def _s5_tables(a_re, a_im, b_re, b_im, c_re, c_im, log_step):
    delta = jnp.exp(log_step.astype(F32))[:, None]
    mag = jnp.exp(delta * a_re)

    def a_pow(n):
        return (mag ** n) * jnp.cos(n * delta * a_im), (mag ** n) * jnp.sin(n * delta * a_im)

    abr, abi = a_pow(1)
    den = a_re * a_re + a_im * a_im
    qr = ((abr - 1.0) * a_re + abi * a_im) / den
    qi = (abi * a_re - (abr - 1.0) * a_im) / den
    bbr = qr[..., None] * b_re - qi[..., None] * b_im
    bbi = qr[..., None] * b_im + qi[..., None] * b_re
    eye = jnp.eye(S5_GROUPS, dtype=F32)
    blk_b = lambda m: jnp.einsum('gph,gk->ghkp', m, eye).reshape(S5_WIDTH, S5_CPLX)
    blk_c = lambda m: jnp.einsum('ghp,gk->gpkh', m, eye).reshape(S5_CPLX, S5_WIDTH)
    wb = jnp.concatenate([blk_b(bbr), blk_b(bbi)], axis=1).astype(BF16)
    wc = jnp.concatenate([blk_c(c_re), -blk_c(c_im)], axis=0).astype(BF16)
    a_row = jnp.concatenate([abr.reshape(1, -1), abi.reshape(1, -1)], axis=1)
    return wb, wc, a_row


def _s5(proj3, tables, d_skip, glu_w, glu_b, out_g):
    wb, wc, a_row = tables
    bsz, seq, _ = proj3.shape
    full = lambda a: pl.BlockSpec(a.shape, lambda i: (0,) * a.ndim)
    vec = pl.BlockSpec((1, S5_WIDTH), lambda i: (0, 0))
    gw = glu_w.astype(BF16)
    return pl.pallas_call(
        _s5_kernel,
        out_shape=jax.ShapeDtypeStruct((bsz, seq, S5_WIDTH), F32),
        grid=(seq // S5_TT,),
        in_specs=[pl.BlockSpec((bsz, S5_TT, S5_WIDTH), lambda i: (0, i, 0)),
                  full(wb), full(wc), full(a_row), vec, full(gw), vec, vec],
        out_specs=pl.BlockSpec((bsz, S5_TT, S5_WIDTH), lambda i: (0, i, 0)),
        scratch_shapes=[pltpu.VMEM((2 * S5_CPLX // LANES, bsz * S5_PITCH, LANES), F32),
                        pltpu.VMEM((2 * S5_CPLX // LANES, bsz, LANES), F32)],
        compiler_params=_cp("arbitrary"),
        name="s5",
    )(proj3, wb, wc, a_row, d_skip.reshape(1, -1), gw, glu_b.reshape(1, -1), out_g.reshape(1, -1))


def _xattn_kernel(hda_ref, hrw_ref, hs5_ref, x_ref, wda_ref, wrw_ref, ws5_ref, g1_ref, b1_ref,
                  kv_ref, wq_ref, wo_ref, g2_ref, b2_ref, o_ref):
    mix = ALPHA * x_ref[...]
    for h_ref, w_ref in ((hda_ref, wda_ref), (hrw_ref, wrw_ref), (hs5_ref, ws5_ref)):
        mix = mix + jnp.dot(h_ref[...].astype(BF16), w_ref[...], preferred_element_type=F32)
    x = _layer_norm_rows(mix, g1_ref[...], b1_ref[...])
    q = jnp.dot(x.astype(BF16), wq_ref[...], preferred_element_type=F32)
    outs = []
    for h in range(MEM_HEADS):
        sl = slice(h * MEM_HEAD_DIM, (h + 1) * MEM_HEAD_DIM)
        kh = kv_ref[:, sl]
        vh = kv_ref[:, D_MODEL + h * MEM_HEAD_DIM:D_MODEL + (h + 1) * MEM_HEAD_DIM]
        s = lax.dot_general(q[:, sl].astype(BF16), kh, (((1,), (1,)), ((), ())),
                            preferred_element_type=F32) * (MEM_HEAD_DIM ** -0.5)
        p = jnp.exp(s - jnp.max(s, axis=-1, keepdims=True))
        p = p / jnp.sum(p, axis=-1, keepdims=True)
        outs.append(jnp.dot(p.astype(BF16), vh, preferred_element_type=F32))
    o = jnp.concatenate(outs, axis=-1)
    h_out = jnp.dot(o.astype(BF16), wo_ref[...], preferred_element_type=F32)
    o_ref[...] = _layer_norm_rows(ALPHA * x + h_out, g2_ref[...], b2_ref[...])


def _mix_out_cross_attention(h_list, w_list, x, g1, b1, kv, wq, wo, g2, b2, seq, tm):
    t, d = x.shape
    n_mem = kv.shape[0] // (t // seq)
    tiles_per_seq = seq // tm
    const = lambda a: pl.BlockSpec(a.shape, lambda i: (0, 0), pipeline_mode=pl.Buffered(1))
    vec = pl.BlockSpec((1, d), lambda i: (0, 0))
    row = lambda n: pl.BlockSpec((tm, n), lambda i: (i, 0))
    return pl.pallas_call(
        _xattn_kernel,
        out_shape=jax.ShapeDtypeStruct((t, d), F32),
        grid=(t // tm,),
        in_specs=[row(h.shape[1]) for h in h_list] + [row(d)] + [const(w) for w in w_list] + [vec, vec]
        + [pl.BlockSpec((n_mem, 2 * d), lambda i: (i // tiles_per_seq, 0)), const(wq), const(wo), vec, vec],
        out_specs=row(d),
        compiler_params=pltpu.CompilerParams(dimension_semantics=("parallel",), vmem_limit_bytes=XATTN_VMEM_LIMIT),
        name="mix_out_cross_attention",
    )(*h_list, x, *w_list, g1.reshape(1, d), b1.reshape(1, d), kv, wq, wo, g2.reshape(1, d), b2.reshape(1, d))


FFN_TN = 1408
FFN_TM = 1024


def _ffn_up_kernel(x_ref, halo_ref, wa_ref, wg_ref, cw_ref, cb_ref, o_ref, *, tiles_per_seq):
    i = pl.program_id(1)
    x = x_ref[...].astype(BF16)
    first = jnp.where(i % tiles_per_seq == 0, 0.0, 1.0)
    xh = (halo_ref[...] * first).astype(BF16)
    a8 = jnp.dot(jnp.concatenate([xh, x], axis=0), wa_ref[...], preferred_element_type=F32)
    a0 = a8[SUBLANES:]
    a1 = pltpu.roll(a8, 1, axis=0)[SUBLANES:]
    a2 = pltpu.roll(a8, 2, axis=0)[SUBLANES:]
    conv = cb_ref[...] + cw_ref[0:1, :] * a2 + cw_ref[1:2, :] * a1 + cw_ref[2:3, :] * a0
    g = jnp.dot(x, wg_ref[...], preferred_element_type=F32)
    o_ref[...] = (jax.nn.silu(conv) * g).astype(o_ref.dtype)


def _ffn_up(x, wa, wg, conv_w, conv_b, seq, tm):
    t, d = x.shape
    hb = tm // SUBLANES
    cw = jnp.zeros((SUBLANES, D_FF), F32).at[0:3].set(conv_w)
    wspec = pl.BlockSpec((d, FFN_TN), lambda j, i: (0, j))
    return pl.pallas_call(
        functools.partial(_ffn_up_kernel, tiles_per_seq=seq // tm),
        out_shape=jax.ShapeDtypeStruct((t, D_FF), BF16),
        grid=(D_FF // FFN_TN, t // tm),
        in_specs=[pl.BlockSpec((tm, d), lambda j, i: (i, 0)),
                  pl.BlockSpec((SUBLANES, d), lambda j, i: (jnp.maximum(i * hb - 1, 0), 0)),
                  wspec, wspec,
                  pl.BlockSpec((SUBLANES, FFN_TN), lambda j, i: (0, j)),
                  pl.BlockSpec((1, FFN_TN), lambda j, i: (0, j))],
        out_specs=pl.BlockSpec((tm, FFN_TN), lambda j, i: (i, j)),
        compiler_params=_cp("arbitrary", "parallel"),
        name="ffn_up",
    )(x, x, wa, wg, cw, conv_b.reshape(1, D_FF))


def kernel(x, mem, positions, w_in, da_lam_q1, da_lam_k1, da_lam_q2, da_lam_k2, da_subln_g, rw_mu, rw_w0, rw_w2, rw_a0, rw_a2, rw_g2, rw_k_k, rw_k_a, rw_r_k, rw_lnx_g, rw_lnx_b, s5_a_re, s5_a_im, s5_b_re, s5_b_im, s5_c_re, s5_c_im, s5_d, s5_log_step, s5_glu_w, s5_glu_b, s5_out_g, w_out, ln1_g, ln1_b, ca_wq, ca_wkv, ca_wo, ln2_g, ln2_b, ffn_w_up, ffn_conv_w, ffn_conv_b, ffn_w_down, ln3_g, ln3_b):
    bsz, seq, d = x.shape
    t = bsz * seq
    tm = min(512, seq)
    n_mem = mem.shape[1]
    xf = x.reshape(t, d)
    memf = mem.reshape(bsz * n_mem, d)
    cos_t, sin_t = _rope_tables(positions, tm)

    for l in range(DEPTH):
        lambda_init = 0.8 - 0.6 * math.exp(-0.3 * l)
        wi = w_in[l]
        w_perm = jnp.concatenate([wi[:, :3 * DA_WIDTH], wi[:, 3 * DA_WIDTH + RW_IN:],
                                  wi[:, 3 * DA_WIDTH:3 * DA_WIDTH + RW_IN]], axis=1).astype(BF16)
        qr, kr, vt, p_rw, p_s5 = _in_proj(xf, w_perm, cos_t, sin_t, seq, tm)
        lam = (jnp.exp(jnp.sum(da_lam_q1[l] * da_lam_k1[l])) - jnp.exp(jnp.sum(da_lam_q2[l] * da_lam_k2[l]))
               + lambda_init)
        kops, v4, dend, gate, bonus = _rw_prep(p_rw.reshape(bsz, seq, RW_IN), rw_mu[l], rw_w0[l], rw_w2[l],
                                               rw_a0[l], rw_a2[l], rw_g2[l], rw_k_k[l], rw_k_a[l],
                                               rw_r_k[l].reshape(-1))
        h_da = _attention(qr, kr, vt, lam, da_subln_g[l], lambda_init, bsz, seq)
        y4 = _rw_scan(kops, v4, dend, seq)
        h_rw = _rw_post(y4, bonus, gate, rw_lnx_g[l], rw_lnx_b[l]).reshape(t, RW_WIDTH)

        tables = _s5_tables(s5_a_re[l], s5_a_im[l], s5_b_re[l], s5_b_im[l], s5_c_re[l], s5_c_im[l],
                            s5_log_step[l])
        h_s5 = _s5(p_s5.reshape(bsz, seq, S5_WIDTH), tables, s5_d[l], s5_glu_w[l], s5_glu_b[l],
                   s5_out_g[l]).reshape(t, S5_WIDTH)

        wo = w_out[l].astype(BF16)
        kv = _matmul(memf, ca_wkv[l].astype(BF16), BF16, n_mem, 1024)
        xf = _mix_out_cross_attention(
            [h_da, h_rw, h_s5], [wo[:DA_WIDTH], wo[DA_WIDTH:DA_WIDTH + RW_WIDTH], wo[DA_WIDTH + RW_WIDTH:]],
            xf, ln1_g[l], ln1_b[l], kv, ca_wq[l].astype(BF16), ca_wo[l].astype(BF16), ln2_g[l], ln2_b[l], seq, min(XATTN_TM, seq))
        w_up = ffn_w_up[l].astype(BF16)
        hff = _ffn_up(xf, w_up[:, :D_FF], w_up[:, D_FF:], ffn_conv_w[l], ffn_conv_b[l], seq, min(FFN_TM, seq))
        xf = _matmul_ln([hff], [ffn_w_down[l].astype(BF16)], xf, ln3_g[l], ln3_b[l], tm)
    return xf.reshape(bsz, seq, d)
```
